```python
import math
import jax, jax.numpy as jnp
from jax import lax
import numpy as np

D_MODEL = 1024
BATCH = 16
SEQ = 2048
DEPTH = 2
DEC_BATCH = 32
DEC_SEQ = 8
PAST_LEN = 16384
PAGE_SIZE = 128

N_A_LAYERS = DEPTH // 2
N_B_LAYERS = DEPTH - N_A_LAYERS
D_FF = 4 * D_MODEL
NORM_EPS = 1e-6
GLA_HEADS = 4
GLA_KEY_DIM = D_MODEL // 2
GLA_VALUE_DIM = D_MODEL
GLA_DK = GLA_KEY_DIM // GLA_HEADS
GLA_DV = GLA_VALUE_DIM // GLA_HEADS
GATE_RANK = 16
GATE_NORMALIZER = 16.0
GLA_CHUNK = 32
GLA_IN_DIM = 2 * GLA_KEY_DIM + 2 * GLA_VALUE_DIM + GATE_RANK
HEAD_DIM = 128
Q_HEADS = D_MODEL // HEAD_DIM
KV_HEADS = 4
GROUP = Q_HEADS // KV_HEADS
MOBA_BLOCK = 256
MOBA_TOPK = 3
Q_CHUNK = 16
ROPE_THETA = 10000.0
NEG_INF = -1e30

kernel_name = 'yoco_gla_moba_decode_step'


def rmsnorm(x, g):
    xf = x.astype(jnp.float32)
    y = xf * lax.rsqrt(jnp.mean(xf * xf, axis=-1, keepdims=True) + NORM_EPS)
    return (y * g.astype(jnp.float32)).astype(x.dtype)


def rope(x, pos):
    half = HEAD_DIM // 2
    inv = ROPE_THETA ** (-jnp.arange(0, HEAD_DIM, 2, dtype=jnp.float32) / HEAD_DIM)
    ang = pos.astype(jnp.float32)[:, None] * inv[None, :]
    cos = jnp.cos(ang)[None, :, None, :]
    sin = jnp.sin(ang)[None, :, None, :]
    xf = x.astype(jnp.float32)
    x1, x2 = xf[..., :half], xf[..., half:]
    return jnp.concatenate([x1 * cos - x2 * sin, x2 * cos + x1 * sin], axis=-1).astype(x.dtype)


def channel_mlp(h, g, w_up, w_down):
    a = jax.nn.relu(rmsnorm(h, g) @ w_up)
    return (a * a) @ w_down


def gla_recurrence(q, k, v, g, s0):
    bsz, nh, t, _ = q.shape
    c = min(GLA_CHUNK, t)
    n = -(-t // c)
    pad = n * c - t

    def prep(a):
        a = jnp.pad(a, ((0, 0), (0, 0), (0, pad), (0, 0)))
        return a.reshape(bsz, nh, n, c, a.shape[-1]).transpose(2, 0, 1, 3, 4)

    qc, kc, vc, gc = prep(q), prep(k), prep(v), prep(g)
    causal = jnp.tril(jnp.ones((c, c), dtype=bool))

    def step(s, inp):
        qi, ki, vi, gi = inp
        bcum = jnp.cumsum(gi, axis=-2)
        o_inter = jnp.einsum('bhcd,bhde->bhce', qi * jnp.exp(bcum), s)
        diff = bcum[:, :, :, None, :] - bcum[:, :, None, :, :]
        decay = jnp.exp(jnp.where(causal[:, :, None], diff, NEG_INF))
        a = jnp.einsum('bhid,bhijd,bhjd->bhij', qi, decay, ki)
        o = o_inter + jnp.einsum('bhij,bhje->bhie', a, vi)
        blast = bcum[:, :, -1:, :]
        s_new = jnp.exp(blast[:, :, 0, :])[..., None] * s + jnp.einsum(
            'bhcd,bhce->bhde', ki * jnp.exp(blast - bcum), vi)
        return s_new, o

    s_fin, o = lax.scan(step, s0, (qc, kc, vc, gc))
    o = o.transpose(1, 2, 0, 3, 4).reshape(bsz, nh, n * c, -1)[:, :, :t]
    return o, s_fin


def gla_mixer(xn, s0, w_in, w_g2, b_g, g_out, w_o):
    bsz, t, _ = xn.shape
    proj = xn @ w_in
    q, k, v, r, glr = jnp.split(
        proj, [GLA_KEY_DIM, 2 * GLA_KEY_DIM, 2 * GLA_KEY_DIM + GLA_VALUE_DIM,
               2 * GLA_KEY_DIM + 2 * GLA_VALUE_DIM], axis=-1)
    gk = jax.nn.log_sigmoid((glr @ w_g2 + b_g).astype(jnp.float32)) / GATE_NORMALIZER

    def heads(a, d):
        return a.reshape(bsz, t, GLA_HEADS, d).transpose(0, 2, 1, 3).astype(jnp.float32)

    o, s_fin = gla_recurrence(heads(q, GLA_DK) * (GLA_DK ** -0.5), heads(k, GLA_DK),
                              heads(v, GLA_DV), heads(gk, GLA_DK), s0.astype(jnp.float32))
    o = rmsnorm(o.transpose(0, 2, 1, 3), g_out).reshape(bsz, t, GLA_VALUE_DIM)
    y = (o * jax.nn.silu(r.astype(jnp.float32))).astype(xn.dtype) @ w_o
    return y, s_fin.astype(s0.dtype)


def shared_kv(h, pos, norm_kv, w_kv, g_k):
    bsz, t, _ = h.shape
    kv = rmsnorm(h, norm_kv) @ w_kv
    k, v = jnp.split(kv, 2, axis=-1)
    k = k.reshape(bsz, t, KV_HEADS, HEAD_DIM)
    v = v.reshape(bsz, t, KV_HEADS, HEAD_DIM)
    return rope(rmsnorm(k, g_k), pos), v


def moba_queries(hn, pos, w_q, g_q):
    bsz, t, _ = hn.shape
    q = (hn @ w_q).reshape(bsz, t, Q_HEADS, HEAD_DIM)
    return rope(rmsnorm(q, g_q), pos)


def moba_core(q, qpos, nb, means, gather_fn, k_own, v_own, kpos):
    scale = HEAD_DIM ** -0.5
    nc = means.shape[2]
    cand = jnp.arange(nc) < nb
    gate = jnp.einsum('bkgqd,bknd->bkgqn', q, means)
    gate = jnp.where(cand, gate, NEG_INF)
    _, idx = lax.top_k(gate, min(MOBA_TOPK, nc))
    sel_ok = cand[idx]
    k_sel, v_sel = gather_fn(idx)
    s_sel = jnp.einsum('bkgqd,bkgqsld->bkgqsl', q, k_sel) * scale
    s_sel = jnp.where(sel_ok[..., None], s_sel, NEG_INF)
    s_own = jnp.einsum('bkgqd,bkld->bkgql', q, k_own) * scale
    s_own = jnp.where(kpos[None, :] <= qpos[:, None], s_own, NEG_INF)
    n_sel = s_sel.shape[-2] * s_sel.shape[-1]
    s = jnp.concatenate([s_sel.reshape(s_sel.shape[:-2] + (n_sel,)), s_own], axis=-1)
    p = jax.nn.softmax(s, axis=-1)
    p_sel = p[..., :n_sel].reshape(s_sel.shape)
    p_own = p[..., n_sel:]
    return (jnp.einsum('bkgqsl,bkgqsld->bkgqd', p_sel, v_sel)
            + jnp.einsum('bkgql,bkld->bkgqd', p_own, v_own))


def moba_prompt(q, k, v):
    bsz, t = q.shape[0], q.shape[1]
    nblk = -(-t // MOBA_BLOCK)
    pad = nblk * MOBA_BLOCK - t
    kt = jnp.pad(k.astype(jnp.float32), ((0, 0), (0, pad), (0, 0), (0, 0))).transpose(0, 2, 1, 3)
    vt = jnp.pad(v.astype(jnp.float32), ((0, 0), (0, pad), (0, 0), (0, 0))).transpose(0, 2, 1, 3)
    kb = kt.reshape(bsz, KV_HEADS, nblk, MOBA_BLOCK, HEAD_DIM)
    vb = vt.reshape(bsz, KV_HEADS, nblk, MOBA_BLOCK, HEAD_DIM)
    means = jnp.mean(kb, axis=3)
    qg = q.astype(jnp.float32).reshape(bsz, t, KV_HEADS, GROUP, HEAD_DIM).transpose(0, 2, 3, 1, 4)
    bi = jnp.arange(bsz)[:, None, None, None, None]
    ki = jnp.arange(KV_HEADS)[None, :, None, None, None]

    def gather(idx):
        return kb[bi, ki, idx], vb[bi, ki, idx]

    def chunk(ci):
        start = ci * Q_CHUNK
        q_c = lax.dynamic_slice_in_dim(qg, start, Q_CHUNK, axis=3)
        qpos = start + jnp.arange(Q_CHUNK, dtype=jnp.int32)
        nb = start // MOBA_BLOCK
        k_own = lax.dynamic_slice_in_dim(kt, nb * MOBA_BLOCK, MOBA_BLOCK, axis=2)
        v_own = lax.dynamic_slice_in_dim(vt, nb * MOBA_BLOCK, MOBA_BLOCK, axis=2)
        kpos = nb * MOBA_BLOCK + jnp.arange(MOBA_BLOCK, dtype=jnp.int32)
        return moba_core(q_c, qpos, nb, means, gather, k_own, v_own, kpos)

    o = lax.map(chunk, jnp.arange(t // Q_CHUNK, dtype=jnp.int32))
    return o.transpose(1, 0, 4, 2, 3, 5).reshape(bsz, t, Q_HEADS * HEAD_DIM)


def moba_sample(q, k_new, v_new, cache_k, cache_v, page_table):
    db, ds = q.shape[0], q.shape[1]
    n_pages = page_table.shape[1]
    past = n_pages * PAGE_SIZE
    ppb = MOBA_BLOCK // PAGE_SIZE
    nb = past // MOBA_BLOCK
    if nb > 0:
        kp = cache_k[page_table[:, :nb * ppb]].astype(jnp.float32)
        means = jnp.mean(kp.reshape(db, nb, MOBA_BLOCK, KV_HEADS, HEAD_DIM), axis=2).transpose(0, 2, 1, 3)
    else:
        means = jnp.zeros((db, KV_HEADS, 1, HEAD_DIM), jnp.float32)
    bi = jnp.arange(db)[:, None, None, None, None, None]
    ki = jnp.arange(KV_HEADS)[None, :, None, None, None, None]

    def gather(idx):
        lp = jnp.minimum(idx[..., None] * ppb + jnp.arange(ppb, dtype=jnp.int32), n_pages - 1)
        phys = page_table[bi, lp]
        ks = cache_k[phys, :, ki].astype(jnp.float32).reshape(idx.shape + (MOBA_BLOCK, HEAD_DIM))
        vs = cache_v[phys, :, ki].astype(jnp.float32).reshape(idx.shape + (MOBA_BLOCK, HEAD_DIM))
        return ks, vs

    n_own = past - nb * MOBA_BLOCK
    own_pages = page_table[:, nb * ppb:]
    k_own_past = cache_k[own_pages].reshape(db, n_own, KV_HEADS, HEAD_DIM).astype(jnp.float32)
    v_own_past = cache_v[own_pages].reshape(db, n_own, KV_HEADS, HEAD_DIM).astype(jnp.float32)
    k_own = jnp.concatenate([k_own_past, k_new.astype(jnp.float32)], axis=1).transpose(0, 2, 1, 3)
    v_own = jnp.concatenate([v_own_past, v_new.astype(jnp.float32)], axis=1).transpose(0, 2, 1, 3)
    kpos = nb * MOBA_BLOCK + jnp.arange(n_own + ds, dtype=jnp.int32)
    qpos = past + jnp.arange(ds, dtype=jnp.int32)
    qg = q.astype(jnp.float32).reshape(db, ds, KV_HEADS, GROUP, HEAD_DIM).transpose(0, 2, 3, 1, 4)
    o = moba_core(qg, qpos, nb, means, gather, k_own, v_own, kpos)
    return o.transpose(0, 3, 1, 2, 4).reshape(db, ds, Q_HEADS * HEAD_DIM)


def setup_inputs(seed: int = 0) -> dict:
    key = jax.random.key(seed)
    ks = jax.random.split(key, 24)
    f32 = jnp.float32

    def w(k, shape, fan_in):
        return jax.random.normal(k, shape, f32) * (fan_in ** -0.5)

    def gain(k, shape):
        return 1.0 + 0.05 * jax.random.normal(k, shape, f32)

    n_pages = PAST_LEN // PAGE_SIZE
    n_used = DEC_BATCH * n_pages
    n_phys = n_used + max(1, n_used // 4)
    page_table = jax.random.permutation(ks[0], n_phys)[:n_used].reshape(DEC_BATCH, n_pages).astype(jnp.int32)
    return {
        'x_prompt': jax.random.normal(ks[1], (BATCH, SEQ, D_MODEL), f32),
        'x_sample': jax.random.normal(ks[2], (DEC_BATCH, DEC_SEQ, D_MODEL), f32),
        'state_gla': jax.random.normal(ks[3], (N_A_LAYERS, DEC_BATCH, GLA_HEADS, GLA_DK, GLA_DV), f32),
        'cache_k': jax.random.normal(ks[4], (n_phys, PAGE_SIZE, KV_HEADS, HEAD_DIM), f32),
        'cache_v': jax.random.normal(ks[5], (n_phys, PAGE_SIZE, KV_HEADS, HEAD_DIM), f32),
        'page_table': page_table,
        'w_in_a': w(ks[6], (N_A_LAYERS, D_MODEL, GLA_IN_DIM), D_MODEL),
        'w_g2': w(ks[7], (N_A_LAYERS, GATE_RANK, GLA_KEY_DIM), GATE_RANK),
        'b_g': 0.1 * jax.random.normal(ks[8], (N_A_LAYERS, GLA_KEY_DIM), f32),
        'g_gla_out': gain(ks[9], (N_A_LAYERS, GLA_DV)),
        'w_o_a': w(ks[10], (N_A_LAYERS, GLA_VALUE_DIM, D_MODEL), GLA_VALUE_DIM),
        'norm_kv': gain(ks[11], (D_MODEL,)),
        'w_kv': w(ks[12], (D_MODEL, 2 * KV_HEADS * HEAD_DIM), D_MODEL),
        'g_k': gain(ks[13], (HEAD_DIM,)),
        'w_q_b': w(ks[14], (N_B_LAYERS, D_MODEL, Q_HEADS * HEAD_DIM), D_MODEL),
        'g_q': gain(ks[15], (N_B_LAYERS, HEAD_DIM)),
        'w_o_b': w(ks[16], (N_B_LAYERS, Q_HEADS * HEAD_DIM, D_MODEL), Q_HEADS * HEAD_DIM),
        'norm_mix': gain(ks[17], (DEPTH, D_MODEL)),
        'norm_mlp': gain(ks[18], (DEPTH, D_MODEL)),
        'w_up': w(ks[19], (DEPTH, D_MODEL, D_FF), D_MODEL),
        'w_down': w(ks[20], (DEPTH, D_FF, D_MODEL), D_FF),
    }


def reference(x_prompt, x_sample, state_gla, cache_k, cache_v, page_table,
              w_in_a, w_g2, b_g, g_gla_out, w_o_a, norm_kv, w_kv, g_k,
              w_q_b, g_q, w_o_b, norm_mix, norm_mlp, w_up, w_down):
    bsz_p, t_p, _ = x_prompt.shape
    pos_p = jnp.arange(t_p, dtype=jnp.int32)
    past = page_table.shape[1] * PAGE_SIZE
    pos_s = past + jnp.arange(x_sample.shape[1], dtype=jnp.int32)
    hp, hs = x_prompt, x_sample
    st_p, st_s = [], []
    for layer in range(DEPTH):
        if layer < N_A_LAYERS:
            s0 = jnp.zeros((bsz_p,) + state_gla.shape[2:], state_gla.dtype)
            yp, sp = gla_mixer(rmsnorm(hp, norm_mix[layer]), s0, w_in_a[layer], w_g2[layer],
                               b_g[layer], g_gla_out[layer], w_o_a[layer])
            ys, ss = gla_mixer(rmsnorm(hs, norm_mix[layer]), state_gla[layer], w_in_a[layer], w_g2[layer],
                               b_g[layer], g_gla_out[layer], w_o_a[layer])
            st_p.append(sp)
            st_s.append(ss)
        else:
            if layer == N_A_LAYERS:
                k_p, v_p = shared_kv(hp, pos_p, norm_kv, w_kv, g_k)
                k_s, v_s = shared_kv(hs, pos_s, norm_kv, w_kv, g_k)
            j = layer - N_A_LAYERS
            qp = moba_queries(rmsnorm(hp, norm_mix[layer]), pos_p, w_q_b[j], g_q[j])
            qs = moba_queries(rmsnorm(hs, norm_mix[layer]), pos_s, w_q_b[j], g_q[j])
            yp = moba_prompt(qp, k_p, v_p).astype(hp.dtype) @ w_o_b[j]
            ys = moba_sample(qs, k_s, v_s, cache_k, cache_v, page_table).astype(hs.dtype) @ w_o_b[j]
        hp = hp + yp
        hs = hs + ys
        hp = hp + channel_mlp(hp, norm_mlp[layer], w_up[layer], w_down[layer])
        hs = hs + channel_mlp(hs, norm_mlp[layer], w_up[layer], w_down[layer])
    return (hp, hs, jnp.stack(st_p), jnp.stack(st_s), k_p, v_p, k_s, v_s)
```

```python
import functools
import math

import jax
import jax.numpy as jnp
from jax import lax
from jax.experimental import pallas as pl
from jax.experimental.pallas import tpu as pltpu

F32 = jnp.float32
BF16 = jnp.bfloat16

D_MODEL = 1024
D_FF = 4 * D_MODEL
NORM_EPS = 1e-6
GLA_HEADS = 4
GLA_DK = 128
GLA_DV = 256
GLA_KEY_DIM = GLA_HEADS * GLA_DK
GLA_VALUE_DIM = GLA_HEADS * GLA_DV
GATE_RANK = 16
GATE_NORMALIZER = 16.0
HEAD_DIM = 128
Q_HEADS = 8
KV_HEADS = 4
GROUP = Q_HEADS // KV_HEADS
MOBA_BLOCK = 256
MOBA_TOPK = 3
PAGE_SIZE = 128
ROPE_THETA = 10000.0
NEG_INF = -1e30
LANES = 128

VMEM_LIMIT_BYTES = 56 * 1024 * 1024

PROMPT_ROW_TILE = 256
PROMPT_GLA_CHUNK = 64
MLP_ROW_TILE = 512
MLP_FF_TILE = 1024
SAMPLE_SEQ_TILE = 4
SAMPLE_PAGES_PER_STEP = 8


def _rms(x):
    return x * lax.rsqrt(jnp.mean(x * x, axis=-1, keepdims=True) + NORM_EPS)


def _log_sigmoid(x):
    return jnp.minimum(x, 0.0) - jnp.log1p(jnp.exp(-jnp.abs(x)))


def _dot(a, b):
    return jnp.dot(a, b, preferred_element_type=F32)


def _dot_nt(a, b, precision=None):
    return lax.dot_general(a, b, (((1,), (1,)), ((), ())), precision=precision,
                           preferred_element_type=F32)


def _dot_tn(a, b):
    return lax.dot_general(a, b, (((0,), (0,)), ((), ())), preferred_element_type=F32)


def _const_spec(shape):
    n = len(shape)
    return pl.BlockSpec(shape, lambda *_: (0,) * n)


def _gla_layer_kernel(x_ref, s0_ref, gmix_ref, wmain_ref, wglr_ref, wg2_ref, bg_ref, gout_ref, wo_ref,
                      h_ref, sout_ref, proj_ref, gk_ref, o_ref, s_ref, *, seqs, rows_per_seq, chunk):
    t = pl.program_id(1)
    rows = seqs * rows_per_seq
    n_chunks = rows_per_seq // chunk

    x = x_ref[...].reshape(rows, D_MODEL)
    xn = (_rms(x) * gmix_ref[...]).astype(BF16)
    proj_ref[...] = _dot(xn, wmain_ref[...])
    glr = _dot(xn, wglr_ref[...])
    gpre = _dot(glr.astype(BF16), wg2_ref[...]) + bg_ref[...]
    gk_ref[...] = _log_sigmoid(gpre) * (1.0 / GATE_NORMALIZER)

    @pl.when(t == 0)
    def _():
        s_ref[...] = s0_ref[...]

    r_i = lax.broadcasted_iota(jnp.int32, (chunk, chunk), 0)
    c_i = lax.broadcasted_iota(jnp.int32, (chunk, chunk), 1)
    causal = c_i <= r_i
    mm = BF16 if chunk % 16 == 0 else F32
    tri = jnp.where(causal, 1.0, 0.0).astype(mm)
    mid = max(chunk // 2, 1)
    gout = gout_ref[...]

    def chunk_body(idx, carry):
        seq = idx // n_chunks if (seqs > 1 and n_chunks > 1) else (idx if seqs > 1 else 0)
        rs = pl.ds(pl.multiple_of(idx * chunk, chunk), chunk)
        g = gk_ref[rs, :]
        g_hi = g.astype(BF16).astype(F32)
        bcum = _dot(tri, g_hi.astype(mm)) + _dot(tri, (g - g_hi).astype(mm))
        for h in range(GLA_HEADS):
            kd = slice(h * GLA_DK, (h + 1) * GLA_DK)
            vd = slice(h * GLA_DV, (h + 1) * GLA_DV)
            b = bcum[:, kd]
            b_last = b[chunk - 1:chunk, :]
            b_mid = b[mid - 1:mid, :]
            q = proj_ref[rs, kd] * (GLA_DK ** -0.5)
            k = proj_ref[rs, GLA_KEY_DIM + h * GLA_DK:GLA_KEY_DIM + (h + 1) * GLA_DK]
            v = proj_ref[rs, 2 * GLA_KEY_DIM + h * GLA_DV:2 * GLA_KEY_DIM + (h + 1) * GLA_DV].astype(mm)
            r = proj_ref[rs, 2 * GLA_KEY_DIM + GLA_VALUE_DIM + h * GLA_DV:
                         2 * GLA_KEY_DIM + GLA_VALUE_DIM + (h + 1) * GLA_DV]
            q_m = q * jnp.exp(b - b_mid)
            k_m = k * jnp.exp(b_mid - b)
            a = jnp.where(causal, _dot_nt(q_m.astype(mm), k_m.astype(mm)), 0.0)
            s_old = s_ref[seq, h]
            q_e = q_m * jnp.exp(b_mid)
            o = _dot(q_e.astype(BF16), s_old.astype(BF16)) + _dot(a.astype(mm), v)
            k_e = k_m * jnp.exp(b_last - b_mid)
            decay_col = jnp.transpose(jnp.broadcast_to(jnp.exp(b_last), (GLA_DK, GLA_DK)))
            s_ref[seq, h] = (s_old * jnp.concatenate([decay_col, decay_col], axis=1)
                             + _dot_tn(k_e.astype(mm), v))
            o_ref[rs, vd] = _rms(o) * gout * (r * jax.nn.sigmoid(r))
        return carry

    lax.fori_loop(0, seqs * n_chunks, chunk_body, 0)

    y = _dot(o_ref[...].astype(BF16), wo_ref[...])
    h_ref[...] = (x + y).reshape(seqs, rows_per_seq, D_MODEL)

    @pl.when(t == pl.num_programs(1) - 1)
    def _():
        sout_ref[...] = s_ref[...]


def _gla_layer(x, s0, gmix, wmain, wglr, wg2, bg, gout, wo, *, seqs, rows_per_seq, chunk):
    nseq, t_len, _ = x.shape
    grid = (nseq // seqs, t_len // rows_per_seq)
    rows = seqs * rows_per_seq
    state_block = (seqs, GLA_HEADS, GLA_DK, GLA_DV)
    kern = functools.partial(_gla_layer_kernel, seqs=seqs, rows_per_seq=rows_per_seq, chunk=chunk)
    return pl.pallas_call(
        kern,
        grid=grid,
        in_specs=[
            pl.BlockSpec((seqs, rows_per_seq, D_MODEL), lambda b, t: (b, t, 0)),
            pl.BlockSpec(state_block, lambda b, t: (b, 0, 0, 0)),
            _const_spec(gmix.shape), _const_spec(wmain.shape), _const_spec(wglr.shape),
            _const_spec(wg2.shape), _const_spec(bg.shape), _const_spec(gout.shape), _const_spec(wo.shape),
        ],
        out_specs=[
            pl.BlockSpec((seqs, rows_per_seq, D_MODEL), lambda b, t: (b, t, 0)),
            pl.BlockSpec(state_block, lambda b, t: (b, 0, 0, 0)),
        ],
        out_shape=[jax.ShapeDtypeStruct(x.shape, F32), jax.ShapeDtypeStruct(s0.shape, F32)],
        scratch_shapes=[
            pltpu.VMEM((rows, 2 * GLA_KEY_DIM + 2 * GLA_VALUE_DIM), F32),
            pltpu.VMEM((rows, GLA_KEY_DIM), F32),
            pltpu.VMEM((rows, GLA_VALUE_DIM), F32),
            pltpu.VMEM(state_block, F32),
        ],
        compiler_params=pltpu.CompilerParams(
            dimension_semantics=("arbitrary", "arbitrary"), vmem_limit_bytes=VMEM_LIMIT_BYTES),
        name="gla_layer",
    )(x, s0, gmix, wmain, wglr, wg2, bg, gout, wo)


def _mlp_kernel(*refs, has_attn):
    if has_attn:
        h_ref, a_ref, wo_ref, g_ref, wup_ref, wdn_ref, out_ref = refs
        h = h_ref[...] + _dot(a_ref[...].astype(BF16), wo_ref[...])
    else:
        h_ref, g_ref, wup_ref, wdn_ref, out_ref = refs
        h = h_ref[...]
    xn = (_rms(h) * g_ref[...]).astype(BF16)
    acc = h
    for j in range(D_FF // MLP_FF_TILE):
        ff = slice(j * MLP_FF_TILE, (j + 1) * MLP_FF_TILE)
        a = jnp.maximum(_dot(xn, wup_ref[:, ff]), 0.0)
        acc = acc + _dot((a * a).astype(BF16), wdn_ref[ff, :])
    out_ref[...] = acc


def _mlp(h, g, wup, wdn, attn=None, wo=None):
    n = h.shape[0]
    tm = min(MLP_ROW_TILE, n)
    row_spec = pl.BlockSpec((tm, D_MODEL), lambda i: (i, 0))
    single = pl.Buffered(1)

    def wspec(w):
        return pl.BlockSpec(w.shape, lambda i: (0, 0), pipeline_mode=single)

    if attn is None:
        args = (h, g, wup, wdn)
        in_specs = [row_spec, _const_spec(g.shape), wspec(wup), wspec(wdn)]
    else:
        args = (h, attn, wo, g, wup, wdn)
        in_specs = [row_spec, row_spec, wspec(wo), _const_spec(g.shape), wspec(wup), wspec(wdn)]
    return pl.pallas_call(
        functools.partial(_mlp_kernel, has_attn=attn is not None),
        grid=(n // tm,),
        in_specs=in_specs,
        out_specs=row_spec,
        out_shape=jax.ShapeDtypeStruct(h.shape, F32),
        compiler_params=pltpu.CompilerParams(
            dimension_semantics=("arbitrary",), vmem_limit_bytes=VMEM_LIMIT_BYTES),
        name="mlp",
    )(*args)


def _rope(x, cos, sin_signed):
    return x * cos + pltpu.roll(x, HEAD_DIM // 2, 1) * sin_signed


def _kvq_kernel(h_ref, nkv_ref, nq_ref, wkv_ref, wq_ref, gk_ref, gq_ref, cos_ref, sin_ref,
                k_ref, v_ref, q_ref, *maybe_vt_ref):
    xh = _rms(h_ref[...])
    kv = _dot((xh * nkv_ref[...]).astype(BF16), wkv_ref[...])
    qq = _dot((xh * nq_ref[...]).astype(BF16), wq_ref[...])
    cos = cos_ref[...]
    sin = sin_ref[...]
    kdim = KV_HEADS * HEAD_DIM
    for hh in range(KV_HEADS):
        hd = slice(hh * HEAD_DIM, (hh + 1) * HEAD_DIM)
        k_ref[:, hd] = _rope(_rms(kv[:, hd]) * gk_ref[...], cos, sin)
    v = kv[:, kdim:]
    v_ref[...] = v
    for hh in range(Q_HEADS):
        hd = slice(hh * HEAD_DIM, (hh + 1) * HEAD_DIM)
        q_ref[:, hd] = _rope(_rms(qq[:, hd]) * gq_ref[...], cos, sin)
    if maybe_vt_ref:
        (vt_ref,) = maybe_vt_ref
        for hh in range(KV_HEADS):
            vt_ref[0, hh, 0] = jnp.transpose(v[:, hh * HEAD_DIM:(hh + 1) * HEAD_DIM]).astype(BF16)


def _kvq(h, nkv, nq, wkv, wq, gk, gq, cos, sin, *, tm, seq_len=None):
    n = h.shape[0]
    pos_tiles = cos.shape[0] // tm
    kdim = KV_HEADS * HEAD_DIM
    row = lambda w: pl.BlockSpec((tm, w), lambda i: (i, 0))
    tab = pl.BlockSpec((tm, HEAD_DIM), lambda i: (i % pos_tiles, 0))
    out_specs = [row(kdim), row(kdim), row(Q_HEADS * HEAD_DIM)]
    out_shape = [jax.ShapeDtypeStruct((n, kdim), F32), jax.ShapeDtypeStruct((n, kdim), F32),
                 jax.ShapeDtypeStruct((n, Q_HEADS * HEAD_DIM), F32)]
    if seq_len is not None:
        assert tm == MOBA_BLOCK and seq_len % MOBA_BLOCK == 0
        nblk = seq_len // MOBA_BLOCK
        out_specs.append(pl.BlockSpec((1, KV_HEADS, 1, HEAD_DIM, MOBA_BLOCK),
                                      lambda i: (i // nblk, 0, i % nblk, 0, 0)))
        out_shape.append(jax.ShapeDtypeStruct((n // seq_len, KV_HEADS, nblk, HEAD_DIM, MOBA_BLOCK), BF16))
    return pl.pallas_call(
        _kvq_kernel,
        grid=(n // tm,),
        in_specs=[row(D_MODEL), _const_spec(nkv.shape), _const_spec(nq.shape), _const_spec(wkv.shape),
                  _const_spec(wq.shape), _const_spec(gk.shape), _const_spec(gq.shape), tab, tab],
        out_specs=out_specs,
        out_shape=out_shape,
        compiler_params=pltpu.CompilerParams(
            dimension_semantics=("arbitrary",), vmem_limit_bytes=VMEM_LIMIT_BYTES),
        name="kvq",
    )(h, nkv, nq, wkv, wq, gk, gq, cos, sin)


def _select_topk_rows(gate, cand, ksel):
    n = gate.shape[0]
    row = lax.broadcasted_iota(jnp.int32, gate.shape, 0).astype(F32)
    gg = jnp.where(cand, gate, -jnp.inf)
    sel = jnp.zeros(gate.shape, F32)
    for _ in range(ksel):
        mx = jnp.max(gg, axis=0, keepdims=True)
        first = jnp.min(jnp.where(gg == mx, row, float(n)), axis=0, keepdims=True)
        hit = row == first
        sel = jnp.where(hit, 1.0, sel)
        gg = jnp.where(hit, -jnp.inf, gg)
    return jnp.where(cand, sel, 0.0)


def _moba_prompt_kernel(q_ref, k_ref, vt_ref, o_ref, means_ref, sel_ref, m_ref, l_ref, acc_ref, *, nblk):
    qi = pl.program_id(2)
    nq = GROUP * MOBA_BLOCK

    @pl.when(qi == 0)
    def _():
        means_ref[...] = jnp.zeros(means_ref.shape, F32)
        for n in range(nblk):
            blk = k_ref[0, n * MOBA_BLOCK:(n + 1) * MOBA_BLOCK, :]
            means_ref[n:n + 1, :] = jnp.sum(blk, axis=0, keepdims=True) * (1.0 / MOBA_BLOCK)

    qb = q_ref[0]
    q2 = jnp.concatenate([qb[:, g * HEAD_DIM:(g + 1) * HEAD_DIM] for g in range(GROUP)], axis=0)
    gate = _dot_nt(means_ref[...], q2, precision=lax.Precision.HIGHEST)
    row = lax.broadcasted_iota(jnp.int32, gate.shape, 0)
    sel_ref[...] = _select_topk_rows(gate, row < qi, min(MOBA_TOPK, nblk))

    qs = (q2 * (HEAD_DIM ** -0.5)).astype(BF16)
    kpos = lax.broadcasted_iota(jnp.int32, (MOBA_BLOCK, nq), 0)
    qpos = lax.broadcasted_iota(jnp.int32, (MOBA_BLOCK, nq), 1) & (MOBA_BLOCK - 1)

    def scores(n):
        kb = k_ref[0, pl.ds(pl.multiple_of(n * MOBA_BLOCK, MOBA_BLOCK), MOBA_BLOCK), :].astype(BF16)
        return _dot_nt(kb, qs)

    s = jnp.where(kpos <= qpos, scores(qi), NEG_INF)
    m0 = jnp.max(s, axis=0, keepdims=True)
    p = jnp.exp(s - m0)
    m_ref[...] = m0
    l_ref[...] = jnp.sum(p, axis=0, keepdims=True)
    acc_ref[...] = _dot(vt_ref[0, 0, qi], p.astype(BF16))

    def past_block(n, carry):
        s = jnp.where(sel_ref[pl.ds(n, 1), :] > 0.0, scores(n), NEG_INF)
        m_old = m_ref[...]
        m_new = jnp.maximum(m_old, jnp.max(s, axis=0, keepdims=True))
        alpha = jnp.exp(m_old - m_new)
        p = jnp.exp(s - m_new)
        m_ref[...] = m_new
        l_ref[...] = alpha * l_ref[...] + jnp.sum(p, axis=0, keepdims=True)
        acc_ref[...] = alpha * acc_ref[...] + _dot(vt_ref[0, 0, n], p.astype(BF16))
        return carry

    lax.fori_loop(0, qi, past_block, 0)

    out = acc_ref[...] / l_ref[...]
    for g in range(GROUP):
        o_ref[0, :, g * HEAD_DIM:(g + 1) * HEAD_DIM] = jnp.transpose(
            out[:, g * MOBA_BLOCK:(g + 1) * MOBA_BLOCK])


def _moba_prompt(q, k, vt):
    bsz, t_len, _ = q.shape
    nblk = t_len // MOBA_BLOCK
    nq = GROUP * MOBA_BLOCK
    nblk_pad = -(-nblk // 8) * 8
    return pl.pallas_call(
        functools.partial(_moba_prompt_kernel, nblk=nblk),
        grid=(bsz, KV_HEADS, nblk),
        in_specs=[
            pl.BlockSpec((1, MOBA_BLOCK, GROUP * HEAD_DIM), lambda b, h, i: (b, i, h)),
            pl.BlockSpec((1, t_len, HEAD_DIM), lambda b, h, i: (b, 0, h)),
            pl.BlockSpec((1, 1, nblk, HEAD_DIM, MOBA_BLOCK), lambda b, h, i: (b, h, 0, 0, 0)),
        ],
        out_specs=pl.BlockSpec((1, MOBA_BLOCK, GROUP * HEAD_DIM), lambda b, h, i: (b, i, h)),
        out_shape=jax.ShapeDtypeStruct(q.shape, F32),
        scratch_shapes=[
            pltpu.VMEM((nblk_pad, HEAD_DIM), F32),
            pltpu.VMEM((nblk_pad, nq), F32),
            pltpu.VMEM((1, nq), F32),
            pltpu.VMEM((1, nq), F32),
            pltpu.VMEM((HEAD_DIM, nq), F32),
        ],
        compiler_params=pltpu.CompilerParams(
            dimension_semantics=("arbitrary", "arbitrary", "arbitrary"), vmem_limit_bytes=VMEM_LIMIT_BYTES),
        name="moba_prompt",
    )(q, k, vt)


def _group_rows(x_ref, h):
    return jnp.concatenate(
        [x_ref[0, :, (h * GROUP + g) * HEAD_DIM:(h * GROUP + g + 1) * HEAD_DIM] for g in range(GROUP)], axis=0)


def _moba_sample_kernel(pt_ref, q_ref, knew_ref, vnew_ref, *refs, pages_per_step, n_blocks, n_new):
    del pt_ref
    kp_refs = refs[:pages_per_step]
    vp_refs = refs[pages_per_step:2 * pages_per_step]
    o_ref, gate_ref, m_ref, l_ref, acc_ref = refs[2 * pages_per_step:]
    j = pl.program_id(1)
    pages_per_block = MOBA_BLOCK // PAGE_SIZE
    blocks_per_step = pages_per_step // pages_per_block
    nrow = GROUP * n_new
    lane = lax.broadcasted_iota(jnp.int32, (nrow, LANES), 1)
    scale = HEAD_DIM ** -0.5

    @pl.when(j == 0)
    def _():
        gate_ref[...] = jnp.zeros(gate_ref.shape, F32)
        m_ref[...] = jnp.zeros(m_ref.shape, F32)
        l_ref[...] = jnp.zeros(l_ref.shape, F32)

    for bi in range(blocks_per_step):
        n = j * blocks_per_step + bi
        hot = lane == n
        kpages = [kp_refs[bi * pages_per_block + pp][0] for pp in range(pages_per_block)]
        vpages = [vp_refs[bi * pages_per_block + pp][0] for pp in range(pages_per_block)]
        for h in range(KV_HEADS):
            hd = slice(h * HEAD_DIM, (h + 1) * HEAD_DIM)
            qf = _group_rows(q_ref, h)
            qs = (qf * scale).astype(BF16)
            ks = [kp[:, hd] for kp in kpages]
            s = jnp.concatenate([_dot_nt(qs, kk.astype(BF16)) for kk in ks], axis=1)
            mb = jnp.max(s, axis=-1, keepdims=True)
            p = jnp.exp(s - mb)
            lb = jnp.sum(p, axis=-1, keepdims=True)
            acc = _dot(p[:, :PAGE_SIZE].astype(BF16), vpages[0][:, hd].astype(BF16))
            for pp in range(1, pages_per_block):
                acc = acc + _dot(p[:, pp * PAGE_SIZE:(pp + 1) * PAGE_SIZE].astype(BF16),
                                 vpages[pp][:, hd].astype(BF16))
            acc_ref[n, h] = acc
            mean = sum(jnp.sum(kk, axis=0, keepdims=True) for kk in ks) * (1.0 / MOBA_BLOCK)
            gcol = jnp.sum(qf * mean, axis=-1, keepdims=True)
            gate_ref[h] = jnp.where(hot, gcol, gate_ref[h])
            m_ref[h] = jnp.where(hot, mb, m_ref[h])
            l_ref[h] = jnp.where(hot, lb, l_ref[h])

    @pl.when(j == pl.num_programs(1) - 1)
    def _():
        qrow_t = lax.rem(lax.broadcasted_iota(jnp.int32, (nrow, LANES), 0), n_new)
        lane_f = lane.astype(F32)
        for h in range(KV_HEADS):
            hd = slice(h * HEAD_DIM, (h + 1) * HEAD_DIM)
            qf = _group_rows(q_ref, h)
            qsc = qf * scale
            gg = jnp.where(lane < n_blocks, gate_ref[h], -jnp.inf)
            picked = jnp.zeros((nrow, LANES), F32)
            for _ in range(min(MOBA_TOPK, n_blocks)):
                mx = jnp.max(gg, axis=-1, keepdims=True)
                first = jnp.min(jnp.where(gg == mx, lane_f, float(LANES)), axis=-1, keepdims=True)
                hit = lane_f == first
                picked = jnp.where(hit, 1.0, picked)
                gg = jnp.where(hit, -jnp.inf, gg)
            sel = jnp.where(lane < n_blocks, picked, 0.0) > 0.0
            knew = knew_ref[0, :, hd]
            vnew = vnew_ref[0, :, hd]
            s_own = jnp.full((nrow, LANES), NEG_INF, F32)
            for jj in range(n_new):
                col = jnp.sum(qsc * knew[jj:jj + 1, :], axis=-1, keepdims=True)
                s_own = jnp.where(jnp.logical_and(lane == jj, qrow_t >= jj), col, s_own)
            mh = m_ref[h]
            m_tot = jnp.maximum(jnp.max(s_own, axis=-1, keepdims=True),
                                jnp.max(jnp.where(sel, mh, NEG_INF), axis=-1, keepdims=True))
            w = jnp.where(sel, jnp.exp(mh - m_tot), 0.0)
            p_own = jnp.exp(s_own - m_tot)
            den = jnp.sum(w * l_ref[h], axis=-1, keepdims=True) + jnp.sum(p_own, axis=-1, keepdims=True)
            num = jnp.zeros((nrow, HEAD_DIM), F32)
            for jj in range(n_new):
                num = num + p_own[:, jj:jj + 1] * vnew[jj:jj + 1, :]
            for n in range(n_blocks):
                num = num + w[:, n:n + 1] * acc_ref[n, h]
            out = num / den
            for g in range(GROUP):
                o_ref[0, :, (h * GROUP + g) * HEAD_DIM:(h * GROUP + g + 1) * HEAD_DIM] = (
                    out[g * n_new:(g + 1) * n_new, :])


def _moba_sample(q, knew, vnew, cache_k, cache_v, page_table):
    db, ds, _ = q.shape
    n_pages = page_table.shape[1]
    pages_per_block = MOBA_BLOCK // PAGE_SIZE
    assert n_pages % pages_per_block == 0, "past length must be a whole number of MoBA blocks"
    n_blocks = n_pages // pages_per_block
    assert 0 < n_blocks <= LANES
    pps = math.gcd(SAMPLE_PAGES_PER_STEP, n_pages)
    assert pps % pages_per_block == 0
    nrow = GROUP * ds
    kdim = KV_HEADS * HEAD_DIM

    def page_spec(i):
        return pl.BlockSpec((1, PAGE_SIZE, kdim),
                            lambda b, j, pt: (pt[b * n_pages + j * pps + i], 0, 0))

    per_seq = lambda w: pl.BlockSpec((1, ds, w), lambda b, j, pt: (b, 0, 0))
    grid_spec = pltpu.PrefetchScalarGridSpec(
        num_scalar_prefetch=1,
        grid=(db, n_pages // pps),
        in_specs=[per_seq(Q_HEADS * HEAD_DIM), per_seq(kdim), per_seq(kdim)]
                 + [page_spec(i) for i in range(pps)] + [page_spec(i) for i in range(pps)],
        out_specs=per_seq(Q_HEADS * HEAD_DIM),
        scratch_shapes=[
            pltpu.VMEM((KV_HEADS, nrow, LANES), F32),
            pltpu.VMEM((KV_HEADS, nrow, LANES), F32),
            pltpu.VMEM((KV_HEADS, nrow, LANES), F32),
            pltpu.VMEM((n_blocks, KV_HEADS, nrow, HEAD_DIM), F32),
        ],
    )
    return pl.pallas_call(
        functools.partial(_moba_sample_kernel, pages_per_step=pps, n_blocks=n_blocks, n_new=ds),
        grid_spec=grid_spec,
        out_shape=jax.ShapeDtypeStruct(q.shape, F32),
        compiler_params=pltpu.CompilerParams(
            dimension_semantics=("arbitrary", "arbitrary"), vmem_limit_bytes=VMEM_LIMIT_BYTES),
        name="moba_sample",
    )(page_table.reshape(-1), q, knew, vnew, *([cache_k] * pps), *([cache_v] * pps))


def _rope_tables(pos):
    inv = ROPE_THETA ** (-jnp.arange(0, HEAD_DIM, 2, dtype=F32) / HEAD_DIM)
    ang = pos.astype(F32)[:, None] * inv[None, :]
    cos, sin = jnp.cos(ang), jnp.sin(ang)
    return jnp.concatenate([cos, cos], axis=-1), jnp.concatenate([-sin, sin], axis=-1)


def kernel(x_prompt, x_sample, state_gla, cache_k, cache_v, page_table, w_in_a, w_g2, b_g, g_gla_out, w_o_a,
           norm_kv, w_kv, g_k, w_q_b, g_q, w_o_b, norm_mix, norm_mlp, w_up, w_down):
    bsz, t_len, _ = x_prompt.shape
    db, ds, _ = x_sample.shape
    n_a = state_gla.shape[0]
    depth = norm_mix.shape[0]
    assert n_a == 1 and depth == 2, "one GLA layer followed by one MoBA layer"
    assert t_len % PROMPT_ROW_TILE == 0 and PROMPT_ROW_TILE == MOBA_BLOCK
    past = page_table.shape[1] * PAGE_SIZE
    kdim = KV_HEADS * HEAD_DIM
    row2 = lambda a: a.reshape(1, -1)

    main = 2 * GLA_KEY_DIM + 2 * GLA_VALUE_DIM
    wmain = w_in_a[0, :, :main].astype(BF16)
    wglr = jnp.pad(w_in_a[0, :, main:], ((0, 0), (0, LANES - GATE_RANK))).astype(BF16)
    wg2 = jnp.pad(w_g2[0], ((0, LANES - GATE_RANK), (0, 0))).astype(BF16)
    gla_w = (row2(norm_mix[0]), wmain, wglr, wg2, row2(b_g[0]), row2(g_gla_out[0]), w_o_a[0].astype(BF16))
    s0_p = jnp.zeros((bsz,) + state_gla.shape[2:], F32)
    seq_tile = math.gcd(SAMPLE_SEQ_TILE, db)
    hp, st_p = _gla_layer(x_prompt, s0_p, *gla_w, seqs=1, rows_per_seq=PROMPT_ROW_TILE,
                          chunk=PROMPT_GLA_CHUNK)
    hs, st_s = _gla_layer(x_sample, state_gla[0], *gla_w, seqs=seq_tile, rows_per_seq=ds, chunk=ds)
    hp = hp.reshape(bsz * t_len, D_MODEL)
    hs = hs.reshape(db * ds, D_MODEL)
    wup = w_up.astype(BF16)
    wdn = w_down.astype(BF16)
    hp = _mlp(hp, row2(norm_mlp[0]), wup[0], wdn[0])
    hs = _mlp(hs, row2(norm_mlp[0]), wup[0], wdn[0])

    cos_p, sin_p = _rope_tables(jnp.arange(t_len, dtype=jnp.int32))
    cos_s, sin_s = _rope_tables(past + jnp.arange(ds, dtype=jnp.int32))
    cos_s, sin_s = jnp.tile(cos_s, (db, 1)), jnp.tile(sin_s, (db, 1))
    kvq_w = (row2(norm_kv), row2(norm_mix[1]), w_kv.astype(BF16), w_q_b[0].astype(BF16), row2(g_k), row2(g_q[0]))
    k_p, v_p, q_p, vt_p = _kvq(hp, *kvq_w, cos_p, sin_p, tm=PROMPT_ROW_TILE, seq_len=t_len)
    k_s, v_s, q_s = _kvq(hs, *kvq_w, cos_s, sin_s, tm=db * ds)

    a_p = _moba_prompt(q_p.reshape(bsz, t_len, -1), k_p.reshape(bsz, t_len, kdim), vt_p)
    a_s = _moba_sample(q_s.reshape(db, ds, -1), k_s.reshape(db, ds, kdim), v_s.reshape(db, ds, kdim),
                       cache_k.reshape(cache_k.shape[0], PAGE_SIZE, kdim),
                       cache_v.reshape(cache_v.shape[0], PAGE_SIZE, kdim), page_table)
    wob = w_o_b[0].astype(BF16)
    hp = _mlp(hp, row2(norm_mlp[1]), wup[1], wdn[1], attn=a_p.reshape(bsz * t_len, -1), wo=wob)
    hs = _mlp(hs, row2(norm_mlp[1]), wup[1], wdn[1], attn=a_s.reshape(db * ds, -1), wo=wob)

    kv4 = lambda a, n, t: a.reshape(n, t, KV_HEADS, HEAD_DIM)
    return (hp.reshape(bsz, t_len, D_MODEL), hs.reshape(db, ds, D_MODEL), st_p[None], st_s[None],
            kv4(k_p, bsz, t_len), kv4(v_p, bsz, t_len), kv4(k_s, db, ds), kv4(v_s, db, ds))
```

```python
import functools
import math

import jax
import jax.numpy as jnp
from jax import lax
from jax.experimental import pallas as pl
from jax.experimental.pallas import tpu as pltpu

F32 = jnp.float32
BF16 = jnp.bfloat16

D_MODEL = 1024
D_FF = 4 * D_MODEL
NORM_EPS = 1e-6
GLA_HEADS = 4
GLA_DK = 128
GLA_DV = 256
GLA_KEY_DIM = GLA_HEADS * GLA_DK
GLA_VALUE_DIM = GLA_HEADS * GLA_DV
GATE_RANK = 16
GATE_NORMALIZER = 16.0
HEAD_DIM = 128
Q_HEADS = 8
KV_HEADS = 4
GROUP = Q_HEADS // KV_HEADS
MOBA_BLOCK = 256
MOBA_TOPK = 3
PAGE_SIZE = 128
ROPE_THETA = 10000.0
NEG_INF = -1e30
LANES = 128

VMEM_LIMIT_BYTES = 56 * 1024 * 1024

PROMPT_ROW_TILE = 256
PROMPT_GLA_CHUNK = 256
MLP_ROW_TILE = 512
MLP_FF_TILE = 1024
SAMPLE_SEQ_TILE = 4
SAMPLE_PAGES_PER_STEP = 8


def _rms(x):
    return x * lax.rsqrt(jnp.mean(x * x, axis=-1, keepdims=True) + NORM_EPS)


def _log_sigmoid(x):
    return jnp.minimum(x, 0.0) - jnp.log1p(jnp.exp(-jnp.abs(x)))


def _dot(a, b):
    return jnp.dot(a, b, preferred_element_type=F32)


def _dot_nt(a, b, precision=None):
    return lax.dot_general(a, b, (((1,), (1,)), ((), ())), precision=precision,
                           preferred_element_type=F32)


def _dot_tn(a, b):
    return lax.dot_general(a, b, (((0,), (0,)), ((), ())), preferred_element_type=F32)


def _const_spec(shape):
    n = len(shape)
    return pl.BlockSpec(shape, lambda *_: (0,) * n)


def _gla_layer_kernel(x_ref, s0_ref, gmix_ref, wmain_ref, wglr_ref, wg2_ref, bg_ref, gout_ref, wo_ref,
                      h_ref, sout_ref, proj_ref, gk_ref, o_ref, s_ref, *, seqs, rows_per_seq, chunk):
    t = pl.program_id(1)
    rows = seqs * rows_per_seq
    n_chunks = rows_per_seq // chunk

    x = x_ref[...].reshape(rows, D_MODEL)
    xn = (_rms(x) * gmix_ref[...]).astype(BF16)
    proj_ref[...] = _dot(xn, wmain_ref[...])
    glr = _dot(xn, wglr_ref[...])
    gpre = _dot(glr.astype(BF16), wg2_ref[...]) + bg_ref[...]
    gk_ref[...] = _log_sigmoid(gpre) * (1.0 / GATE_NORMALIZER)

    @pl.when(t == 0)
    def _():
        s_ref[...] = s0_ref[...]

    r_i = lax.broadcasted_iota(jnp.int32, (chunk, chunk), 0)
    c_i = lax.broadcasted_iota(jnp.int32, (chunk, chunk), 1)
    causal = c_i <= r_i
    mm = BF16 if chunk % 16 == 0 else F32
    tri = jnp.where(causal, 1.0, 0.0).astype(mm)
    mid = max(chunk // 2, 1)
    gout = gout_ref[...]

    def chunk_body(idx, carry):
        seq = idx // n_chunks if (seqs > 1 and n_chunks > 1) else (idx if seqs > 1 else 0)
        rs = pl.ds(pl.multiple_of(idx * chunk, chunk), chunk)
        g = gk_ref[rs, :]
        g_hi = g.astype(BF16).astype(F32)
        bcum = _dot(tri, g_hi.astype(mm)) + _dot(tri, (g - g_hi).astype(mm))
        for h in range(GLA_HEADS):
            kd = slice(h * GLA_DK, (h + 1) * GLA_DK)
            vd = slice(h * GLA_DV, (h + 1) * GLA_DV)
            b = bcum[:, kd]
            b_last = b[chunk - 1:chunk, :]
            b_mid = b[mid - 1:mid, :]
            q = proj_ref[rs, kd] * (GLA_DK ** -0.5)
            k = proj_ref[rs, GLA_KEY_DIM + h * GLA_DK:GLA_KEY_DIM + (h + 1) * GLA_DK]
            v = proj_ref[rs, 2 * GLA_KEY_DIM + h * GLA_DV:2 * GLA_KEY_DIM + (h + 1) * GLA_DV].astype(mm)
            r = proj_ref[rs, 2 * GLA_KEY_DIM + GLA_VALUE_DIM + h * GLA_DV:
                         2 * GLA_KEY_DIM + GLA_VALUE_DIM + (h + 1) * GLA_DV]
            q_m = q * jnp.exp(b - b_mid)
            k_m = k * jnp.exp(b_mid - b)
            a = jnp.where(causal, _dot_nt(q_m.astype(mm), k_m.astype(mm)), 0.0)
            s_old = s_ref[seq, h]
            q_e = q_m * jnp.exp(b_mid)
            o = _dot(q_e.astype(BF16), s_old.astype(BF16)) + _dot(a.astype(mm), v)
            k_e = k_m * jnp.exp(b_last - b_mid)
            decay_col = jnp.transpose(jnp.broadcast_to(jnp.exp(b_last), (GLA_DK, GLA_DK)))
            s_ref[seq, h] = (s_old * jnp.concatenate([decay_col, decay_col], axis=1)
                             + _dot_tn(k_e.astype(mm), v))
            o_ref[rs, vd] = _rms(o) * gout * (r * jax.nn.sigmoid(r))
        return carry

    lax.fori_loop(0, seqs * n_chunks, chunk_body, 0)

    y = _dot(o_ref[...].astype(BF16), wo_ref[...])
    h_ref[...] = (x + y).reshape(seqs, rows_per_seq, D_MODEL)

    @pl.when(t == pl.num_programs(1) - 1)
    def _():
        sout_ref[...] = s_ref[...]


def _gla_layer(x, s0, gmix, wmain, wglr, wg2, bg, gout, wo, *, seqs, rows_per_seq, chunk):
    nseq, t_len, _ = x.shape
    grid = (nseq // seqs, t_len // rows_per_seq)
    rows = seqs * rows_per_seq
    state_block = (seqs, GLA_HEADS, GLA_DK, GLA_DV)
    kern = functools.partial(_gla_layer_kernel, seqs=seqs, rows_per_seq=rows_per_seq, chunk=chunk)
    return pl.pallas_call(
        kern,
        grid=grid,
        in_specs=[
            pl.BlockSpec((seqs, rows_per_seq, D_MODEL), lambda b, t: (b, t, 0)),
            pl.BlockSpec(state_block, lambda b, t: (b, 0, 0, 0)),
            _const_spec(gmix.shape), _const_spec(wmain.shape), _const_spec(wglr.shape),
            _const_spec(wg2.shape), _const_spec(bg.shape), _const_spec(gout.shape), _const_spec(wo.shape),
        ],
        out_specs=[
            pl.BlockSpec((seqs, rows_per_seq, D_MODEL), lambda b, t: (b, t, 0)),
            pl.BlockSpec(state_block, lambda b, t: (b, 0, 0, 0)),
        ],
        out_shape=[jax.ShapeDtypeStruct(x.shape, F32), jax.ShapeDtypeStruct(s0.shape, F32)],
        scratch_shapes=[
            pltpu.VMEM((rows, 2 * GLA_KEY_DIM + 2 * GLA_VALUE_DIM), F32),
            pltpu.VMEM((rows, GLA_KEY_DIM), F32),
            pltpu.VMEM((rows, GLA_VALUE_DIM), F32),
            pltpu.VMEM(state_block, F32),
        ],
        compiler_params=pltpu.CompilerParams(
            dimension_semantics=("arbitrary", "arbitrary"), vmem_limit_bytes=VMEM_LIMIT_BYTES),
        name="gla_layer",
    )(x, s0, gmix, wmain, wglr, wg2, bg, gout, wo)


def _mlp_kernel(*refs, has_attn):
    if has_attn:
        h_ref, a_ref, wo_ref, g_ref, wup_ref, wdn_ref, out_ref = refs
        h = h_ref[...] + _dot(a_ref[...].astype(BF16), wo_ref[...])
    else:
        h_ref, g_ref, wup_ref, wdn_ref, out_ref = refs
        h = h_ref[...]
    xn = (_rms(h) * g_ref[...]).astype(BF16)
    acc = h
    for j in range(D_FF // MLP_FF_TILE):
        ff = slice(j * MLP_FF_TILE, (j + 1) * MLP_FF_TILE)
        a = jnp.maximum(_dot(xn, wup_ref[:, ff]), 0.0)
        acc = acc + _dot((a * a).astype(BF16), wdn_ref[ff, :])
    out_ref[...] = acc


def _mlp(h, g, wup, wdn, attn=None, wo=None):
    n = h.shape[0]
    tm = min(MLP_ROW_TILE, n)
    row_spec = pl.BlockSpec((tm, D_MODEL), lambda i: (i, 0))
    single = pl.Buffered(1)

    def wspec(w):
        return pl.BlockSpec(w.shape, lambda i: (0, 0), pipeline_mode=single)

    if attn is None:
        args = (h, g, wup, wdn)
        in_specs = [row_spec, _const_spec(g.shape), wspec(wup), wspec(wdn)]
    else:
        args = (h, attn, wo, g, wup, wdn)
        in_specs = [row_spec, row_spec, wspec(wo), _const_spec(g.shape), wspec(wup), wspec(wdn)]
    return pl.pallas_call(
        functools.partial(_mlp_kernel, has_attn=attn is not None),
        grid=(n // tm,),
        in_specs=in_specs,
        out_specs=row_spec,
        out_shape=jax.ShapeDtypeStruct(h.shape, F32),
        compiler_params=pltpu.CompilerParams(
            dimension_semantics=("arbitrary",), vmem_limit_bytes=VMEM_LIMIT_BYTES),
        name="mlp",
    )(*args)


def _rope(x, cos, sin_signed):
    return x * cos + pltpu.roll(x, HEAD_DIM // 2, 1) * sin_signed


def _kvq_kernel(h_ref, nkv_ref, nq_ref, wkv_ref, wq_ref, gk_ref, gq_ref, cos_ref, sin_ref,
                k_ref, q_ref, k4_ref, v4_ref, v_ref, *, transposed_v):
    tm = h_ref.shape[0]
    xh = _rms(h_ref[...])
    kv = _dot((xh * nkv_ref[...]).astype(BF16), wkv_ref[...])
    qq = _dot((xh * nq_ref[...]).astype(BF16), wq_ref[...])
    cos = cos_ref[...]
    sin = sin_ref[...]
    kdim = KV_HEADS * HEAD_DIM
    for hh in range(KV_HEADS):
        hd = slice(hh * HEAD_DIM, (hh + 1) * HEAD_DIM)
        kh = _rope(_rms(kv[:, hd]) * gk_ref[...], cos, sin)
        vh = kv[:, kdim + hh * HEAD_DIM:kdim + (hh + 1) * HEAD_DIM]
        k_ref[:, hd] = kh
        k4_ref[pl.ds(hh, tm, stride=KV_HEADS), :] = kh
        v4_ref[pl.ds(hh, tm, stride=KV_HEADS), :] = vh
        if transposed_v:
            v_ref[0, hh] = jnp.transpose(vh).astype(BF16)
    if not transposed_v:
        v_ref[...] = kv[:, kdim:]
    for hh in range(Q_HEADS):
        hd = slice(hh * HEAD_DIM, (hh + 1) * HEAD_DIM)
        q_ref[:, hd] = _rope(_rms(qq[:, hd]) * gq_ref[...], cos, sin)


def _kvq(h, nkv, nq, wkv, wq, gk, gq, cos, sin, *, tm, seq_len=None):
    n = h.shape[0]
    pos_tiles = cos.shape[0] // tm
    kdim = KV_HEADS * HEAD_DIM
    row = lambda w: pl.BlockSpec((tm, w), lambda i: (i, 0))
    row4 = pl.BlockSpec((tm * KV_HEADS, HEAD_DIM), lambda i: (i, 0))
    tab = pl.BlockSpec((tm, HEAD_DIM), lambda i: (i % pos_tiles, 0))
    out_specs = [row(kdim), row(Q_HEADS * HEAD_DIM), row4, row4]
    out_shape = [jax.ShapeDtypeStruct((n, kdim), F32), jax.ShapeDtypeStruct((n, Q_HEADS * HEAD_DIM), F32),
                 jax.ShapeDtypeStruct((n * KV_HEADS, HEAD_DIM), F32),
                 jax.ShapeDtypeStruct((n * KV_HEADS, HEAD_DIM), F32)]
    if seq_len is not None:
        assert seq_len % tm == 0 and tm % LANES == 0
        tiles = seq_len // tm
        out_specs.append(pl.BlockSpec((1, KV_HEADS, HEAD_DIM, tm), lambda i: (i // tiles, 0, 0, i % tiles)))
        out_shape.append(jax.ShapeDtypeStruct((n // seq_len, KV_HEADS, HEAD_DIM, seq_len), BF16))
    else:
        out_specs.append(row(kdim))
        out_shape.append(jax.ShapeDtypeStruct((n, kdim), F32))
    return pl.pallas_call(
        functools.partial(_kvq_kernel, transposed_v=seq_len is not None),
        grid=(n // tm,),
        in_specs=[row(D_MODEL), _const_spec(nkv.shape), _const_spec(nq.shape), _const_spec(wkv.shape),
                  _const_spec(wq.shape), _const_spec(gk.shape), _const_spec(gq.shape), tab, tab],
        out_specs=out_specs,
        out_shape=out_shape,
        compiler_params=pltpu.CompilerParams(
            dimension_semantics=("arbitrary",), vmem_limit_bytes=VMEM_LIMIT_BYTES),
        name="kvq",
    )(h, nkv, nq, wkv, wq, gk, gq, cos, sin)


def _select_topk_rows(gate, cand, ksel):
    n = gate.shape[0]
    row = lax.broadcasted_iota(jnp.int32, gate.shape, 0).astype(F32)
    gg = jnp.where(cand, gate, -jnp.inf)
    sel = jnp.zeros(gate.shape, F32)
    for _ in range(ksel):
        mx = jnp.max(gg, axis=0, keepdims=True)
        first = jnp.min(jnp.where(gg == mx, row, float(n)), axis=0, keepdims=True)
        hit = row == first
        sel = jnp.where(hit, 1.0, sel)
        gg = jnp.where(hit, -jnp.inf, gg)
    return jnp.where(cand, sel, 0.0)


def _split_bf16(x):
    hi = x.astype(BF16)
    return hi, (x - hi.astype(F32)).astype(BF16)


def _moba_prompt_kernel(q_ref, k_ref, vt_ref, o_ref, means_ref, sel_ref, s_ref, p_ref, *, nblk):
    qi = pl.program_id(2)
    nq = GROUP * MOBA_BLOCK
    sub = 8

    @pl.when(qi == 0)
    def _():
        means_ref[...] = jnp.zeros(means_ref.shape, F32)
        for n in range(nblk):
            blk = k_ref[0, n * MOBA_BLOCK:(n + 1) * MOBA_BLOCK, :]
            means_ref[n:n + 1, :] = jnp.sum(blk, axis=0, keepdims=True) * (1.0 / MOBA_BLOCK)

    qb = q_ref[0]
    q2 = jnp.concatenate([qb[:, g * HEAD_DIM:(g + 1) * HEAD_DIM] for g in range(GROUP)], axis=0)
    m_hi, m_lo = _split_bf16(means_ref[...])
    q_hi, q_lo = _split_bf16(q2)
    gate = _dot_nt(m_hi, q_hi) + _dot_nt(m_hi, q_lo) + _dot_nt(m_lo, q_hi)
    row = lax.broadcasted_iota(jnp.int32, gate.shape, 0)
    sel_ref[...] = _select_topk_rows(gate, row < qi, min(MOBA_TOPK, nblk))
    qs = (q2 * (HEAD_DIM ** -0.5 * math.log2(math.e))).astype(BF16)

    def attend(nb):
        kpos = lax.broadcasted_iota(jnp.int32, (MOBA_BLOCK, nq), 0)
        qpos = lax.broadcasted_iota(jnp.int32, (MOBA_BLOCK, nq), 1) & (MOBA_BLOCK - 1)
        m_part = None
        for n in range(nb + 1):
            kb = k_ref[0, n * MOBA_BLOCK:(n + 1) * MOBA_BLOCK, :].astype(BF16)
            mask = (kpos <= qpos) if n == nb else (sel_ref[n:n + 1, :] > 0.0)
            s = jnp.where(mask, _dot_nt(kb, qs), NEG_INF)
            s_ref[n] = s
            part = jnp.max(s.reshape(MOBA_BLOCK // sub, sub, nq), axis=0)
            m_part = part if m_part is None else jnp.maximum(m_part, part)
        m = jnp.max(m_part, axis=0, keepdims=True)
        l_part = jnp.zeros((sub, nq), F32)
        for n in range(nb + 1):
            p = jnp.exp2(s_ref[n] - m)
            l_part = l_part + jnp.sum(p.reshape(MOBA_BLOCK // sub, sub, nq), axis=0)
            p_ref[n * MOBA_BLOCK:(n + 1) * MOBA_BLOCK, :] = p.astype(BF16)
        keys = (nb + 1) * MOBA_BLOCK
        acc = _dot(vt_ref[0, 0, :, :keys], p_ref[:keys, :])
        out = acc * (1.0 / jnp.sum(l_part, axis=0, keepdims=True))
        for g in range(GROUP):
            o_ref[0, :, g * HEAD_DIM:(g + 1) * HEAD_DIM] = jnp.transpose(
                out[:, g * MOBA_BLOCK:(g + 1) * MOBA_BLOCK])

    for nb in range(nblk):
        pl.when(qi == nb)(functools.partial(attend, nb))


def _moba_prompt(q, k, vt):
    bsz, t_len, _ = q.shape
    nblk = t_len // MOBA_BLOCK
    nblk_pad = -(-nblk // 8) * 8
    nq = GROUP * MOBA_BLOCK
    return pl.pallas_call(
        functools.partial(_moba_prompt_kernel, nblk=nblk),
        grid=(bsz, KV_HEADS, nblk),
        in_specs=[
            pl.BlockSpec((1, MOBA_BLOCK, GROUP * HEAD_DIM), lambda b, h, i: (b, i, h)),
            pl.BlockSpec((1, t_len, HEAD_DIM), lambda b, h, i: (b, 0, h)),
            pl.BlockSpec((1, 1, HEAD_DIM, t_len), lambda b, h, i: (b, h, 0, 0)),
        ],
        out_specs=pl.BlockSpec((1, MOBA_BLOCK, GROUP * HEAD_DIM), lambda b, h, i: (b, i, h)),
        out_shape=jax.ShapeDtypeStruct(q.shape, F32),
        scratch_shapes=[
            pltpu.VMEM((nblk_pad, HEAD_DIM), F32),
            pltpu.VMEM((nblk_pad, nq), F32),
            pltpu.VMEM((nblk, MOBA_BLOCK, nq), F32),
            pltpu.VMEM((t_len, nq), BF16),
        ],
        compiler_params=pltpu.CompilerParams(
            dimension_semantics=("arbitrary", "arbitrary", "arbitrary"), vmem_limit_bytes=VMEM_LIMIT_BYTES),
        name="moba_prompt",
    )(q, k, vt)


def _moba_sample_kernel(pt_ref, q_ref, knew_ref, vnew_ref, *refs, pages_per_step, n_blocks, n_new):
    del pt_ref
    kp_refs = refs[:pages_per_step]
    vp_refs = refs[pages_per_step:2 * pages_per_step]
    o_ref, gate_ref, m_ref, l_ref, acc_ref = refs[2 * pages_per_step:]
    j = pl.program_id(1)
    pages_per_block = MOBA_BLOCK // PAGE_SIZE
    blocks_per_step = pages_per_step // pages_per_block
    page_rows = PAGE_SIZE * KV_HEADS
    nkv = GROUP * n_new
    nrow = KV_HEADS * nkv
    lane = lax.broadcasted_iota(jnp.int32, (nrow, LANES), 1)
    scale = HEAD_DIM ** -0.5

    @pl.when(j == 0)
    def _():
        gate_ref[...] = jnp.zeros(gate_ref.shape, F32)
        m_ref[...] = jnp.zeros(m_ref.shape, F32)
        l_ref[...] = jnp.zeros(l_ref.shape, F32)

    qf = jnp.concatenate([q_ref[0, :, hq * HEAD_DIM:(hq + 1) * HEAD_DIM] for hq in range(Q_HEADS)], axis=0)
    qs = (qf * scale).astype(BF16)
    blk_cols = pages_per_block * page_rows
    col_head = lax.broadcasted_iota(jnp.int32, (nrow, blk_cols), 1) & (KV_HEADS - 1)
    row_head = lax.div(lax.broadcasted_iota(jnp.int32, (nrow, blk_cols), 0), nkv)
    same_head = col_head == row_head
    sub = 8

    for bi in range(blocks_per_step):
        n = j * blocks_per_step + bi
        hot = lane == n
        kpages = [kp_refs[bi * pages_per_block + pp][0] for pp in range(pages_per_block)]
        vpages = [vp_refs[bi * pages_per_block + pp][0] for pp in range(pages_per_block)]
        s = jnp.concatenate([_dot_nt(qs, kp.astype(BF16)) for kp in kpages], axis=1)
        s = jnp.where(same_head, s, NEG_INF)
        mb = jnp.max(s, axis=-1, keepdims=True)
        p = jnp.exp(s - mb)
        lb = jnp.sum(p, axis=-1, keepdims=True)
        acc = _dot(p[:, :page_rows].astype(BF16), vpages[0].astype(BF16))
        for pp in range(1, pages_per_block):
            acc = acc + _dot(p[:, pp * page_rows:(pp + 1) * page_rows].astype(BF16), vpages[pp].astype(BF16))
        acc_ref[n] = acc
        ksum = sum(jnp.sum(kp.reshape(page_rows // sub, sub, HEAD_DIM), axis=0) for kp in kpages)
        mean = sum(ksum[i * KV_HEADS:(i + 1) * KV_HEADS] for i in range(sub // KV_HEADS)) * (1.0 / MOBA_BLOCK)
        mean_rows = jnp.concatenate(
            [jnp.broadcast_to(mean[h:h + 1], (nkv, HEAD_DIM)) for h in range(KV_HEADS)], axis=0)
        gcol = jnp.sum(qf * mean_rows, axis=-1, keepdims=True)
        gate_ref[...] = jnp.where(hot, gcol, gate_ref[...])
        m_ref[...] = jnp.where(hot, mb, m_ref[...])
        l_ref[...] = jnp.where(hot, lb, l_ref[...])

    @pl.when(j == pl.num_programs(1) - 1)
    def _():
        lane = lax.broadcasted_iota(jnp.int32, (nkv, LANES), 1)
        qrow_t = lax.rem(lax.broadcasted_iota(jnp.int32, (nkv, LANES), 0), n_new)
        lane_f = lane.astype(F32)
        for h in range(KV_HEADS):
            hd = slice(h * HEAD_DIM, (h + 1) * HEAD_DIM)
            rs = slice(h * nkv, (h + 1) * nkv)
            qsc = qf[rs] * scale
            gg = jnp.where(lane < n_blocks, gate_ref[rs, :], -jnp.inf)
            picked = jnp.zeros((nkv, LANES), F32)
            for _ in range(min(MOBA_TOPK, n_blocks)):
                mx = jnp.max(gg, axis=-1, keepdims=True)
                first = jnp.min(jnp.where(gg == mx, lane_f, float(LANES)), axis=-1, keepdims=True)
                hit = lane_f == first
                picked = jnp.where(hit, 1.0, picked)
                gg = jnp.where(hit, -jnp.inf, gg)
            sel = jnp.where(lane < n_blocks, picked, 0.0) > 0.0
            knew = knew_ref[0, :, hd]
            vnew = vnew_ref[0, :, hd]
            s_own = jnp.full((nkv, LANES), NEG_INF, F32)
            for jj in range(n_new):
                col = jnp.sum(qsc * knew[jj:jj + 1, :], axis=-1, keepdims=True)
                s_own = jnp.where(jnp.logical_and(lane == jj, qrow_t >= jj), col, s_own)
            mh = m_ref[rs, :]
            m_tot = jnp.maximum(jnp.max(s_own, axis=-1, keepdims=True),
                                jnp.max(jnp.where(sel, mh, NEG_INF), axis=-1, keepdims=True))
            w = jnp.where(sel, jnp.exp(mh - m_tot), 0.0)
            p_own = jnp.exp(s_own - m_tot)
            den = jnp.sum(w * l_ref[rs, :], axis=-1, keepdims=True) + jnp.sum(p_own, axis=-1, keepdims=True)
            num = jnp.zeros((nkv, HEAD_DIM), F32)
            for jj in range(n_new):
                num = num + p_own[:, jj:jj + 1] * vnew[jj:jj + 1, :]
            for n in range(n_blocks):
                num = num + w[:, n:n + 1] * acc_ref[n, rs, :]
            out = num / den
            for g in range(GROUP):
                o_ref[0, :, (h * GROUP + g) * HEAD_DIM:(h * GROUP + g + 1) * HEAD_DIM] = (
                    out[g * n_new:(g + 1) * n_new, :])


def _moba_sample(q, knew, vnew, cache_k, cache_v, page_table):
    db, ds, _ = q.shape
    n_pages = page_table.shape[1]
    pages_per_block = MOBA_BLOCK // PAGE_SIZE
    assert n_pages % pages_per_block == 0, "past length must be a whole number of MoBA blocks"
    n_blocks = n_pages // pages_per_block
    assert 0 < n_blocks <= LANES
    assert KV_HEADS & (KV_HEADS - 1) == 0 and 8 % KV_HEADS == 0
    pps = math.gcd(SAMPLE_PAGES_PER_STEP, n_pages)
    assert pps % pages_per_block == 0
    nrow = Q_HEADS * ds
    kdim = KV_HEADS * HEAD_DIM

    def page_spec(i):
        return pl.BlockSpec((1, PAGE_SIZE * KV_HEADS, HEAD_DIM),
                            lambda b, j, pt: (pt[b * n_pages + j * pps + i], 0, 0))

    per_seq = lambda w: pl.BlockSpec((1, ds, w), lambda b, j, pt: (b, 0, 0))
    grid_spec = pltpu.PrefetchScalarGridSpec(
        num_scalar_prefetch=1,
        grid=(db, n_pages // pps),
        in_specs=[per_seq(Q_HEADS * HEAD_DIM), per_seq(kdim), per_seq(kdim)]
                 + [page_spec(i) for i in range(pps)] + [page_spec(i) for i in range(pps)],
        out_specs=per_seq(Q_HEADS * HEAD_DIM),
        scratch_shapes=[
            pltpu.VMEM((nrow, LANES), F32),
            pltpu.VMEM((nrow, LANES), F32),
            pltpu.VMEM((nrow, LANES), F32),
            pltpu.VMEM((n_blocks, nrow, HEAD_DIM), F32),
        ],
    )
    return pl.pallas_call(
        functools.partial(_moba_sample_kernel, pages_per_step=pps, n_blocks=n_blocks, n_new=ds),
        grid_spec=grid_spec,
        out_shape=jax.ShapeDtypeStruct(q.shape, F32),
        compiler_params=pltpu.CompilerParams(
            dimension_semantics=("arbitrary", "arbitrary"), vmem_limit_bytes=VMEM_LIMIT_BYTES),
        name="moba_sample",
    )(page_table.reshape(-1), q, knew, vnew, *([cache_k] * pps), *([cache_v] * pps))


def _rope_tables(pos):
    inv = ROPE_THETA ** (-jnp.arange(0, HEAD_DIM, 2, dtype=F32) / HEAD_DIM)
    ang = pos.astype(F32)[:, None] * inv[None, :]
    cos, sin = jnp.cos(ang), jnp.sin(ang)
    return jnp.concatenate([cos, cos], axis=-1), jnp.concatenate([-sin, sin], axis=-1)


def kernel(x_prompt, x_sample, state_gla, cache_k, cache_v, page_table, w_in_a, w_g2, b_g, g_gla_out, w_o_a,
           norm_kv, w_kv, g_k, w_q_b, g_q, w_o_b, norm_mix, norm_mlp, w_up, w_down):
    bsz, t_len, _ = x_prompt.shape
    db, ds, _ = x_sample.shape
    n_a = state_gla.shape[0]
    depth = norm_mix.shape[0]
    assert n_a == 1 and depth == 2, "one GLA layer followed by one MoBA layer"
    assert t_len % PROMPT_ROW_TILE == 0 and PROMPT_ROW_TILE == MOBA_BLOCK
    past = page_table.shape[1] * PAGE_SIZE
    kdim = KV_HEADS * HEAD_DIM
    row2 = lambda a: a.reshape(1, -1)

    main = 2 * GLA_KEY_DIM + 2 * GLA_VALUE_DIM
    wmain = w_in_a[0, :, :main].astype(BF16)
    wglr = jnp.pad(w_in_a[0, :, main:], ((0, 0), (0, LANES - GATE_RANK))).astype(BF16)
    wg2 = jnp.pad(w_g2[0], ((0, LANES - GATE_RANK), (0, 0))).astype(BF16)
    gla_w = (row2(norm_mix[0]), wmain, wglr, wg2, row2(b_g[0]), row2(g_gla_out[0]), w_o_a[0].astype(BF16))
    s0_p = jnp.zeros((bsz,) + state_gla.shape[2:], F32)
    seq_tile = math.gcd(SAMPLE_SEQ_TILE, db)
    hp, st_p = _gla_layer(x_prompt, s0_p, *gla_w, seqs=1, rows_per_seq=PROMPT_ROW_TILE,
                          chunk=PROMPT_GLA_CHUNK)
    hs, st_s = _gla_layer(x_sample, state_gla[0], *gla_w, seqs=seq_tile, rows_per_seq=ds, chunk=ds)
    hp = hp.reshape(bsz * t_len, D_MODEL)
    hs = hs.reshape(db * ds, D_MODEL)
    wup = w_up.astype(BF16)
    wdn = w_down.astype(BF16)
    hp = _mlp(hp, row2(norm_mlp[0]), wup[0], wdn[0])
    hs = _mlp(hs, row2(norm_mlp[0]), wup[0], wdn[0])

    cos_p, sin_p = _rope_tables(jnp.arange(t_len, dtype=jnp.int32))
    cos_s, sin_s = _rope_tables(past + jnp.arange(ds, dtype=jnp.int32))
    cos_s, sin_s = jnp.tile(cos_s, (db, 1)), jnp.tile(sin_s, (db, 1))
    kvq_w = (row2(norm_kv), row2(norm_mix[1]), w_kv.astype(BF16), w_q_b[0].astype(BF16), row2(g_k), row2(g_q[0]))
    k_p, q_p, k4_p, v4_p, vt_p = _kvq(hp, *kvq_w, cos_p, sin_p, tm=PROMPT_ROW_TILE, seq_len=t_len)
    k_s, q_s, k4_s, v4_s, v_s = _kvq(hs, *kvq_w, cos_s, sin_s, tm=db * ds)

    a_p = _moba_prompt(q_p.reshape(bsz, t_len, -1), k_p.reshape(bsz, t_len, kdim), vt_p)
    pages = lambda c: c.reshape(c.shape[0], PAGE_SIZE * KV_HEADS, HEAD_DIM)
    a_s = _moba_sample(q_s.reshape(db, ds, -1), k_s.reshape(db, ds, kdim), v_s.reshape(db, ds, kdim),
                       pages(cache_k), pages(cache_v), page_table)
    wob = w_o_b[0].astype(BF16)
    hp = _mlp(hp, row2(norm_mlp[1]), wup[1], wdn[1], attn=a_p.reshape(bsz * t_len, -1), wo=wob)
    hs = _mlp(hs, row2(norm_mlp[1]), wup[1], wdn[1], attn=a_s.reshape(db * ds, -1), wo=wob)

    kv4 = lambda a, n, t: a.reshape(n, t, KV_HEADS, HEAD_DIM)
    return (hp.reshape(bsz, t_len, D_MODEL), hs.reshape(db, ds, D_MODEL), st_p[None], st_s[None],
            kv4(k4_p, bsz, t_len), kv4(v4_p, bsz, t_len), kv4(k4_s, db, ds), kv4(v4_s, db, ds))
```

```python
import functools
import math

import jax
import jax.numpy as jnp
from jax import lax
from jax.experimental import pallas as pl
from jax.experimental.pallas import tpu as pltpu

F32 = jnp.float32
BF16 = jnp.bfloat16

D_MODEL = 1024
D_FF = 4 * D_MODEL
NORM_EPS = 1e-6
GLA_HEADS = 4
GLA_DK = 128
GLA_DV = 256
GLA_KEY_DIM = GLA_HEADS * GLA_DK
GLA_VALUE_DIM = GLA_HEADS * GLA_DV
GATE_RANK = 16
GATE_NORMALIZER = 16.0
HEAD_DIM = 128
Q_HEADS = 8
KV_HEADS = 4
GROUP = Q_HEADS // KV_HEADS
MOBA_BLOCK = 256
MOBA_TOPK = 3
PAGE_SIZE = 128
ROPE_THETA = 10000.0
NEG_INF = -1e30
LANES = 128

VMEM_LIMIT_BYTES = 56 * 1024 * 1024

PROMPT_ROW_TILE = 256
GLA_MAX_FACTORED_SPAN = 40.0
PROMPT_GLA_CHUNK = 256
MLP_ROW_TILE = 512
MLP_FF_TILE = 1024
SAMPLE_SEQ_TILE = 4
SAMPLE_PAGES_PER_STEP = 16


def _rms(x):
    return x * lax.rsqrt(jnp.mean(x * x, axis=-1, keepdims=True) + NORM_EPS)


def _log_sigmoid(x):
    return jnp.minimum(x, 0.0) - jnp.log1p(jnp.exp(-jnp.abs(x)))


def _dot(a, b):
    return jnp.dot(a, b, preferred_element_type=F32)


def _dot_nt(a, b, precision=None):
    return lax.dot_general(a, b, (((1,), (1,)), ((), ())), precision=precision,
                           preferred_element_type=F32)


def _dot_tn(a, b):
    return lax.dot_general(a, b, (((0,), (0,)), ((), ())), preferred_element_type=F32)


def _const_spec(shape):
    n = len(shape)
    return pl.BlockSpec(shape, lambda *_: (0,) * n)


def _gla_layer_kernel(x_ref, s0_ref, gmix_ref, wmain_ref, wglr_ref, wg2_ref, bg_ref, gout_ref, wo_ref,
                      h_ref, sout_ref, proj_ref, gk_ref, b_ref, oi_ref, o_ref, s_ref, *,
                      seqs, rows_per_seq, chunk):
    t = pl.program_id(1)
    rows = seqs * rows_per_seq
    n_chunks = rows_per_seq // chunk

    x = x_ref[...].reshape(rows, D_MODEL)
    xn = (_rms(x) * gmix_ref[...]).astype(BF16)
    proj_ref[...] = _dot(xn, wmain_ref[...])
    glr = _dot(xn, wglr_ref[...])
    gpre = _dot(glr.astype(BF16), wg2_ref[...]) + bg_ref[...]
    gk_ref[...] = _log_sigmoid(gpre) * (1.0 / GATE_NORMALIZER)

    @pl.when(t == 0)
    def _():
        s_ref[...] = s0_ref[...]

    r_i = lax.broadcasted_iota(jnp.int32, (chunk, chunk), 0)
    c_i = lax.broadcasted_iota(jnp.int32, (chunk, chunk), 1)
    causal = c_i <= r_i
    mm = BF16 if chunk % 16 == 0 else F32
    tri = jnp.where(causal, 1.0, 0.0).astype(mm)
    n_sub = 2 if chunk % 32 == 0 else 1
    sub = chunk // n_sub
    half = max(sub // 2, 1)
    same_sub = causal if n_sub == 1 else jnp.logical_and(causal, (r_i >= sub) == (c_i >= sub))
    gout = gout_ref[...]
    heads = range(GLA_HEADS)
    kd = [slice(h * GLA_DK, (h + 1) * GLA_DK) for h in heads]
    vd = [slice(h * GLA_DV, (h + 1) * GLA_DV) for h in heads]
    k_off, v_off, r_off = GLA_KEY_DIM, 2 * GLA_KEY_DIM, 2 * GLA_KEY_DIM + GLA_VALUE_DIM

    def chunk_body(idx, carry):
        seq = idx // n_chunks if (seqs > 1 and n_chunks > 1) else (idx if seqs > 1 else 0)
        base = pl.multiple_of(idx * chunk, chunk)
        rs = pl.ds(base, chunk)
        g = gk_ref[rs, :]
        g_hi = g.astype(BF16).astype(F32)
        bcum = _dot(tri, g_hi.astype(mm)) + _dot(tri, (g - g_hi).astype(mm))
        b = [bcum[:, kd[h]] for h in heads]
        b_last = [b[h][chunk - 1:chunk] for h in heads]
        q = [proj_ref[rs, kd[h]] * (GLA_DK ** -0.5) for h in heads]
        k = [proj_ref[rs, k_off + h * GLA_DK:k_off + (h + 1) * GLA_DK] for h in heads]
        v = [proj_ref[rs, v_off + h * GLA_DV:v_off + (h + 1) * GLA_DV].astype(mm) for h in heads]
        s_old = [s_ref[seq, h] for h in heads]

        o_inter, kv = [], []
        for h in heads:
            q_e = q[h] * jnp.exp(b[h])
            o_inter.append(_dot(q_e.astype(BF16), s_old[h].astype(BF16)))
        for h in heads:
            k_e = k[h] * jnp.exp(b_last[h] - b[h])
            kv.append(_dot_tn(k_e.astype(mm), v[h]))
        for h in heads:
            decay_col = jnp.transpose(jnp.broadcast_to(jnp.exp(b_last[h]), (GLA_DK, GLA_DK)))
            s_ref[seq, h] = s_old[h] * jnp.concatenate([decay_col, decay_col], axis=1) + kv[h]

        q_d, k_d, a = [], [], []
        for h in heads:
            mids = jnp.concatenate(
                [jnp.broadcast_to(b[h][i * sub + half - 1:i * sub + half], (sub, GLA_DK)) for i in range(n_sub)],
                axis=0)
            q_d.append((q[h] * jnp.exp(b[h] - mids)).astype(mm))
            k_d.append((k[h] * jnp.exp(mids - b[h])).astype(mm))
        for h in heads:
            a.append(jnp.where(same_sub, _dot_nt(q_d[h], k_d[h]), 0.0).astype(mm))
        intra = [_dot(a[h], v[h]) for h in heads]
        if n_sub == 2:
            a_x = []
            for h in heads:
                edge = b[h][sub - 1:sub]
                q_x = q[h][sub:] * jnp.exp(b[h][sub:] - edge)
                k_x = k[h][:sub] * jnp.exp(edge - b[h][:sub])
                a_x.append(_dot_nt(q_x.astype(mm), k_x.astype(mm)).astype(mm))
            for h in heads:
                cross = _dot(a_x[h], v[h][:sub])
                intra[h] = jnp.concatenate([intra[h][:sub], intra[h][sub:] + cross], axis=0)

        def gated_norm(o, h):
            r = proj_ref[rs, r_off + h * GLA_DV:r_off + (h + 1) * GLA_DV]
            return _rms(o) * gout * (r * jax.nn.sigmoid(r))

        b_ref[rs, :] = bcum
        for h in heads:
            oi_ref[rs, vd[h]] = o_inter[h]
            o_ref[rs, vd[h]] = gated_norm(o_inter[h] + intra[h], h)

        span = jnp.zeros((1, GLA_DK), F32)
        for h in heads:
            prev = jnp.zeros((1, GLA_DK), F32)
            for i in range(n_sub):
                end = b[h][(i + 1) * sub - 1:(i + 1) * sub]
                span = jnp.maximum(span, prev - end)
                prev = end

        @pl.when(jnp.max(span) > GLA_MAX_FACTORED_SPAN)
        def _():
            col = lax.broadcasted_iota(jnp.int32, (chunk, GLA_DK), 0)

            def row_group(gi, c):
                grp = pl.ds(pl.multiple_of(base + gi * 8, 8), 8)
                for h in heads:
                    b_g = b_ref[grp, kd[h]]
                    q_g = proj_ref[grp, kd[h]] * (GLA_DK ** -0.5)
                    b_h = b_ref[rs, kd[h]]
                    k_h = proj_ref[rs, k_off + h * GLA_DK:k_off + (h + 1) * GLA_DK]
                    v_h = proj_ref[rs, v_off + h * GLA_DV:v_off + (h + 1) * GLA_DV]
                    out_rows = []
                    for r in range(8):
                        live = col <= gi * 8 + r
                        decay = jnp.where(live, jnp.exp(jnp.minimum(b_g[r:r + 1] - b_h, 0.0)), 0.0)
                        a_col = jnp.sum(decay * k_h * q_g[r:r + 1], axis=-1, keepdims=True)
                        out_rows.append(jnp.sum(a_col * v_h, axis=0, keepdims=True))
                    oi_ref[grp, vd[h]] += jnp.concatenate(out_rows, axis=0)
                return c

            lax.fori_loop(0, chunk // 8, row_group, 0)
            for h in heads:
                o_ref[rs, vd[h]] = gated_norm(oi_ref[rs, vd[h]], h)

        return carry

    lax.fori_loop(0, seqs * n_chunks, chunk_body, 0)

    y = _dot(o_ref[...].astype(BF16), wo_ref[...])
    h_ref[...] = (x + y).reshape(seqs, rows_per_seq, D_MODEL)

    @pl.when(t == pl.num_programs(1) - 1)
    def _():
        sout_ref[...] = s_ref[...]


def _gla_layer(x, s0, gmix, wmain, wglr, wg2, bg, gout, wo, *, seqs, rows_per_seq, chunk):
    nseq, t_len, _ = x.shape
    grid = (nseq // seqs, t_len // rows_per_seq)
    rows = seqs * rows_per_seq
    state_block = (seqs, GLA_HEADS, GLA_DK, GLA_DV)
    kern = functools.partial(_gla_layer_kernel, seqs=seqs, rows_per_seq=rows_per_seq, chunk=chunk)
    return pl.pallas_call(
        kern,
        grid=grid,
        in_specs=[
            pl.BlockSpec((seqs, rows_per_seq, D_MODEL), lambda b, t: (b, t, 0)),
            pl.BlockSpec(state_block, lambda b, t: (b, 0, 0, 0)),
            _const_spec(gmix.shape), _const_spec(wmain.shape), _const_spec(wglr.shape),
            _const_spec(wg2.shape), _const_spec(bg.shape), _const_spec(gout.shape), _const_spec(wo.shape),
        ],
        out_specs=[
            pl.BlockSpec((seqs, rows_per_seq, D_MODEL), lambda b, t: (b, t, 0)),
            pl.BlockSpec(state_block, lambda b, t: (b, 0, 0, 0)),
        ],
        out_shape=[jax.ShapeDtypeStruct(x.shape, F32), jax.ShapeDtypeStruct(s0.shape, F32)],
        scratch_shapes=[
            pltpu.VMEM((rows, 2 * GLA_KEY_DIM + 2 * GLA_VALUE_DIM), F32),
            pltpu.VMEM((rows, GLA_KEY_DIM), F32),
            pltpu.VMEM((rows, GLA_KEY_DIM), F32),
            pltpu.VMEM((rows, GLA_VALUE_DIM), F32),
            pltpu.VMEM((rows, GLA_VALUE_DIM), F32),
            pltpu.VMEM(state_block, F32),
        ],
        compiler_params=pltpu.CompilerParams(
            dimension_semantics=("arbitrary", "arbitrary"), vmem_limit_bytes=VMEM_LIMIT_BYTES),
        name="gla_layer",
    )(x, s0, gmix, wmain, wglr, wg2, bg, gout, wo)


def _mlp_kernel(*refs, has_attn):
    if has_attn:
        h_ref, a_ref, wo_ref, g_ref, wup_ref, wdn_ref, out_ref, act_ref = refs
        h = h_ref[...] + _dot(a_ref[...].astype(BF16), wo_ref[...])
    else:
        h_ref, g_ref, wup_ref, wdn_ref, out_ref, act_ref = refs
        h = h_ref[...]
    xn = (_rms(h) * g_ref[...]).astype(BF16)
    for j in range(D_FF // MLP_FF_TILE):
        ff = slice(j * MLP_FF_TILE, (j + 1) * MLP_FF_TILE)
        a = jnp.maximum(_dot(xn, wup_ref[:, ff]), 0.0)
        act_ref[:, ff] = (a * a).astype(BF16)
    out_ref[...] = h + _dot(act_ref[...], wdn_ref[...])


def _mlp(h, g, wup, wdn, attn=None, wo=None):
    n = h.shape[0]
    tm = min(MLP_ROW_TILE, n)
    row_spec = pl.BlockSpec((tm, D_MODEL), lambda i: (i, 0))
    single = pl.Buffered(1)

    def wspec(w):
        return pl.BlockSpec(w.shape, lambda i: (0, 0), pipeline_mode=single)

    if attn is None:
        args = (h, g, wup, wdn)
        in_specs = [row_spec, _const_spec(g.shape), wspec(wup), wspec(wdn)]
    else:
        args = (h, attn, wo, g, wup, wdn)
        in_specs = [row_spec, row_spec, wspec(wo), _const_spec(g.shape), wspec(wup), wspec(wdn)]
    return pl.pallas_call(
        functools.partial(_mlp_kernel, has_attn=attn is not None),
        grid=(n // tm,),
        in_specs=in_specs,
        out_specs=row_spec,
        out_shape=jax.ShapeDtypeStruct(h.shape, F32),
        scratch_shapes=[pltpu.VMEM((tm, D_FF), BF16)],
        compiler_params=pltpu.CompilerParams(
            dimension_semantics=("arbitrary",), vmem_limit_bytes=VMEM_LIMIT_BYTES),
        name="mlp",
    )(*args)


def _rope(x, cos, sin_signed):
    return x * cos + pltpu.roll(x, HEAD_DIM // 2, 1) * sin_signed


def _kvq_kernel(h_ref, nkv_ref, nq_ref, wkv_ref, wq_ref, gk_ref, gq_ref, cos_ref, sin_ref,
                k_ref, q_ref, k4_ref, v4_ref, v_ref, *, transposed_v):
    tm = h_ref.shape[0]
    xh = _rms(h_ref[...])
    kv = _dot((xh * nkv_ref[...]).astype(BF16), wkv_ref[...])
    qq = _dot((xh * nq_ref[...]).astype(BF16), wq_ref[...])
    cos = cos_ref[...]
    sin = sin_ref[...]
    kdim = KV_HEADS * HEAD_DIM
    for hh in range(KV_HEADS):
        hd = slice(hh * HEAD_DIM, (hh + 1) * HEAD_DIM)
        kh = _rope(_rms(kv[:, hd]) * gk_ref[...], cos, sin)
        vh = kv[:, kdim + hh * HEAD_DIM:kdim + (hh + 1) * HEAD_DIM]
        k_ref[:, hd] = kh
        k4_ref[pl.ds(hh, tm, stride=KV_HEADS), :] = kh
        v4_ref[pl.ds(hh, tm, stride=KV_HEADS), :] = vh
        if transposed_v:
            v_ref[0, hh] = jnp.transpose(vh).astype(BF16)
    if not transposed_v:
        v_ref[...] = kv[:, kdim:]
    for hh in range(Q_HEADS):
        hd = slice(hh * HEAD_DIM, (hh + 1) * HEAD_DIM)
        q_ref[:, hd] = _rope(_rms(qq[:, hd]) * gq_ref[...], cos, sin)


def _kvq(h, nkv, nq, wkv, wq, gk, gq, cos, sin, *, tm, seq_len=None):
    n = h.shape[0]
    pos_tiles = cos.shape[0] // tm
    kdim = KV_HEADS * HEAD_DIM
    row = lambda w: pl.BlockSpec((tm, w), lambda i: (i, 0))
    row4 = pl.BlockSpec((tm * KV_HEADS, HEAD_DIM), lambda i: (i, 0))
    tab = pl.BlockSpec((tm, HEAD_DIM), lambda i: (i % pos_tiles, 0))
    out_specs = [row(kdim), row(Q_HEADS * HEAD_DIM), row4, row4]
    out_shape = [jax.ShapeDtypeStruct((n, kdim), F32), jax.ShapeDtypeStruct((n, Q_HEADS * HEAD_DIM), F32),
                 jax.ShapeDtypeStruct((n * KV_HEADS, HEAD_DIM), F32),
                 jax.ShapeDtypeStruct((n * KV_HEADS, HEAD_DIM), F32)]
    if seq_len is not None:
        assert seq_len % tm == 0 and tm % LANES == 0
        tiles = seq_len // tm
        out_specs.append(pl.BlockSpec((1, KV_HEADS, HEAD_DIM, tm), lambda i: (i // tiles, 0, 0, i % tiles)))
        out_shape.append(jax.ShapeDtypeStruct((n // seq_len, KV_HEADS, HEAD_DIM, seq_len), BF16))
    else:
        out_specs.append(row(kdim))
        out_shape.append(jax.ShapeDtypeStruct((n, kdim), F32))
    return pl.pallas_call(
        functools.partial(_kvq_kernel, transposed_v=seq_len is not None),
        grid=(n // tm,),
        in_specs=[row(D_MODEL), _const_spec(nkv.shape), _const_spec(nq.shape), _const_spec(wkv.shape),
                  _const_spec(wq.shape), _const_spec(gk.shape), _const_spec(gq.shape), tab, tab],
        out_specs=out_specs,
        out_shape=out_shape,
        compiler_params=pltpu.CompilerParams(
            dimension_semantics=("arbitrary",), vmem_limit_bytes=VMEM_LIMIT_BYTES),
        name="kvq",
    )(h, nkv, nq, wkv, wq, gk, gq, cos, sin)


def _select_topk_rows(gate, cand, ksel):
    n = gate.shape[0]
    row = lax.broadcasted_iota(jnp.int32, gate.shape, 0).astype(F32)
    gg = jnp.where(cand, gate, -jnp.inf)
    sel = jnp.zeros(gate.shape, F32)
    for _ in range(ksel):
        mx = jnp.max(gg, axis=0, keepdims=True)
        first = jnp.min(jnp.where(gg == mx, row, float(n)), axis=0, keepdims=True)
        hit = row == first
        sel = jnp.where(hit, 1.0, sel)
        gg = jnp.where(hit, -jnp.inf, gg)
    return jnp.where(cand, sel, 0.0)


def _split_bf16(x):
    hi = x.astype(BF16)
    return hi, (x - hi.astype(F32)).astype(BF16)


def _moba_prompt_kernel(q_ref, k_ref, vt_ref, o_ref, means_ref, sel_ref, s_ref, p_ref, *, nblk):
    qi = pl.program_id(2)
    nq = GROUP * MOBA_BLOCK
    sub = 8

    @pl.when(qi == 0)
    def _():
        means_ref[...] = jnp.zeros(means_ref.shape, F32)
        for n in range(nblk):
            blk = k_ref[0, n * MOBA_BLOCK:(n + 1) * MOBA_BLOCK, :]
            means_ref[n:n + 1, :] = jnp.sum(blk, axis=0, keepdims=True) * (1.0 / MOBA_BLOCK)

    qb = q_ref[0]
    q2 = jnp.concatenate([qb[:, g * HEAD_DIM:(g + 1) * HEAD_DIM] for g in range(GROUP)], axis=0)
    m_hi, m_lo = _split_bf16(means_ref[...])
    q_hi, q_lo = _split_bf16(q2)
    gate = _dot_nt(m_hi, q_hi) + _dot_nt(m_hi, q_lo) + _dot_nt(m_lo, q_hi)
    row = lax.broadcasted_iota(jnp.int32, gate.shape, 0)
    sel_ref[...] = _select_topk_rows(gate, row < qi, min(MOBA_TOPK, nblk))
    qs = (q2 * (HEAD_DIM ** -0.5 * math.log2(math.e))).astype(BF16)

    def attend(nb):
        kpos = lax.broadcasted_iota(jnp.int32, (MOBA_BLOCK, nq), 0)
        qpos = lax.broadcasted_iota(jnp.int32, (MOBA_BLOCK, nq), 1) & (MOBA_BLOCK - 1)
        m_part = None
        for n in range(nb + 1):
            kb = k_ref[0, n * MOBA_BLOCK:(n + 1) * MOBA_BLOCK, :].astype(BF16)
            mask = (kpos <= qpos) if n == nb else (sel_ref[n:n + 1, :] > 0.0)
            s = jnp.where(mask, _dot_nt(kb, qs), NEG_INF)
            s_ref[n] = s
            part = jnp.max(s.reshape(MOBA_BLOCK // sub, sub, nq), axis=0)
            m_part = part if m_part is None else jnp.maximum(m_part, part)
        m = jnp.max(m_part, axis=0, keepdims=True)
        l_part = jnp.zeros((sub, nq), F32)
        for n in range(nb + 1):
            p = jnp.exp2(s_ref[n] - m)
            l_part = l_part + jnp.sum(p.reshape(MOBA_BLOCK // sub, sub, nq), axis=0)
            p_ref[n * MOBA_BLOCK:(n + 1) * MOBA_BLOCK, :] = p.astype(BF16)
        keys = (nb + 1) * MOBA_BLOCK
        acc = _dot(vt_ref[0, 0, :, :keys], p_ref[:keys, :])
        out = acc * (1.0 / jnp.sum(l_part, axis=0, keepdims=True))
        for g in range(GROUP):
            o_ref[0, :, g * HEAD_DIM:(g + 1) * HEAD_DIM] = jnp.transpose(
                out[:, g * MOBA_BLOCK:(g + 1) * MOBA_BLOCK])

    for nb in range(nblk):
        pl.when(qi == nb)(functools.partial(attend, nb))


def _moba_prompt(q, k, vt):
    bsz, t_len, _ = q.shape
    nblk = t_len // MOBA_BLOCK
    nblk_pad = -(-nblk // 8) * 8
    nq = GROUP * MOBA_BLOCK
    return pl.pallas_call(
        functools.partial(_moba_prompt_kernel, nblk=nblk),
        grid=(bsz, KV_HEADS, nblk),
        in_specs=[
            pl.BlockSpec((1, MOBA_BLOCK, GROUP * HEAD_DIM), lambda b, h, i: (b, i, h)),
            pl.BlockSpec((1, t_len, HEAD_DIM), lambda b, h, i: (b, 0, h)),
            pl.BlockSpec((1, 1, HEAD_DIM, t_len), lambda b, h, i: (b, h, 0, 0)),
        ],
        out_specs=pl.BlockSpec((1, MOBA_BLOCK, GROUP * HEAD_DIM), lambda b, h, i: (b, i, h)),
        out_shape=jax.ShapeDtypeStruct(q.shape, F32),
        scratch_shapes=[
            pltpu.VMEM((nblk_pad, HEAD_DIM), F32),
            pltpu.VMEM((nblk_pad, nq), F32),
            pltpu.VMEM((nblk, MOBA_BLOCK, nq), F32),
            pltpu.VMEM((t_len, nq), BF16),
        ],
        compiler_params=pltpu.CompilerParams(
            dimension_semantics=("arbitrary", "arbitrary", "arbitrary"), vmem_limit_bytes=VMEM_LIMIT_BYTES),
        name="moba_prompt",
    )(q, k, vt)


def _moba_sample_kernel(pt_ref, q_ref, knew_ref, vnew_ref, *refs, pages_per_step, n_blocks, n_new):
    del pt_ref
    kp_refs = refs[:pages_per_step]
    vp_refs = refs[pages_per_step:2 * pages_per_step]
    o_ref, gate_ref, m_ref, l_ref, acc_ref = refs[2 * pages_per_step:]
    j = pl.program_id(1)
    pages_per_block = MOBA_BLOCK // PAGE_SIZE
    blocks_per_step = pages_per_step // pages_per_block
    page_rows = PAGE_SIZE * KV_HEADS
    nkv = GROUP * n_new
    nrow = KV_HEADS * nkv
    lane = lax.broadcasted_iota(jnp.int32, (nrow, LANES), 1)
    scale = HEAD_DIM ** -0.5

    @pl.when(j == 0)
    def _():
        gate_ref[...] = jnp.zeros(gate_ref.shape, F32)
        m_ref[...] = jnp.zeros(m_ref.shape, F32)
        l_ref[...] = jnp.zeros(l_ref.shape, F32)

    qf = jnp.concatenate([q_ref[0, :, hq * HEAD_DIM:(hq + 1) * HEAD_DIM] for hq in range(Q_HEADS)], axis=0)
    qs = (qf * scale).astype(BF16)
    blk_cols = pages_per_block * page_rows
    col_head = lax.broadcasted_iota(jnp.int32, (nrow, blk_cols), 1) & (KV_HEADS - 1)
    row_head = lax.div(lax.broadcasted_iota(jnp.int32, (nrow, blk_cols), 0), nkv)
    same_head = col_head == row_head
    sub = 8

    blocks = range(blocks_per_step)
    scores, gcols = [], []
    for bi in blocks:
        kpages = [kp_refs[bi * pages_per_block + pp][0] for pp in range(pages_per_block)]
        scores.append(jnp.concatenate([_dot_nt(qs, kp.astype(BF16)) for kp in kpages], axis=1))
        ksum = sum(jnp.sum(kp.reshape(page_rows // sub, sub, HEAD_DIM), axis=0) for kp in kpages)
        mean = sum(ksum[i * KV_HEADS:(i + 1) * KV_HEADS] for i in range(sub // KV_HEADS)) * (1.0 / MOBA_BLOCK)
        mean_rows = jnp.concatenate(
            [jnp.broadcast_to(mean[h:h + 1], (nkv, HEAD_DIM)) for h in range(KV_HEADS)], axis=0)
        gcols.append(jnp.sum(qf * mean_rows, axis=-1, keepdims=True))
    probs, maxes, sums = [], [], []
    for bi in blocks:
        s = jnp.where(same_head, scores[bi], NEG_INF)
        mb = jnp.max(s, axis=-1, keepdims=True)
        p = jnp.exp(s - mb)
        maxes.append(mb)
        sums.append(jnp.sum(p, axis=-1, keepdims=True))
        probs.append(p.astype(BF16))
    gate, m_all, l_all = gate_ref[...], m_ref[...], l_ref[...]
    for bi in blocks:
        n = j * blocks_per_step + bi
        acc = _dot(probs[bi][:, :page_rows], vp_refs[bi * pages_per_block][0].astype(BF16))
        for pp in range(1, pages_per_block):
            acc = acc + _dot(probs[bi][:, pp * page_rows:(pp + 1) * page_rows],
                             vp_refs[bi * pages_per_block + pp][0].astype(BF16))
        acc_ref[n] = acc
        hot = lane == n
        gate = jnp.where(hot, gcols[bi], gate)
        m_all = jnp.where(hot, maxes[bi], m_all)
        l_all = jnp.where(hot, sums[bi], l_all)
    gate_ref[...] = gate
    m_ref[...] = m_all
    l_ref[...] = l_all

    @pl.when(j == pl.num_programs(1) - 1)
    def _():
        lane = lax.broadcasted_iota(jnp.int32, (nkv, LANES), 1)
        qrow_t = lax.rem(lax.broadcasted_iota(jnp.int32, (nkv, LANES), 0), n_new)
        lane_f = lane.astype(F32)
        for h in range(KV_HEADS):
            hd = slice(h * HEAD_DIM, (h + 1) * HEAD_DIM)
            rs = slice(h * nkv, (h + 1) * nkv)
            qsc = qf[rs] * scale
            gg = jnp.where(lane < n_blocks, gate_ref[rs, :], -jnp.inf)
            picked = jnp.zeros((nkv, LANES), F32)
            for _ in range(min(MOBA_TOPK, n_blocks)):
                mx = jnp.max(gg, axis=-1, keepdims=True)
                first = jnp.min(jnp.where(gg == mx, lane_f, float(LANES)), axis=-1, keepdims=True)
                hit = lane_f == first
                picked = jnp.where(hit, 1.0, picked)
                gg = jnp.where(hit, -jnp.inf, gg)
            sel = jnp.where(lane < n_blocks, picked, 0.0) > 0.0
            knew = knew_ref[0, :, hd]
            vnew = vnew_ref[0, :, hd]
            s_own = jnp.full((nkv, LANES), NEG_INF, F32)
            for jj in range(n_new):
                col = jnp.sum(qsc * knew[jj:jj + 1, :], axis=-1, keepdims=True)
                s_own = jnp.where(jnp.logical_and(lane == jj, qrow_t >= jj), col, s_own)
            mh = m_ref[rs, :]
            m_tot = jnp.maximum(jnp.max(s_own, axis=-1, keepdims=True),
                                jnp.max(jnp.where(sel, mh, NEG_INF), axis=-1, keepdims=True))
            w = jnp.where(sel, jnp.exp(mh - m_tot), 0.0)
            p_own = jnp.exp(s_own - m_tot)
            den = jnp.sum(w * l_ref[rs, :], axis=-1, keepdims=True) + jnp.sum(p_own, axis=-1, keepdims=True)
            num = jnp.zeros((nkv, HEAD_DIM), F32)
            for jj in range(n_new):
                num = num + p_own[:, jj:jj + 1] * vnew[jj:jj + 1, :]
            for n in range(n_blocks):
                num = num + w[:, n:n + 1] * acc_ref[n, rs, :]
            out = num / den
            for g in range(GROUP):
                o_ref[0, :, (h * GROUP + g) * HEAD_DIM:(h * GROUP + g + 1) * HEAD_DIM] = (
                    out[g * n_new:(g + 1) * n_new, :])


def _moba_sample(q, knew, vnew, cache_k, cache_v, page_table):
    db, ds, _ = q.shape
    n_pages = page_table.shape[1]
    pages_per_block = MOBA_BLOCK // PAGE_SIZE
    assert n_pages % pages_per_block == 0, "past length must be a whole number of MoBA blocks"
    n_blocks = n_pages // pages_per_block
    assert 0 < n_blocks <= LANES
    assert KV_HEADS & (KV_HEADS - 1) == 0 and 8 % KV_HEADS == 0
    pps = math.gcd(SAMPLE_PAGES_PER_STEP, n_pages)
    assert pps % pages_per_block == 0
    nrow = Q_HEADS * ds
    kdim = KV_HEADS * HEAD_DIM

    def page_spec(i):
        return pl.BlockSpec((1, PAGE_SIZE * KV_HEADS, HEAD_DIM),
                            lambda b, j, pt: (pt[b * n_pages + j * pps + i], 0, 0))

    per_seq = lambda w: pl.BlockSpec((1, ds, w), lambda b, j, pt: (b, 0, 0))
    grid_spec = pltpu.PrefetchScalarGridSpec(
        num_scalar_prefetch=1,
        grid=(db, n_pages // pps),
        in_specs=[per_seq(Q_HEADS * HEAD_DIM), per_seq(kdim), per_seq(kdim)]
                 + [page_spec(i) for i in range(pps)] + [page_spec(i) for i in range(pps)],
        out_specs=per_seq(Q_HEADS * HEAD_DIM),
        scratch_shapes=[
            pltpu.VMEM((nrow, LANES), F32),
            pltpu.VMEM((nrow, LANES), F32),
            pltpu.VMEM((nrow, LANES), F32),
            pltpu.VMEM((n_blocks, nrow, HEAD_DIM), F32),
        ],
    )
    return pl.pallas_call(
        functools.partial(_moba_sample_kernel, pages_per_step=pps, n_blocks=n_blocks, n_new=ds),
        grid_spec=grid_spec,
        out_shape=jax.ShapeDtypeStruct(q.shape, F32),
        compiler_params=pltpu.CompilerParams(
            dimension_semantics=("arbitrary", "arbitrary"), vmem_limit_bytes=VMEM_LIMIT_BYTES),
        name="moba_sample",
    )(page_table.reshape(-1), q, knew, vnew, *([cache_k] * pps), *([cache_v] * pps))


def _rope_tables(pos):
    inv = ROPE_THETA ** (-jnp.arange(0, HEAD_DIM, 2, dtype=F32) / HEAD_DIM)
    ang = pos.astype(F32)[:, None] * inv[None, :]
    cos, sin = jnp.cos(ang), jnp.sin(ang)
    return jnp.concatenate([cos, cos], axis=-1), jnp.concatenate([-sin, sin], axis=-1)


def kernel(x_prompt, x_sample, state_gla, cache_k, cache_v, page_table, w_in_a, w_g2, b_g, g_gla_out, w_o_a,
           norm_kv, w_kv, g_k, w_q_b, g_q, w_o_b, norm_mix, norm_mlp, w_up, w_down):
    bsz, t_len, _ = x_prompt.shape
    db, ds, _ = x_sample.shape
    n_a = state_gla.shape[0]
    depth = norm_mix.shape[0]
    assert n_a == 1 and depth == 2, "one GLA layer followed by one MoBA layer"
    assert t_len % PROMPT_ROW_TILE == 0 and PROMPT_ROW_TILE % PROMPT_GLA_CHUNK == 0
    assert t_len % MOBA_BLOCK == 0
    past = page_table.shape[1] * PAGE_SIZE
    kdim = KV_HEADS * HEAD_DIM
    row2 = lambda a: a.reshape(1, -1)

    main = 2 * GLA_KEY_DIM + 2 * GLA_VALUE_DIM
    wmain = w_in_a[0, :, :main].astype(BF16)
    wglr = jnp.pad(w_in_a[0, :, main:], ((0, 0), (0, LANES - GATE_RANK))).astype(BF16)
    wg2 = jnp.pad(w_g2[0], ((0, LANES - GATE_RANK), (0, 0))).astype(BF16)
    gla_w = (row2(norm_mix[0]), wmain, wglr, wg2, row2(b_g[0]), row2(g_gla_out[0]), w_o_a[0].astype(BF16))
    s0_p = jnp.zeros((bsz,) + state_gla.shape[2:], F32)
    seq_tile = math.gcd(SAMPLE_SEQ_TILE, db)
    hp, st_p = _gla_layer(x_prompt, s0_p, *gla_w, seqs=1, rows_per_seq=PROMPT_ROW_TILE,
                          chunk=PROMPT_GLA_CHUNK)
    hs, st_s = _gla_layer(x_sample, state_gla[0], *gla_w, seqs=seq_tile, rows_per_seq=ds, chunk=ds)
    hp = hp.reshape(bsz * t_len, D_MODEL)
    hs = hs.reshape(db * ds, D_MODEL)
    wup = w_up.astype(BF16)
    wdn = w_down.astype(BF16)
    hp = _mlp(hp, row2(norm_mlp[0]), wup[0], wdn[0])
    hs = _mlp(hs, row2(norm_mlp[0]), wup[0], wdn[0])

    cos_p, sin_p = _rope_tables(jnp.arange(t_len, dtype=jnp.int32))
    cos_s, sin_s = _rope_tables(past + jnp.arange(ds, dtype=jnp.int32))
    cos_s, sin_s = jnp.tile(cos_s, (db, 1)), jnp.tile(sin_s, (db, 1))
    kvq_w = (row2(norm_kv), row2(norm_mix[1]), w_kv.astype(BF16), w_q_b[0].astype(BF16), row2(g_k), row2(g_q[0]))
    k_p, q_p, k4_p, v4_p, vt_p = _kvq(hp, *kvq_w, cos_p, sin_p, tm=PROMPT_ROW_TILE, seq_len=t_len)
    k_s, q_s, k4_s, v4_s, v_s = _kvq(hs, *kvq_w, cos_s, sin_s, tm=db * ds)

    a_p = _moba_prompt(q_p.reshape(bsz, t_len, -1), k_p.reshape(bsz, t_len, kdim), vt_p)
    pages = lambda c: c.reshape(c.shape[0], PAGE_SIZE * KV_HEADS, HEAD_DIM)
    a_s = _moba_sample(q_s.reshape(db, ds, -1), k_s.reshape(db, ds, kdim), v_s.reshape(db, ds, kdim),
                       pages(cache_k), pages(cache_v), page_table)
    wob = w_o_b[0].astype(BF16)
    hp = _mlp(hp, row2(norm_mlp[1]), wup[1], wdn[1], attn=a_p.reshape(bsz * t_len, -1), wo=wob)
    hs = _mlp(hs, row2(norm_mlp[1]), wup[1], wdn[1], attn=a_s.reshape(db * ds, -1), wo=wob)

    kv4 = lambda a, n, t: a.reshape(n, t, KV_HEADS, HEAD_DIM)
    return (hp.reshape(bsz, t_len, D_MODEL), hs.reshape(db, ds, D_MODEL), st_p[None], st_s[None],
            kv4(k4_p, bsz, t_len), kv4(v4_p, bsz, t_len), kv4(k4_s, db, ds), kv4(v4_s, db, ds))
```

```python
import functools
import math

import jax
import jax.numpy as jnp
from jax import lax
from jax.experimental import pallas as pl
from jax.experimental.pallas import tpu as pltpu

F32 = jnp.float32
BF16 = jnp.bfloat16

D_MODEL = 1024
D_FF = 4 * D_MODEL
NORM_EPS = 1e-6
GLA_HEADS = 4
GLA_DK = 128
GLA_DV = 256
GLA_KEY_DIM = GLA_HEADS * GLA_DK
GLA_VALUE_DIM = GLA_HEADS * GLA_DV
GATE_RANK = 16
GATE_NORMALIZER = 16.0
HEAD_DIM = 128
Q_HEADS = 8
KV_HEADS = 4
GROUP = Q_HEADS // KV_HEADS
MOBA_BLOCK = 256
MOBA_TOPK = 3
PAGE_SIZE = 128
ROPE_THETA = 10000.0
NEG_INF = -1e30
LANES = 128

VMEM_LIMIT_BYTES = 56 * 1024 * 1024

PROMPT_ROW_TILE = 256
GLA_MAX_FACTORED_SPAN = 40.0
PROMPT_GLA_CHUNK = 256
MLP_ROW_TILE = 512
MLP_FF_TILE = 1024
SAMPLE_SEQ_TILE = 4
SAMPLE_PAGES_PER_STEP = 16
SAMPLE_PAGE_SLOTS = 3


def _rms(x):
    return x * lax.rsqrt(jnp.mean(x * x, axis=-1, keepdims=True) + NORM_EPS)


def _log_sigmoid(x):
    return jnp.minimum(x, 0.0) - jnp.log1p(jnp.exp(-jnp.abs(x)))


def _dot(a, b):
    return jnp.dot(a, b, preferred_element_type=F32)


def _dot_nt(a, b, precision=None):
    return lax.dot_general(a, b, (((1,), (1,)), ((), ())), precision=precision,
                           preferred_element_type=F32)


def _dot_tn(a, b):
    return lax.dot_general(a, b, (((0,), (0,)), ((), ())), preferred_element_type=F32)


def _const_spec(shape):
    n = len(shape)
    return pl.BlockSpec(shape, lambda *_: (0,) * n)


def _gla_layer_kernel(x_ref, s0_ref, gmix_ref, wmain_ref, wglr_ref, wg2_ref, bg_ref, gout_ref, wo_ref,
                      h_ref, sout_ref, proj_ref, gk_ref, b_ref, oi_ref, o_ref, s_ref, *,
                      seqs, rows_per_seq, chunk):
    t = pl.program_id(1)
    rows = seqs * rows_per_seq
    n_chunks = rows_per_seq // chunk

    x = x_ref[...].reshape(rows, D_MODEL)
    xn = (_rms(x) * gmix_ref[...]).astype(BF16)
    proj_ref[...] = _dot(xn, wmain_ref[...])
    glr = _dot(xn, wglr_ref[...])
    gpre = _dot(glr.astype(BF16), wg2_ref[...]) + bg_ref[...]
    gk_ref[...] = _log_sigmoid(gpre) * (1.0 / GATE_NORMALIZER)

    @pl.when(t == 0)
    def _():
        s_ref[...] = s0_ref[...]

    r_i = lax.broadcasted_iota(jnp.int32, (chunk, chunk), 0)
    c_i = lax.broadcasted_iota(jnp.int32, (chunk, chunk), 1)
    causal = c_i <= r_i
    mm = BF16 if chunk % 16 == 0 else F32
    tri = jnp.where(causal, 1.0, 0.0).astype(mm)
    n_sub = 2 if chunk % 32 == 0 else 1
    sub = chunk // n_sub
    half = max(sub // 2, 1)
    same_sub = causal if n_sub == 1 else jnp.logical_and(causal, (r_i >= sub) == (c_i >= sub))
    gout = gout_ref[...]
    heads = range(GLA_HEADS)
    kd = [slice(h * GLA_DK, (h + 1) * GLA_DK) for h in heads]
    vd = [slice(h * GLA_DV, (h + 1) * GLA_DV) for h in heads]
    k_off, v_off, r_off = GLA_KEY_DIM, 2 * GLA_KEY_DIM, 2 * GLA_KEY_DIM + GLA_VALUE_DIM

    def chunk_body(idx, carry):
        seq = idx // n_chunks if (seqs > 1 and n_chunks > 1) else (idx if seqs > 1 else 0)
        base = pl.multiple_of(idx * chunk, chunk)
        rs = pl.ds(base, chunk)
        g = gk_ref[rs, :]
        g_hi = g.astype(BF16).astype(F32)
        bcum = _dot(tri, g_hi.astype(mm)) + _dot(tri, (g - g_hi).astype(mm))
        b = [bcum[:, kd[h]] for h in heads]
        b_last = [b[h][chunk - 1:chunk] for h in heads]
        q = [proj_ref[rs, kd[h]] * (GLA_DK ** -0.5) for h in heads]
        k = [proj_ref[rs, k_off + h * GLA_DK:k_off + (h + 1) * GLA_DK] for h in heads]
        v = [proj_ref[rs, v_off + h * GLA_DV:v_off + (h + 1) * GLA_DV].astype(mm) for h in heads]
        s_old = [s_ref[seq, h] for h in heads]

        o_inter, kv = [], []
        for h in heads:
            q_e = q[h] * jnp.exp(b[h])
            o_inter.append(_dot(q_e.astype(BF16), s_old[h].astype(BF16)))
        for h in heads:
            k_e = k[h] * jnp.exp(b_last[h] - b[h])
            kv.append(_dot_tn(k_e.astype(mm), v[h]))
        for h in heads:
            decay_col = jnp.transpose(jnp.broadcast_to(jnp.exp(b_last[h]), (GLA_DK, GLA_DK)))
            s_ref[seq, h] = s_old[h] * jnp.concatenate([decay_col, decay_col], axis=1) + kv[h]

        q_d, k_d, a = [], [], []
        for h in heads:
            mids = jnp.concatenate(
                [jnp.broadcast_to(b[h][i * sub + half - 1:i * sub + half], (sub, GLA_DK)) for i in range(n_sub)],
                axis=0)
            q_d.append((q[h] * jnp.exp(b[h] - mids)).astype(mm))
            k_d.append((k[h] * jnp.exp(mids - b[h])).astype(mm))
        for h in heads:
            a.append(jnp.where(same_sub, _dot_nt(q_d[h], k_d[h]), 0.0).astype(mm))
        intra = [_dot(a[h], v[h]) for h in heads]
        if n_sub == 2:
            a_x = []
            for h in heads:
                edge = b[h][sub - 1:sub]
                q_x = q[h][sub:] * jnp.exp(b[h][sub:] - edge)
                k_x = k[h][:sub] * jnp.exp(edge - b[h][:sub])
                a_x.append(_dot_nt(q_x.astype(mm), k_x.astype(mm)).astype(mm))
            for h in heads:
                cross = _dot(a_x[h], v[h][:sub])
                intra[h] = jnp.concatenate([intra[h][:sub], intra[h][sub:] + cross], axis=0)

        def gated_norm(o, h):
            r = proj_ref[rs, r_off + h * GLA_DV:r_off + (h + 1) * GLA_DV]
            return _rms(o) * gout * (r * jax.nn.sigmoid(r))

        b_ref[rs, :] = bcum
        for h in heads:
            oi_ref[rs, vd[h]] = o_inter[h]
            o_ref[rs, vd[h]] = gated_norm(o_inter[h] + intra[h], h)

        span = jnp.zeros((1, GLA_DK), F32)
        for h in heads:
            prev = jnp.zeros((1, GLA_DK), F32)
            for i in range(n_sub):
                end = b[h][(i + 1) * sub - 1:(i + 1) * sub]
                span = jnp.maximum(span, prev - end)
                prev = end

        @pl.when(jnp.max(span) > GLA_MAX_FACTORED_SPAN)
        def _():
            col = lax.broadcasted_iota(jnp.int32, (chunk, GLA_DK), 0)

            def row_group(gi, c):
                grp = pl.ds(pl.multiple_of(base + gi * 8, 8), 8)
                for h in heads:
                    b_g = b_ref[grp, kd[h]]
                    q_g = proj_ref[grp, kd[h]] * (GLA_DK ** -0.5)
                    b_h = b_ref[rs, kd[h]]
                    k_h = proj_ref[rs, k_off + h * GLA_DK:k_off + (h + 1) * GLA_DK]
                    v_h = proj_ref[rs, v_off + h * GLA_DV:v_off + (h + 1) * GLA_DV]
                    out_rows = []
                    for r in range(8):
                        live = col <= gi * 8 + r
                        decay = jnp.where(live, jnp.exp(jnp.minimum(b_g[r:r + 1] - b_h, 0.0)), 0.0)
                        a_col = jnp.sum(decay * k_h * q_g[r:r + 1], axis=-1, keepdims=True)
                        out_rows.append(jnp.sum(a_col * v_h, axis=0, keepdims=True))
                    oi_ref[grp, vd[h]] += jnp.concatenate(out_rows, axis=0)
                return c

            lax.fori_loop(0, chunk // 8, row_group, 0)
            for h in heads:
                o_ref[rs, vd[h]] = gated_norm(oi_ref[rs, vd[h]], h)

        return carry

    lax.fori_loop(0, seqs * n_chunks, chunk_body, 0)

    y = _dot(o_ref[...].astype(BF16), wo_ref[...])
    h_ref[...] = (x + y).reshape(seqs, rows_per_seq, D_MODEL)

    @pl.when(t == pl.num_programs(1) - 1)
    def _():
        sout_ref[...] = s_ref[...]


def _gla_layer(x, s0, gmix, wmain, wglr, wg2, bg, gout, wo, *, seqs, rows_per_seq, chunk):
    nseq, t_len, _ = x.shape
    grid = (nseq // seqs, t_len // rows_per_seq)
    rows = seqs * rows_per_seq
    state_block = (seqs, GLA_HEADS, GLA_DK, GLA_DV)
    kern = functools.partial(_gla_layer_kernel, seqs=seqs, rows_per_seq=rows_per_seq, chunk=chunk)
    return pl.pallas_call(
        kern,
        grid=grid,
        in_specs=[
            pl.BlockSpec((seqs, rows_per_seq, D_MODEL), lambda b, t: (b, t, 0)),
            pl.BlockSpec(state_block, lambda b, t: (b, 0, 0, 0)),
            _const_spec(gmix.shape), _const_spec(wmain.shape), _const_spec(wglr.shape),
            _const_spec(wg2.shape), _const_spec(bg.shape), _const_spec(gout.shape), _const_spec(wo.shape),
        ],
        out_specs=[
            pl.BlockSpec((seqs, rows_per_seq, D_MODEL), lambda b, t: (b, t, 0)),
            pl.BlockSpec(state_block, lambda b, t: (b, 0, 0, 0)),
        ],
        out_shape=[jax.ShapeDtypeStruct(x.shape, F32), jax.ShapeDtypeStruct(s0.shape, F32)],
        scratch_shapes=[
            pltpu.VMEM((rows, 2 * GLA_KEY_DIM + 2 * GLA_VALUE_DIM), F32),
            pltpu.VMEM((rows, GLA_KEY_DIM), F32),
            pltpu.VMEM((rows, GLA_KEY_DIM), F32),
            pltpu.VMEM((rows, GLA_VALUE_DIM), F32),
            pltpu.VMEM((rows, GLA_VALUE_DIM), F32),
            pltpu.VMEM(state_block, F32),
        ],
        compiler_params=pltpu.CompilerParams(
            dimension_semantics=("arbitrary", "arbitrary"), vmem_limit_bytes=VMEM_LIMIT_BYTES),
        name="gla_layer",
    )(x, s0, gmix, wmain, wglr, wg2, bg, gout, wo)


def _mlp_kernel(*refs, has_attn):
    if has_attn:
        h_ref, a_ref, wo_ref, g_ref, wup_ref, wdn_ref, out_ref, act_ref = refs
        h = h_ref[...] + _dot(a_ref[...].astype(BF16), wo_ref[...])
    else:
        h_ref, g_ref, wup_ref, wdn_ref, out_ref, act_ref = refs
        h = h_ref[...]
    xn = (_rms(h) * g_ref[...]).astype(BF16)
    for j in range(D_FF // MLP_FF_TILE):
        ff = slice(j * MLP_FF_TILE, (j + 1) * MLP_FF_TILE)
        a = jnp.maximum(_dot(xn, wup_ref[:, ff]), 0.0)
        act_ref[:, ff] = (a * a).astype(BF16)
    out_ref[...] = h + _dot(act_ref[...], wdn_ref[...])


def _mlp(h, g, wup, wdn, attn=None, wo=None):
    n = h.shape[0]
    tm = min(MLP_ROW_TILE, n)
    row_spec = pl.BlockSpec((tm, D_MODEL), lambda i: (i, 0))
    single = pl.Buffered(1)

    def wspec(w):
        return pl.BlockSpec(w.shape, lambda i: (0, 0), pipeline_mode=single)

    if attn is None:
        args = (h, g, wup, wdn)
        in_specs = [row_spec, _const_spec(g.shape), wspec(wup), wspec(wdn)]
    else:
        args = (h, attn, wo, g, wup, wdn)
        in_specs = [row_spec, row_spec, wspec(wo), _const_spec(g.shape), wspec(wup), wspec(wdn)]
    return pl.pallas_call(
        functools.partial(_mlp_kernel, has_attn=attn is not None),
        grid=(n // tm,),
        in_specs=in_specs,
        out_specs=row_spec,
        out_shape=jax.ShapeDtypeStruct(h.shape, F32),
        scratch_shapes=[pltpu.VMEM((tm, D_FF), BF16)],
        compiler_params=pltpu.CompilerParams(
            dimension_semantics=("arbitrary",), vmem_limit_bytes=VMEM_LIMIT_BYTES),
        name="mlp",
    )(*args)


def _rope(x, cos, sin_signed):
    return x * cos + pltpu.roll(x, HEAD_DIM // 2, 1) * sin_signed


def _kvq_kernel(h_ref, nkv_ref, nq_ref, wkv_ref, wq_ref, gk_ref, gq_ref, cos_ref, sin_ref,
                k_ref, q_ref, k4_ref, v4_ref, v_ref, *, transposed_v):
    tm = h_ref.shape[0]
    xh = _rms(h_ref[...])
    kv = _dot((xh * nkv_ref[...]).astype(BF16), wkv_ref[...])
    qq = _dot((xh * nq_ref[...]).astype(BF16), wq_ref[...])
    cos = cos_ref[...]
    sin = sin_ref[...]
    kdim = KV_HEADS * HEAD_DIM
    for hh in range(KV_HEADS):
        hd = slice(hh * HEAD_DIM, (hh + 1) * HEAD_DIM)
        kh = _rope(_rms(kv[:, hd]) * gk_ref[...], cos, sin)
        vh = kv[:, kdim + hh * HEAD_DIM:kdim + (hh + 1) * HEAD_DIM]
        k_ref[:, hd] = kh
        k4_ref[pl.ds(hh, tm, stride=KV_HEADS), :] = kh
        v4_ref[pl.ds(hh, tm, stride=KV_HEADS), :] = vh
        if transposed_v:
            v_ref[0, hh] = jnp.transpose(vh).astype(BF16)
    if not transposed_v:
        v_ref[...] = kv[:, kdim:]
    for hh in range(Q_HEADS):
        hd = slice(hh * HEAD_DIM, (hh + 1) * HEAD_DIM)
        q_ref[:, hd] = _rope(_rms(qq[:, hd]) * gq_ref[...], cos, sin)


def _kvq(h, nkv, nq, wkv, wq, gk, gq, cos, sin, *, tm, seq_len=None):
    n = h.shape[0]
    pos_tiles = cos.shape[0] // tm
    kdim = KV_HEADS * HEAD_DIM
    row = lambda w: pl.BlockSpec((tm, w), lambda i: (i, 0))
    row4 = pl.BlockSpec((tm * KV_HEADS, HEAD_DIM), lambda i: (i, 0))
    tab = pl.BlockSpec((tm, HEAD_DIM), lambda i: (i % pos_tiles, 0))
    out_specs = [row(kdim), row(Q_HEADS * HEAD_DIM), row4, row4]
    out_shape = [jax.ShapeDtypeStruct((n, kdim), F32), jax.ShapeDtypeStruct((n, Q_HEADS * HEAD_DIM), F32),
                 jax.ShapeDtypeStruct((n * KV_HEADS, HEAD_DIM), F32),
                 jax.ShapeDtypeStruct((n * KV_HEADS, HEAD_DIM), F32)]
    if seq_len is not None:
        assert seq_len % tm == 0 and tm % LANES == 0
        tiles = seq_len // tm
        out_specs.append(pl.BlockSpec((1, KV_HEADS, HEAD_DIM, tm), lambda i: (i // tiles, 0, 0, i % tiles)))
        out_shape.append(jax.ShapeDtypeStruct((n // seq_len, KV_HEADS, HEAD_DIM, seq_len), BF16))
    else:
        out_specs.append(row(kdim))
        out_shape.append(jax.ShapeDtypeStruct((n, kdim), F32))
    return pl.pallas_call(
        functools.partial(_kvq_kernel, transposed_v=seq_len is not None),
        grid=(n // tm,),
        in_specs=[row(D_MODEL), _const_spec(nkv.shape), _const_spec(nq.shape), _const_spec(wkv.shape),
                  _const_spec(wq.shape), _const_spec(gk.shape), _const_spec(gq.shape), tab, tab],
        out_specs=out_specs,
        out_shape=out_shape,
        compiler_params=pltpu.CompilerParams(
            dimension_semantics=("arbitrary",), vmem_limit_bytes=VMEM_LIMIT_BYTES),
        name="kvq",
    )(h, nkv, nq, wkv, wq, gk, gq, cos, sin)


def _select_topk_rows(gate, cand, ksel):
    n = gate.shape[0]
    row = lax.broadcasted_iota(jnp.int32, gate.shape, 0).astype(F32)
    gg = jnp.where(cand, gate, -jnp.inf)
    sel = jnp.zeros(gate.shape, F32)
    for _ in range(ksel):
        mx = jnp.max(gg, axis=0, keepdims=True)
        first = jnp.min(jnp.where(gg == mx, row, float(n)), axis=0, keepdims=True)
        hit = row == first
        sel = jnp.where(hit, 1.0, sel)
        gg = jnp.where(hit, -jnp.inf, gg)
    return jnp.where(cand, sel, 0.0)


def _split_bf16(x):
    hi = x.astype(BF16)
    return hi, (x - hi.astype(F32)).astype(BF16)


def _moba_prompt_kernel(q_ref, k_ref, vt_ref, o_ref, means_ref, s_ref, p_ref, *, nblk):
    qi = pl.program_id(2)
    sub = 8

    @pl.when(qi == 0)
    def _():
        means_ref[...] = jnp.zeros(means_ref.shape, F32)
        for n in range(nblk):
            blk = k_ref[0, n * MOBA_BLOCK:(n + 1) * MOBA_BLOCK, :]
            means_ref[n:n + 1, :] = jnp.sum(blk, axis=0, keepdims=True) * (1.0 / MOBA_BLOCK)

    def attend(nb):
        qb = q_ref[0]
        heads = [qb[:, g * HEAD_DIM:(g + 1) * HEAD_DIM] for g in range(GROUP)]
        qs = [(heads[g] * (HEAD_DIM ** -0.5 * math.log2(math.e))).astype(BF16) for g in range(GROUP)]
        causal = (lax.broadcasted_iota(jnp.int32, (MOBA_BLOCK, MOBA_BLOCK), 0)
                  <= lax.broadcasted_iota(jnp.int32, (MOBA_BLOCK, MOBA_BLOCK), 1))
        ksel = min(MOBA_TOPK, nblk)
        kbs = [k_ref[0, n * MOBA_BLOCK:(n + 1) * MOBA_BLOCK, :].astype(BF16) for n in range(nb + 1)]

        sel = [None] * GROUP
        if nb > ksel:
            m_hi, m_lo = _split_bf16(means_ref[...])
            row = lax.broadcasted_iota(jnp.int32, (means_ref.shape[0], MOBA_BLOCK), 0)
            for g in range(GROUP):
                q_hi, q_lo = _split_bf16(heads[g])
                gate = _dot_nt(m_hi, q_hi) + _dot_nt(m_hi, q_lo) + _dot_nt(m_lo, q_hi)
                sel[g] = _select_topk_rows(gate, row < nb, ksel)
        m_parts = []
        for g in range(GROUP):
            m_part = None
            for n in range(nb + 1):
                s = _dot_nt(kbs[n], qs[g])
                if n == nb:
                    s = jnp.where(causal, s, NEG_INF)
                elif sel[g] is not None:
                    s = jnp.where(sel[g][n:n + 1, :] > 0.0, s, NEG_INF)
                s_ref[g, n] = s
                part = jnp.max(s.reshape(MOBA_BLOCK // sub, sub, MOBA_BLOCK), axis=0)
                m_part = part if m_part is None else jnp.maximum(m_part, part)
            m_parts.append(m_part)
        keys = (nb + 1) * MOBA_BLOCK
        for g in range(GROUP):
            m = jnp.max(m_parts[g], axis=0, keepdims=True)
            l_part = jnp.zeros((sub, MOBA_BLOCK), F32)
            for n in range(nb + 1):
                p = jnp.exp2(s_ref[g, n] - m)
                l_part = l_part + jnp.sum(p.reshape(MOBA_BLOCK // sub, sub, MOBA_BLOCK), axis=0)
                p_ref[g, n * MOBA_BLOCK:(n + 1) * MOBA_BLOCK, :] = p.astype(BF16)
            acc = _dot(vt_ref[0, 0, :, :keys], p_ref[g, :keys, :])
            out = acc * (1.0 / jnp.sum(l_part, axis=0, keepdims=True))
            o_ref[0, :, g * HEAD_DIM:(g + 1) * HEAD_DIM] = jnp.transpose(out)

    for nb in range(nblk):
        pl.when(qi == nb)(functools.partial(attend, nb))


def _moba_prompt(q, k, vt):
    bsz, t_len, _ = q.shape
    nblk = t_len // MOBA_BLOCK
    nblk_pad = -(-nblk // 8) * 8
    return pl.pallas_call(
        functools.partial(_moba_prompt_kernel, nblk=nblk),
        grid=(bsz, KV_HEADS, nblk),
        in_specs=[
            pl.BlockSpec((1, MOBA_BLOCK, GROUP * HEAD_DIM), lambda b, h, i: (b, i, h)),
            pl.BlockSpec((1, t_len, HEAD_DIM), lambda b, h, i: (b, 0, h)),
            pl.BlockSpec((1, 1, HEAD_DIM, t_len), lambda b, h, i: (b, h, 0, 0)),
        ],
        out_specs=pl.BlockSpec((1, MOBA_BLOCK, GROUP * HEAD_DIM), lambda b, h, i: (b, i, h)),
        out_shape=jax.ShapeDtypeStruct(q.shape, F32),
        scratch_shapes=[
            pltpu.VMEM((nblk_pad, HEAD_DIM), F32),
            pltpu.VMEM((GROUP, nblk, MOBA_BLOCK, MOBA_BLOCK), F32),
            pltpu.VMEM((GROUP, t_len, MOBA_BLOCK), BF16),
        ],
        compiler_params=pltpu.CompilerParams(
            dimension_semantics=("arbitrary", "arbitrary", "arbitrary"), vmem_limit_bytes=VMEM_LIMIT_BYTES),
        name="moba_prompt",
    )(q, k, vt)


def _moba_sample_kernel(pt_ref, q_ref, knew_ref, vnew_ref, ck_ref, cv_ref, o_ref,
                        kbuf, vbuf, sem, gate_ref, m_ref, l_ref, acc_ref, *, pages_per_step, n_blocks, n_new):
    j = pl.program_id(1)
    steps_per_seq = pl.num_programs(1)
    step = pl.program_id(0) * steps_per_seq + j
    n_steps = pl.num_programs(0) * steps_per_seq
    ahead = SAMPLE_PAGE_SLOTS - 1

    def page_copies(s, slot):
        out = []
        for i in range(pages_per_step):
            page = pt_ref[s * pages_per_step + i]
            out.append(pltpu.make_async_copy(ck_ref.at[page], kbuf.at[slot, i], sem.at[0, slot]))
            out.append(pltpu.make_async_copy(cv_ref.at[page], vbuf.at[slot, i], sem.at[1, slot]))
        return out

    @pl.when(step == 0)
    def _():
        for s in range(ahead):
            for c in page_copies(s, s):
                c.start()

    @pl.when(step + ahead < n_steps)
    def _():
        for c in page_copies(step + ahead, lax.rem(step + ahead, SAMPLE_PAGE_SLOTS)):
            c.start()

    slot = lax.rem(step, SAMPLE_PAGE_SLOTS)
    for c in page_copies(step, slot):
        c.wait()

    pages_per_block = MOBA_BLOCK // PAGE_SIZE
    blocks_per_step = pages_per_step // pages_per_block
    page_rows = PAGE_SIZE * KV_HEADS
    nkv = GROUP * n_new
    nrow = KV_HEADS * nkv
    lane = lax.broadcasted_iota(jnp.int32, (nrow, LANES), 1)
    scale = HEAD_DIM ** -0.5

    @pl.when(j == 0)
    def _():
        gate_ref[...] = jnp.zeros(gate_ref.shape, F32)
        m_ref[...] = jnp.zeros(m_ref.shape, F32)
        l_ref[...] = jnp.zeros(l_ref.shape, F32)

    qf = jnp.concatenate([q_ref[0, :, hq * HEAD_DIM:(hq + 1) * HEAD_DIM] for hq in range(Q_HEADS)], axis=0)
    qs = (qf * scale).astype(BF16)
    blk_cols = pages_per_block * page_rows
    col_head = lax.broadcasted_iota(jnp.int32, (nrow, blk_cols), 1) & (KV_HEADS - 1)
    row_head = lax.div(lax.broadcasted_iota(jnp.int32, (nrow, blk_cols), 0), nkv)
    same_head = col_head == row_head
    sub = 8

    blocks = range(blocks_per_step)
    scores, gcols = [], []
    for bi in blocks:
        kpages = [kbuf[slot, bi * pages_per_block + pp] for pp in range(pages_per_block)]
        scores.append(jnp.concatenate([_dot_nt(qs, kp.astype(BF16)) for kp in kpages], axis=1))
        ksum = sum(jnp.sum(kp.reshape(page_rows // sub, sub, HEAD_DIM), axis=0) for kp in kpages)
        mean = sum(ksum[i * KV_HEADS:(i + 1) * KV_HEADS] for i in range(sub // KV_HEADS)) * (1.0 / MOBA_BLOCK)
        mean_rows = jnp.concatenate(
            [jnp.broadcast_to(mean[h:h + 1], (nkv, HEAD_DIM)) for h in range(KV_HEADS)], axis=0)
        gcols.append(jnp.sum(qf * mean_rows, axis=-1, keepdims=True))
    probs, maxes, sums = [], [], []
    for bi in blocks:
        s = jnp.where(same_head, scores[bi], NEG_INF)
        mb = jnp.max(s, axis=-1, keepdims=True)
        p = jnp.exp(s - mb)
        maxes.append(mb)
        sums.append(jnp.sum(p, axis=-1, keepdims=True))
        probs.append(p.astype(BF16))
    gate, m_all, l_all = gate_ref[...], m_ref[...], l_ref[...]
    for bi in blocks:
        n = j * blocks_per_step + bi
        acc = _dot(probs[bi][:, :page_rows], vbuf[slot, bi * pages_per_block].astype(BF16))
        for pp in range(1, pages_per_block):
            acc = acc + _dot(probs[bi][:, pp * page_rows:(pp + 1) * page_rows],
                             vbuf[slot, bi * pages_per_block + pp].astype(BF16))
        acc_ref[n] = acc
        hot = lane == n
        gate = jnp.where(hot, gcols[bi], gate)
        m_all = jnp.where(hot, maxes[bi], m_all)
        l_all = jnp.where(hot, sums[bi], l_all)
    gate_ref[...] = gate
    m_ref[...] = m_all
    l_ref[...] = l_all

    @pl.when(j == pl.num_programs(1) - 1)
    def _():
        lane = lax.broadcasted_iota(jnp.int32, (nkv, LANES), 1)
        qrow_t = lax.rem(lax.broadcasted_iota(jnp.int32, (nkv, LANES), 0), n_new)
        lane_f = lane.astype(F32)
        for h in range(KV_HEADS):
            hd = slice(h * HEAD_DIM, (h + 1) * HEAD_DIM)
            rs = slice(h * nkv, (h + 1) * nkv)
            qsc = qf[rs] * scale
            gg = jnp.where(lane < n_blocks, gate_ref[rs, :], -jnp.inf)
            picked = jnp.zeros((nkv, LANES), F32)
            for _ in range(min(MOBA_TOPK, n_blocks)):
                mx = jnp.max(gg, axis=-1, keepdims=True)
                first = jnp.min(jnp.where(gg == mx, lane_f, float(LANES)), axis=-1, keepdims=True)
                hit = lane_f == first
                picked = jnp.where(hit, 1.0, picked)
                gg = jnp.where(hit, -jnp.inf, gg)
            sel = jnp.where(lane < n_blocks, picked, 0.0) > 0.0
            knew = knew_ref[0, :, hd]
            vnew = vnew_ref[0, :, hd]
            s_own = jnp.full((nkv, LANES), NEG_INF, F32)
            for jj in range(n_new):
                col = jnp.sum(qsc * knew[jj:jj + 1, :], axis=-1, keepdims=True)
                s_own = jnp.where(jnp.logical_and(lane == jj, qrow_t >= jj), col, s_own)
            mh = m_ref[rs, :]
            m_tot = jnp.maximum(jnp.max(s_own, axis=-1, keepdims=True),
                                jnp.max(jnp.where(sel, mh, NEG_INF), axis=-1, keepdims=True))
            w = jnp.where(sel, jnp.exp(mh - m_tot), 0.0)
            p_own = jnp.exp(s_own - m_tot)
            den = jnp.sum(w * l_ref[rs, :], axis=-1, keepdims=True) + jnp.sum(p_own, axis=-1, keepdims=True)
            num = jnp.zeros((nkv, HEAD_DIM), F32)
            for jj in range(n_new):
                num = num + p_own[:, jj:jj + 1] * vnew[jj:jj + 1, :]
            for n in range(n_blocks):
                num = num + w[:, n:n + 1] * acc_ref[n, rs, :]
            out = num / den
            for g in range(GROUP):
                o_ref[0, :, (h * GROUP + g) * HEAD_DIM:(h * GROUP + g + 1) * HEAD_DIM] = (
                    out[g * n_new:(g + 1) * n_new, :])


def _moba_sample(q, knew, vnew, cache_k, cache_v, page_table):
    db, ds, _ = q.shape
    n_pages = page_table.shape[1]
    pages_per_block = MOBA_BLOCK // PAGE_SIZE
    assert n_pages % pages_per_block == 0, "past length must be a whole number of MoBA blocks"
    n_blocks = n_pages // pages_per_block
    assert 0 < n_blocks <= LANES
    assert KV_HEADS & (KV_HEADS - 1) == 0 and 8 % KV_HEADS == 0
    pps = math.gcd(SAMPLE_PAGES_PER_STEP, n_pages)
    assert pps % pages_per_block == 0
    nrow = Q_HEADS * ds
    kdim = KV_HEADS * HEAD_DIM
    assert db * (n_pages // pps) >= SAMPLE_PAGE_SLOTS - 1
    page_buf = pltpu.VMEM((SAMPLE_PAGE_SLOTS, pps, PAGE_SIZE * KV_HEADS, HEAD_DIM), F32)

    per_seq = lambda w: pl.BlockSpec((1, ds, w), lambda b, j, pt: (b, 0, 0))
    hbm = pl.BlockSpec(memory_space=pl.ANY)
    grid_spec = pltpu.PrefetchScalarGridSpec(
        num_scalar_prefetch=1,
        grid=(db, n_pages // pps),
        in_specs=[per_seq(Q_HEADS * HEAD_DIM), per_seq(kdim), per_seq(kdim), hbm, hbm],
        out_specs=per_seq(Q_HEADS * HEAD_DIM),
        scratch_shapes=[
            page_buf, page_buf, pltpu.SemaphoreType.DMA((2, SAMPLE_PAGE_SLOTS)),
            pltpu.VMEM((nrow, LANES), F32),
            pltpu.VMEM((nrow, LANES), F32),
            pltpu.VMEM((nrow, LANES), F32),
            pltpu.VMEM((n_blocks, nrow, HEAD_DIM), F32),
        ],
    )
    return pl.pallas_call(
        functools.partial(_moba_sample_kernel, pages_per_step=pps, n_blocks=n_blocks, n_new=ds),
        grid_spec=grid_spec,
        out_shape=jax.ShapeDtypeStruct(q.shape, F32),
        compiler_params=pltpu.CompilerParams(
            dimension_semantics=("arbitrary", "arbitrary"), vmem_limit_bytes=VMEM_LIMIT_BYTES),
        name="moba_sample",
    )(page_table.reshape(-1), q, knew, vnew, cache_k, cache_v)


def _rope_tables(pos):
    inv = ROPE_THETA ** (-jnp.arange(0, HEAD_DIM, 2, dtype=F32) / HEAD_DIM)
    ang = pos.astype(F32)[:, None] * inv[None, :]
    cos, sin = jnp.cos(ang), jnp.sin(ang)
    return jnp.concatenate([cos, cos], axis=-1), jnp.concatenate([-sin, sin], axis=-1)


def kernel(x_prompt, x_sample, state_gla, cache_k, cache_v, page_table, w_in_a, w_g2, b_g, g_gla_out, w_o_a,
           norm_kv, w_kv, g_k, w_q_b, g_q, w_o_b, norm_mix, norm_mlp, w_up, w_down):
    bsz, t_len, _ = x_prompt.shape
    db, ds, _ = x_sample.shape
    n_a = state_gla.shape[0]
    depth = norm_mix.shape[0]
    assert n_a == 1 and depth == 2, "one GLA layer followed by one MoBA layer"
    assert t_len % PROMPT_ROW_TILE == 0 and PROMPT_ROW_TILE % PROMPT_GLA_CHUNK == 0
    assert t_len % MOBA_BLOCK == 0
    past = page_table.shape[1] * PAGE_SIZE
    kdim = KV_HEADS * HEAD_DIM
    row2 = lambda a: a.reshape(1, -1)

    main = 2 * GLA_KEY_DIM + 2 * GLA_VALUE_DIM
    wmain = w_in_a[0, :, :main].astype(BF16)
    wglr = jnp.pad(w_in_a[0, :, main:], ((0, 0), (0, LANES - GATE_RANK))).astype(BF16)
    wg2 = jnp.pad(w_g2[0], ((0, LANES - GATE_RANK), (0, 0))).astype(BF16)
    gla_w = (row2(norm_mix[0]), wmain, wglr, wg2, row2(b_g[0]), row2(g_gla_out[0]), w_o_a[0].astype(BF16))
    s0_p = jnp.zeros((bsz,) + state_gla.shape[2:], F32)
    seq_tile = math.gcd(SAMPLE_SEQ_TILE, db)
    hp, st_p = _gla_layer(x_prompt, s0_p, *gla_w, seqs=1, rows_per_seq=PROMPT_ROW_TILE,
                          chunk=PROMPT_GLA_CHUNK)
    hs, st_s = _gla_layer(x_sample, state_gla[0], *gla_w, seqs=seq_tile, rows_per_seq=ds, chunk=ds)
    hp = hp.reshape(bsz * t_len, D_MODEL)
    hs = hs.reshape(db * ds, D_MODEL)
    wup = w_up.astype(BF16)
    wdn = w_down.astype(BF16)
    hp = _mlp(hp, row2(norm_mlp[0]), wup[0], wdn[0])
    hs = _mlp(hs, row2(norm_mlp[0]), wup[0], wdn[0])

    cos_p, sin_p = _rope_tables(jnp.arange(t_len, dtype=jnp.int32))
    cos_s, sin_s = _rope_tables(past + jnp.arange(ds, dtype=jnp.int32))
    cos_s, sin_s = jnp.tile(cos_s, (db, 1)), jnp.tile(sin_s, (db, 1))
    kvq_w = (row2(norm_kv), row2(norm_mix[1]), w_kv.astype(BF16), w_q_b[0].astype(BF16), row2(g_k), row2(g_q[0]))
    k_p, q_p, k4_p, v4_p, vt_p = _kvq(hp, *kvq_w, cos_p, sin_p, tm=PROMPT_ROW_TILE, seq_len=t_len)
    k_s, q_s, k4_s, v4_s, v_s = _kvq(hs, *kvq_w, cos_s, sin_s, tm=db * ds)

    a_p = _moba_prompt(q_p.reshape(bsz, t_len, -1), k_p.reshape(bsz, t_len, kdim), vt_p)
    pages = lambda c: c.reshape(c.shape[0], PAGE_SIZE * KV_HEADS, HEAD_DIM)
    a_s = _moba_sample(q_s.reshape(db, ds, -1), k_s.reshape(db, ds, kdim), v_s.reshape(db, ds, kdim),
                       pages(cache_k), pages(cache_v), page_table)
    wob = w_o_b[0].astype(BF16)
    hp = _mlp(hp, row2(norm_mlp[1]), wup[1], wdn[1], attn=a_p.reshape(bsz * t_len, -1), wo=wob)
    hs = _mlp(hs, row2(norm_mlp[1]), wup[1], wdn[1], attn=a_s.reshape(db * ds, -1), wo=wob)

    kv4 = lambda a, n, t: a.reshape(n, t, KV_HEADS, HEAD_DIM)
    return (hp.reshape(bsz, t_len, D_MODEL), hs.reshape(db, ds, D_MODEL), st_p[None], st_s[None],
            kv4(k4_p, bsz, t_len), kv4(v4_p, bsz, t_len), kv4(k4_s, db, ds), kv4(v4_s, db, ds))
```

```python
import functools
import math

import jax
import jax.numpy as jnp
from jax import lax
from jax.experimental import pallas as pl
from jax.experimental.pallas import tpu as pltpu

F32 = jnp.float32
BF16 = jnp.bfloat16

D_MODEL = 1024
D_FF = 4 * D_MODEL
NORM_EPS = 1e-6
GLA_HEADS = 4
GLA_DK = 128
GLA_DV = 256
GLA_KEY_DIM = GLA_HEADS * GLA_DK
GLA_VALUE_DIM = GLA_HEADS * GLA_DV
GATE_RANK = 16
GATE_NORMALIZER = 16.0
HEAD_DIM = 128
Q_HEADS = 8
KV_HEADS = 4
GROUP = Q_HEADS // KV_HEADS
MOBA_BLOCK = 256
MOBA_TOPK = 3
PAGE_SIZE = 128
ROPE_THETA = 10000.0
NEG_INF = -1e30
LANES = 128

VMEM_LIMIT_BYTES = 56 * 1024 * 1024

PROMPT_ROW_TILE = 256
GLA_MAX_FACTORED_SPAN = 40.0
PROMPT_GLA_CHUNK = 256
MLP_ROW_TILE = 512
MLP_FF_TILE = 1024
MOBA_KV_HEADS_PER_STEP = 2
SAMPLE_SEQ_TILE = 4
SAMPLE_PAGES_PER_STEP = 16
SAMPLE_PAGE_SLOTS = 3


def _rms(x):
    return x * lax.rsqrt(jnp.mean(x * x, axis=-1, keepdims=True) + NORM_EPS)


def _log_sigmoid(x):
    return jnp.minimum(x, 0.0) - jnp.log1p(jnp.exp(-jnp.abs(x)))


def _dot(a, b):
    return jnp.dot(a, b, preferred_element_type=F32)


def _dot_nt(a, b, precision=None):
    return lax.dot_general(a, b, (((1,), (1,)), ((), ())), precision=precision,
                           preferred_element_type=F32)


def _dot_tn(a, b):
    return lax.dot_general(a, b, (((0,), (0,)), ((), ())), preferred_element_type=F32)


def _const_spec(shape):
    n = len(shape)
    return pl.BlockSpec(shape, lambda *_: (0,) * n)


def _gla_layer_kernel(x_ref, s0_ref, gmix_ref, wmain_ref, wglr_ref, wg2_ref, bg_ref, gout_ref, wo_ref,
                      h_ref, sout_ref, proj_ref, gk_ref, b_ref, oi_ref, o_ref, s_ref, *,
                      seqs, rows_per_seq, chunk):
    t = pl.program_id(1)
    rows = seqs * rows_per_seq
    n_chunks = rows_per_seq // chunk

    x = x_ref[...].reshape(rows, D_MODEL)
    xn = (_rms(x) * gmix_ref[...]).astype(BF16)
    proj_ref[...] = _dot(xn, wmain_ref[...])
    glr = _dot(xn, wglr_ref[...])
    gpre = _dot(glr.astype(BF16), wg2_ref[...]) + bg_ref[...]
    gk_ref[...] = _log_sigmoid(gpre) * (1.0 / GATE_NORMALIZER)

    @pl.when(t == 0)
    def _():
        s_ref[...] = s0_ref[...]

    r_i = lax.broadcasted_iota(jnp.int32, (chunk, chunk), 0)
    c_i = lax.broadcasted_iota(jnp.int32, (chunk, chunk), 1)
    causal = c_i <= r_i
    mm = BF16 if chunk % 16 == 0 else F32
    tri = jnp.where(causal, 1.0, 0.0).astype(mm)
    n_sub = 2 if chunk % 32 == 0 else 1
    sub = chunk // n_sub
    half = max(sub // 2, 1)
    same_sub = causal if n_sub == 1 else jnp.logical_and(causal, (r_i >= sub) == (c_i >= sub))
    gout = gout_ref[...]
    heads = range(GLA_HEADS)
    kd = [slice(h * GLA_DK, (h + 1) * GLA_DK) for h in heads]
    vd = [slice(h * GLA_DV, (h + 1) * GLA_DV) for h in heads]
    k_off, v_off, r_off = GLA_KEY_DIM, 2 * GLA_KEY_DIM, 2 * GLA_KEY_DIM + GLA_VALUE_DIM

    def chunk_body(idx, carry):
        seq = idx // n_chunks if (seqs > 1 and n_chunks > 1) else (idx if seqs > 1 else 0)
        base = pl.multiple_of(idx * chunk, chunk)
        rs = pl.ds(base, chunk)
        g = gk_ref[rs, :]
        g_hi = g.astype(BF16).astype(F32)
        bcum = _dot(tri, g_hi.astype(mm)) + _dot(tri, (g - g_hi).astype(mm))
        b = [bcum[:, kd[h]] for h in heads]
        b_last = [b[h][chunk - 1:chunk] for h in heads]
        q = [proj_ref[rs, kd[h]] * (GLA_DK ** -0.5) for h in heads]
        k = [proj_ref[rs, k_off + h * GLA_DK:k_off + (h + 1) * GLA_DK] for h in heads]
        v = [proj_ref[rs, v_off + h * GLA_DV:v_off + (h + 1) * GLA_DV].astype(mm) for h in heads]
        s_old = [s_ref[seq, h] for h in heads]

        o_inter, kv = [], []
        for h in heads:
            q_e = q[h] * jnp.exp(b[h])
            o_inter.append(_dot(q_e.astype(BF16), s_old[h].astype(BF16)))
        for h in heads:
            k_e = k[h] * jnp.exp(b_last[h] - b[h])
            kv.append(_dot_tn(k_e.astype(mm), v[h]))
        for h in heads:
            decay_col = jnp.transpose(jnp.broadcast_to(jnp.exp(b_last[h]), (GLA_DK, GLA_DK)))
            s_ref[seq, h] = s_old[h] * jnp.concatenate([decay_col, decay_col], axis=1) + kv[h]

        q_d, k_d, a = [], [], []
        for h in heads:
            mids = jnp.concatenate(
                [jnp.broadcast_to(b[h][i * sub + half - 1:i * sub + half], (sub, GLA_DK)) for i in range(n_sub)],
                axis=0)
            q_d.append((q[h] * jnp.exp(b[h] - mids)).astype(mm))
            k_d.append((k[h] * jnp.exp(mids - b[h])).astype(mm))
        for h in heads:
            a.append(jnp.where(same_sub, _dot_nt(q_d[h], k_d[h]), 0.0).astype(mm))
        intra = [_dot(a[h], v[h]) for h in heads]
        if n_sub == 2:
            a_x = []
            for h in heads:
                edge = b[h][sub - 1:sub]
                q_x = q[h][sub:] * jnp.exp(b[h][sub:] - edge)
                k_x = k[h][:sub] * jnp.exp(edge - b[h][:sub])
                a_x.append(_dot_nt(q_x.astype(mm), k_x.astype(mm)).astype(mm))
            for h in heads:
                cross = _dot(a_x[h], v[h][:sub])
                intra[h] = jnp.concatenate([intra[h][:sub], intra[h][sub:] + cross], axis=0)

        def gated_norm(o, h):
            r = proj_ref[rs, r_off + h * GLA_DV:r_off + (h + 1) * GLA_DV]
            return _rms(o) * gout * (r * jax.nn.sigmoid(r))

        b_ref[rs, :] = bcum
        for h in heads:
            oi_ref[rs, vd[h]] = o_inter[h]
            o_ref[rs, vd[h]] = gated_norm(o_inter[h] + intra[h], h)

        span = jnp.zeros((1, GLA_DK), F32)
        for h in heads:
            prev = jnp.zeros((1, GLA_DK), F32)
            for i in range(n_sub):
                end = b[h][(i + 1) * sub - 1:(i + 1) * sub]
                span = jnp.maximum(span, prev - end)
                prev = end

        @pl.when(jnp.max(span) > GLA_MAX_FACTORED_SPAN)
        def _():
            col = lax.broadcasted_iota(jnp.int32, (chunk, GLA_DK), 0)

            def row_group(gi, c):
                grp = pl.ds(pl.multiple_of(base + gi * 8, 8), 8)
                for h in heads:
                    b_g = b_ref[grp, kd[h]]
                    q_g = proj_ref[grp, kd[h]] * (GLA_DK ** -0.5)
                    b_h = b_ref[rs, kd[h]]
                    k_h = proj_ref[rs, k_off + h * GLA_DK:k_off + (h + 1) * GLA_DK]
                    v_h = proj_ref[rs, v_off + h * GLA_DV:v_off + (h + 1) * GLA_DV]
                    out_rows = []
                    for r in range(8):
                        live = col <= gi * 8 + r
                        decay = jnp.where(live, jnp.exp(jnp.minimum(b_g[r:r + 1] - b_h, 0.0)), 0.0)
                        a_col = jnp.sum(decay * k_h * q_g[r:r + 1], axis=-1, keepdims=True)
                        out_rows.append(jnp.sum(a_col * v_h, axis=0, keepdims=True))
                    oi_ref[grp, vd[h]] += jnp.concatenate(out_rows, axis=0)
                return c

            lax.fori_loop(0, chunk // 8, row_group, 0)
            for h in heads:
                o_ref[rs, vd[h]] = gated_norm(oi_ref[rs, vd[h]], h)

        return carry

    lax.fori_loop(0, seqs * n_chunks, chunk_body, 0)

    y = _dot(o_ref[...].astype(BF16), wo_ref[...])
    h_ref[...] = (x + y).reshape(seqs, rows_per_seq, D_MODEL)

    @pl.when(t == pl.num_programs(1) - 1)
    def _():
        sout_ref[...] = s_ref[...]


def _gla_layer(x, s0, gmix, wmain, wglr, wg2, bg, gout, wo, *, seqs, rows_per_seq, chunk):
    nseq, t_len, _ = x.shape
    grid = (nseq // seqs, t_len // rows_per_seq)
    rows = seqs * rows_per_seq
    state_block = (seqs, GLA_HEADS, GLA_DK, GLA_DV)
    kern = functools.partial(_gla_layer_kernel, seqs=seqs, rows_per_seq=rows_per_seq, chunk=chunk)
    return pl.pallas_call(
        kern,
        grid=grid,
        in_specs=[
            pl.BlockSpec((seqs, rows_per_seq, D_MODEL), lambda b, t: (b, t, 0)),
            pl.BlockSpec(state_block, lambda b, t: (b, 0, 0, 0)),
            _const_spec(gmix.shape), _const_spec(wmain.shape), _const_spec(wglr.shape),
            _const_spec(wg2.shape), _const_spec(bg.shape), _const_spec(gout.shape), _const_spec(wo.shape),
        ],
        out_specs=[
            pl.BlockSpec((seqs, rows_per_seq, D_MODEL), lambda b, t: (b, t, 0)),
            pl.BlockSpec(state_block, lambda b, t: (b, 0, 0, 0)),
        ],
        out_shape=[jax.ShapeDtypeStruct(x.shape, F32), jax.ShapeDtypeStruct(s0.shape, F32)],
        scratch_shapes=[
            pltpu.VMEM((rows, 2 * GLA_KEY_DIM + 2 * GLA_VALUE_DIM), F32),
            pltpu.VMEM((rows, GLA_KEY_DIM), F32),
            pltpu.VMEM((rows, GLA_KEY_DIM), F32),
            pltpu.VMEM((rows, GLA_VALUE_DIM), F32),
            pltpu.VMEM((rows, GLA_VALUE_DIM), F32),
            pltpu.VMEM(state_block, F32),
        ],
        compiler_params=pltpu.CompilerParams(
            dimension_semantics=("arbitrary", "arbitrary"), vmem_limit_bytes=VMEM_LIMIT_BYTES),
        name="gla_layer",
    )(x, s0, gmix, wmain, wglr, wg2, bg, gout, wo)


def _gla_pipe_kernel(x_ref, s0_ref, gmix_ref, wmain_ref, wglr_ref, wg2_ref, bg_ref, gout_ref, wo_ref,
                     h_ref, sout_ref, proj_ref, gk_ref, xk_ref, b_ref, oi_ref, o_ref, s_ref, *,
                     seqs, chunk, tiles_per_seq):
    i = pl.program_id(0)
    rows = seqs * chunk
    slot = lax.rem(i, 2)
    prev = 1 - slot
    tile_b = jnp.maximum(i - 1, 0)
    t_in_seq = lax.rem(tile_b, tiles_per_seq)
    kq = 2 * GLA_KEY_DIM
    k_off, v_off, r_off = GLA_KEY_DIM, 2 * GLA_KEY_DIM, 2 * GLA_KEY_DIM + GLA_VALUE_DIM

    @pl.when(i == 0)
    def _():
        proj_ref[1] = jnp.zeros(proj_ref.shape[1:], F32)
        gk_ref[1] = jnp.zeros(gk_ref.shape[1:], F32)
        xk_ref[1] = jnp.zeros(xk_ref.shape[1:], F32)

    @pl.when(t_in_seq == 0)
    def _():
        s_ref[...] = s0_ref[...]

    r_i = lax.broadcasted_iota(jnp.int32, (chunk, chunk), 0)
    c_i = lax.broadcasted_iota(jnp.int32, (chunk, chunk), 1)
    causal = c_i <= r_i
    mm = BF16 if chunk % 16 == 0 else F32
    tri = jnp.where(causal, 1.0, 0.0).astype(mm)
    n_sub = 2 if chunk % 32 == 0 else 1
    sub = chunk // n_sub
    half = max(sub // 2, 1)
    same_sub = causal if n_sub == 1 else jnp.logical_and(causal, (r_i >= sub) == (c_i >= sub))
    gout = gout_ref[...]
    units = [(sq, h) for sq in range(seqs) for h in range(GLA_HEADS)]
    rs = [slice(sq * chunk, (sq + 1) * chunk) for sq, _ in units]
    kd = [slice(h * GLA_DK, (h + 1) * GLA_DK) for _, h in units]
    vd = [slice(h * GLA_DV, (h + 1) * GLA_DV) for _, h in units]
    n_units = range(len(units))

    bcum = []
    for sq in range(seqs):
        g = gk_ref[prev, sq * chunk:(sq + 1) * chunk, :]
        g_hi = g.astype(BF16).astype(F32)
        bcum.append(_dot(tri, g_hi.astype(mm)) + _dot(tri, (g - g_hi).astype(mm)))

    b = [bcum[sq][:, kd[u]] for u, (sq, _) in enumerate(units)]
    b_last = [b[u][chunk - 1:chunk] for u in n_units]
    q = [proj_ref[prev, rs[u], kd[u]] * (GLA_DK ** -0.5) for u in n_units]
    k = [proj_ref[prev, rs[u], k_off + h * GLA_DK:k_off + (h + 1) * GLA_DK] for u, (_, h) in enumerate(units)]
    v = [proj_ref[prev, rs[u], v_off + h * GLA_DV:v_off + (h + 1) * GLA_DV].astype(mm)
         for u, (_, h) in enumerate(units)]
    s_old = [s_ref[sq, h] for sq, h in units]
    q_e = [(q[u] * jnp.exp(b[u])).astype(BF16) for u in n_units]
    k_e = [(k[u] * jnp.exp(b_last[u] - b[u])).astype(mm) for u in n_units]
    q_d, k_d = [], []
    for u in n_units:
        mids = jnp.concatenate(
            [jnp.broadcast_to(b[u][j * sub + half - 1:j * sub + half], (sub, GLA_DK)) for j in range(n_sub)],
            axis=0)
        q_d.append((q[u] * jnp.exp(b[u] - mids)).astype(mm))
        k_d.append((k[u] * jnp.exp(mids - b[u])).astype(mm))

    x = x_ref[...].reshape(rows, D_MODEL)
    xn = (_rms(x) * gmix_ref[...]).astype(BF16)
    proj_ref[slot, :, :kq] = _dot(xn, wmain_ref[:, :kq])
    glr = _dot(xn, wglr_ref[...])
    gpre = _dot(glr.astype(BF16), wg2_ref[...]) + bg_ref[...]
    gk_ref[slot] = _log_sigmoid(gpre) * (1.0 / GATE_NORMALIZER)

    o_inter = [_dot(q_e[u], s_old[u].astype(BF16)) for u in n_units]
    kv = [_dot_tn(k_e[u], v[u]) for u in n_units]
    for u, (sq, h) in enumerate(units):
        decay_col = jnp.transpose(jnp.broadcast_to(jnp.exp(b_last[u]), (GLA_DK, GLA_DK)))
        s_ref[sq, h] = s_old[u] * jnp.concatenate([decay_col, decay_col], axis=1) + kv[u]
    proj_ref[slot, :, kq:r_off] = _dot(xn, wmain_ref[:, kq:r_off])

    a = []
    for u in n_units:
        a.append(jnp.where(same_sub, _dot_nt(q_d[u], k_d[u]), 0.0).astype(mm))
    intra = [_dot(a[u], v[u]) for u in n_units]
    if n_sub == 2:
        a_x = []
        for u in n_units:
            edge = b[u][sub - 1:sub]
            q_x = q[u][sub:] * jnp.exp(b[u][sub:] - edge)
            k_x = k[u][:sub] * jnp.exp(edge - b[u][:sub])
            a_x.append(_dot_nt(q_x.astype(mm), k_x.astype(mm)).astype(mm))
        for u in n_units:
            cross = _dot(a_x[u], v[u][:sub])
            intra[u] = jnp.concatenate([intra[u][:sub], intra[u][sub:] + cross], axis=0)

    proj_ref[slot, :, r_off:] = _dot(xn, wmain_ref[:, r_off:])
    xk_ref[slot] = x

    def gated_norm(o, u):
        h = units[u][1]
        r = proj_ref[prev, rs[u], r_off + h * GLA_DV:r_off + (h + 1) * GLA_DV]
        return _rms(o) * gout * (r * jax.nn.sigmoid(r))

    for sq in range(seqs):
        b_ref[sq * chunk:(sq + 1) * chunk, :] = bcum[sq]
    for u in n_units:
        oi_ref[rs[u], vd[u]] = o_inter[u]
        o_ref[rs[u], vd[u]] = gated_norm(o_inter[u] + intra[u], u)

    def finish():
        y = _dot(o_ref[...].astype(BF16), wo_ref[...])
        h_ref[...] = (xk_ref[prev] + y).reshape(h_ref.shape)

    finish()

    span = jnp.zeros((1, GLA_DK), F32)
    for u in n_units:
        before = jnp.zeros((1, GLA_DK), F32)
        for j in range(n_sub):
            end = b[u][(j + 1) * sub - 1:(j + 1) * sub]
            span = jnp.maximum(span, before - end)
            before = end

    @pl.when(jnp.max(span) > GLA_MAX_FACTORED_SPAN)
    def _():
        col = lax.broadcasted_iota(jnp.int32, (chunk, GLA_DK), 0)

        def row_group(gi, c):
            for u, (sq, h) in enumerate(units):
                grp = pl.ds(pl.multiple_of(sq * chunk + gi * 8, 8), 8)
                b_g = b_ref[grp, kd[u]]
                q_g = proj_ref[prev, grp, kd[u]] * (GLA_DK ** -0.5)
                b_u = b_ref[rs[u], kd[u]]
                k_u = proj_ref[prev, rs[u], k_off + h * GLA_DK:k_off + (h + 1) * GLA_DK]
                v_u = proj_ref[prev, rs[u], v_off + h * GLA_DV:v_off + (h + 1) * GLA_DV]
                out_rows = []
                for r in range(8):
                    live = col <= gi * 8 + r
                    decay = jnp.where(live, jnp.exp(jnp.minimum(b_g[r:r + 1] - b_u, 0.0)), 0.0)
                    a_col = jnp.sum(decay * k_u * q_g[r:r + 1], axis=-1, keepdims=True)
                    out_rows.append(jnp.sum(a_col * v_u, axis=0, keepdims=True))
                oi_ref[grp, vd[u]] += jnp.concatenate(out_rows, axis=0)
            return c

        lax.fori_loop(0, chunk // 8, row_group, 0)
        for u in n_units:
            o_ref[rs[u], vd[u]] = gated_norm(oi_ref[rs[u], vd[u]], u)
        finish()

    @pl.when(t_in_seq == tiles_per_seq - 1)
    def _():
        sout_ref[...] = s_ref[...]


def _gla_pipe(x, s0, gmix, wmain, wglr, wg2, bg, gout, wo, *, seqs, chunk):
    nseq, t_len, _ = x.shape
    assert nseq % seqs == 0 and t_len % chunk == 0 and chunk % 8 == 0
    assert seqs == 1 or t_len == chunk, "several sequences per tile only when a tile spans the sequence"
    tiles_per_seq = t_len // chunk
    n_tiles = (nseq // seqs) * tiles_per_seq
    rows = seqs * chunk
    state_block = (seqs, GLA_HEADS, GLA_DK, GLA_DV)
    proj_tile = lambda i: jnp.minimum(i, n_tiles - 1)
    rec_tile = lambda i: jnp.maximum(i - 1, 0)
    x_spec = lambda tile: pl.BlockSpec(
        (seqs, chunk, D_MODEL), lambda i: (tile(i) // tiles_per_seq, tile(i) % tiles_per_seq, 0))
    s_spec = pl.BlockSpec(state_block, lambda i: (rec_tile(i) // tiles_per_seq, 0, 0, 0))
    kern = functools.partial(_gla_pipe_kernel, seqs=seqs, chunk=chunk, tiles_per_seq=tiles_per_seq)
    return pl.pallas_call(
        kern,
        grid=(n_tiles + 1,),
        in_specs=[
            x_spec(proj_tile), s_spec,
            _const_spec(gmix.shape), _const_spec(wmain.shape), _const_spec(wglr.shape),
            _const_spec(wg2.shape), _const_spec(bg.shape), _const_spec(gout.shape), _const_spec(wo.shape),
        ],
        out_specs=[x_spec(rec_tile), s_spec],
        out_shape=[jax.ShapeDtypeStruct(x.shape, F32), jax.ShapeDtypeStruct(s0.shape, F32)],
        scratch_shapes=[
            pltpu.VMEM((2, rows, 2 * GLA_KEY_DIM + 2 * GLA_VALUE_DIM), F32),
            pltpu.VMEM((2, rows, GLA_KEY_DIM), F32),
            pltpu.VMEM((2, rows, D_MODEL), F32),
            pltpu.VMEM((rows, GLA_KEY_DIM), F32),
            pltpu.VMEM((rows, GLA_VALUE_DIM), F32),
            pltpu.VMEM((rows, GLA_VALUE_DIM), F32),
            pltpu.VMEM(state_block, F32),
        ],
        compiler_params=pltpu.CompilerParams(
            dimension_semantics=("arbitrary",), vmem_limit_bytes=VMEM_LIMIT_BYTES),
        name="gla_layer",
    )(x, s0, gmix, wmain, wglr, wg2, bg, gout, wo)


def _mlp_kernel(*refs, has_attn):
    if has_attn:
        h_ref, a_ref, wo_ref, g_ref, wup_ref, wdn_ref, out_ref, act_ref = refs
        h = h_ref[...] + _dot(a_ref[...].astype(BF16), wo_ref[...])
    else:
        h_ref, g_ref, wup_ref, wdn_ref, out_ref, act_ref = refs
        h = h_ref[...]
    xn = (_rms(h) * g_ref[...]).astype(BF16)
    for j in range(D_FF // MLP_FF_TILE):
        ff = slice(j * MLP_FF_TILE, (j + 1) * MLP_FF_TILE)
        a = jnp.maximum(_dot(xn, wup_ref[:, ff]), 0.0)
        act_ref[:, ff] = (a * a).astype(BF16)
    out_ref[...] = h + _dot(act_ref[...], wdn_ref[...])


def _mlp(h, g, wup, wdn, attn=None, wo=None):
    n = h.shape[0]
    tm = min(MLP_ROW_TILE, n)
    row_spec = pl.BlockSpec((tm, D_MODEL), lambda i: (i, 0))
    single = pl.Buffered(1)

    def wspec(w):
        return pl.BlockSpec(w.shape, lambda i: (0, 0), pipeline_mode=single)

    if attn is None:
        args = (h, g, wup, wdn)
        in_specs = [row_spec, _const_spec(g.shape), wspec(wup), wspec(wdn)]
    else:
        args = (h, attn, wo, g, wup, wdn)
        in_specs = [row_spec, row_spec, wspec(wo), _const_spec(g.shape), wspec(wup), wspec(wdn)]
    return pl.pallas_call(
        functools.partial(_mlp_kernel, has_attn=attn is not None),
        grid=(n // tm,),
        in_specs=in_specs,
        out_specs=row_spec,
        out_shape=jax.ShapeDtypeStruct(h.shape, F32),
        scratch_shapes=[pltpu.VMEM((tm, D_FF), BF16)],
        compiler_params=pltpu.CompilerParams(
            dimension_semantics=("arbitrary",), vmem_limit_bytes=VMEM_LIMIT_BYTES),
        name="mlp",
    )(*args)


def _rope(x, cos, sin_signed):
    return x * cos + pltpu.roll(x, HEAD_DIM // 2, 1) * sin_signed


def _kvq_kernel(h_ref, nkv_ref, nq_ref, wkv_ref, wq_ref, gk_ref, gq_ref, cos_ref, sin_ref,
                k_ref, q_ref, k4_ref, v4_ref, v_ref, *, transposed_v):
    tm = h_ref.shape[0]
    xh = _rms(h_ref[...])
    kv = _dot((xh * nkv_ref[...]).astype(BF16), wkv_ref[...])
    qq = _dot((xh * nq_ref[...]).astype(BF16), wq_ref[...])
    cos = cos_ref[...]
    sin = sin_ref[...]
    kdim = KV_HEADS * HEAD_DIM
    for hh in range(KV_HEADS):
        hd = slice(hh * HEAD_DIM, (hh + 1) * HEAD_DIM)
        kh = _rope(_rms(kv[:, hd]) * gk_ref[...], cos, sin)
        vh = kv[:, kdim + hh * HEAD_DIM:kdim + (hh + 1) * HEAD_DIM]
        k_ref[:, hd] = kh
        k4_ref[pl.ds(hh, tm, stride=KV_HEADS), :] = kh
        v4_ref[pl.ds(hh, tm, stride=KV_HEADS), :] = vh
        if transposed_v:
            v_ref[0, hh] = jnp.transpose(vh).astype(BF16)
    if not transposed_v:
        v_ref[...] = kv[:, kdim:]
    for hh in range(Q_HEADS):
        hd = slice(hh * HEAD_DIM, (hh + 1) * HEAD_DIM)
        q_ref[:, hd] = _rope(_rms(qq[:, hd]) * gq_ref[...], cos, sin)


def _kvq(h, nkv, nq, wkv, wq, gk, gq, cos, sin, *, tm, seq_len=None):
    n = h.shape[0]
    pos_tiles = cos.shape[0] // tm
    kdim = KV_HEADS * HEAD_DIM
    row = lambda w: pl.BlockSpec((tm, w), lambda i: (i, 0))
    row4 = pl.BlockSpec((tm * KV_HEADS, HEAD_DIM), lambda i: (i, 0))
    tab = pl.BlockSpec((tm, HEAD_DIM), lambda i: (i % pos_tiles, 0))
    out_specs = [row(kdim), row(Q_HEADS * HEAD_DIM), row4, row4]
    out_shape = [jax.ShapeDtypeStruct((n, kdim), F32), jax.ShapeDtypeStruct((n, Q_HEADS * HEAD_DIM), F32),
                 jax.ShapeDtypeStruct((n * KV_HEADS, HEAD_DIM), F32),
                 jax.ShapeDtypeStruct((n * KV_HEADS, HEAD_DIM), F32)]
    if seq_len is not None:
        assert seq_len % tm == 0 and tm % LANES == 0
        tiles = seq_len // tm
        out_specs.append(pl.BlockSpec((1, KV_HEADS, HEAD_DIM, tm), lambda i: (i // tiles, 0, 0, i % tiles)))
        out_shape.append(jax.ShapeDtypeStruct((n // seq_len, KV_HEADS, HEAD_DIM, seq_len), BF16))
    else:
        out_specs.append(row(kdim))
        out_shape.append(jax.ShapeDtypeStruct((n, kdim), F32))
    return pl.pallas_call(
        functools.partial(_kvq_kernel, transposed_v=seq_len is not None),
        grid=(n // tm,),
        in_specs=[row(D_MODEL), _const_spec(nkv.shape), _const_spec(nq.shape), _const_spec(wkv.shape),
                  _const_spec(wq.shape), _const_spec(gk.shape), _const_spec(gq.shape), tab, tab],
        out_specs=out_specs,
        out_shape=out_shape,
        compiler_params=pltpu.CompilerParams(
            dimension_semantics=("arbitrary",), vmem_limit_bytes=VMEM_LIMIT_BYTES),
        name="kvq",
    )(h, nkv, nq, wkv, wq, gk, gq, cos, sin)


def _select_topk_rows(gate, cand, ksel):
    n = gate.shape[0]
    row = lax.broadcasted_iota(jnp.int32, gate.shape, 0).astype(F32)
    gg = jnp.where(cand, gate, -jnp.inf)
    sel = jnp.zeros(gate.shape, F32)
    for _ in range(ksel):
        mx = jnp.max(gg, axis=0, keepdims=True)
        first = jnp.min(jnp.where(gg == mx, row, float(n)), axis=0, keepdims=True)
        hit = row == first
        sel = jnp.where(hit, 1.0, sel)
        gg = jnp.where(hit, -jnp.inf, gg)
    return jnp.where(cand, sel, 0.0)


def _split_bf16(x):
    hi = x.astype(BF16)
    return hi, (x - hi.astype(F32)).astype(BF16)


def _moba_prompt_kernel(q_ref, k_ref, vt_ref, o_ref, means_ref, s_ref, p_ref, *, nblk):
    qi = pl.program_id(2)
    sub = 8
    kv_heads = k_ref.shape[2] // HEAD_DIM
    streams = [(kv, kv * GROUP + g) for kv in range(kv_heads) for g in range(GROUP)]

    @pl.when(qi == 0)
    def _():
        means_ref[...] = jnp.zeros(means_ref.shape, F32)
        for kv in range(kv_heads):
            for n in range(nblk):
                blk = k_ref[0, n * MOBA_BLOCK:(n + 1) * MOBA_BLOCK, kv * HEAD_DIM:(kv + 1) * HEAD_DIM]
                means_ref[kv, n:n + 1, :] = jnp.sum(blk, axis=0, keepdims=True) * (1.0 / MOBA_BLOCK)

    def attend(nb):
        qb = q_ref[0]
        heads = [qb[:, hq * HEAD_DIM:(hq + 1) * HEAD_DIM] for _, hq in streams]
        qs = [(h * (HEAD_DIM ** -0.5 * math.log2(math.e))).astype(BF16) for h in heads]
        causal = (lax.broadcasted_iota(jnp.int32, (MOBA_BLOCK, MOBA_BLOCK), 0)
                  <= lax.broadcasted_iota(jnp.int32, (MOBA_BLOCK, MOBA_BLOCK), 1))
        ksel = min(MOBA_TOPK, nblk)
        kbs = [[k_ref[0, n * MOBA_BLOCK:(n + 1) * MOBA_BLOCK, kv * HEAD_DIM:(kv + 1) * HEAD_DIM].astype(BF16)
                for n in range(nb + 1)] for kv in range(kv_heads)]

        sel = [None] * len(streams)
        if nb > ksel:
            row = lax.broadcasted_iota(jnp.int32, (means_ref.shape[1], MOBA_BLOCK), 0)
            for st, (kv, _) in enumerate(streams):
                m_hi, m_lo = _split_bf16(means_ref[kv])
                q_hi, q_lo = _split_bf16(heads[st])
                gate = _dot_nt(m_hi, q_hi) + _dot_nt(m_hi, q_lo) + _dot_nt(m_lo, q_hi)
                sel[st] = _select_topk_rows(gate, row < nb, ksel)
        m_parts = []
        for st, (kv, _) in enumerate(streams):
            m_part = None
            for n in range(nb + 1):
                s = _dot_nt(kbs[kv][n], qs[st])
                if n == nb:
                    s = jnp.where(causal, s, NEG_INF)
                elif sel[st] is not None:
                    s = jnp.where(sel[st][n:n + 1, :] > 0.0, s, NEG_INF)
                s_ref[st, n] = s
                part = jnp.max(s.reshape(MOBA_BLOCK // sub, sub, MOBA_BLOCK), axis=0)
                m_part = part if m_part is None else jnp.maximum(m_part, part)
            m_parts.append(m_part)
        keys = (nb + 1) * MOBA_BLOCK
        for st, (kv, hq) in enumerate(streams):
            m = jnp.max(m_parts[st], axis=0, keepdims=True)
            l_part = jnp.zeros((sub, MOBA_BLOCK), F32)
            for n in range(nb + 1):
                p = jnp.exp2(s_ref[st, n] - m)
                l_part = l_part + jnp.sum(p.reshape(MOBA_BLOCK // sub, sub, MOBA_BLOCK), axis=0)
                p_ref[st, n * MOBA_BLOCK:(n + 1) * MOBA_BLOCK, :] = p.astype(BF16)
            acc = _dot(vt_ref[0, kv, :, :keys], p_ref[st, :keys, :])
            out = acc * (1.0 / jnp.sum(l_part, axis=0, keepdims=True))
            o_ref[0, :, hq * HEAD_DIM:(hq + 1) * HEAD_DIM] = jnp.transpose(out)

    for nb in range(nblk):
        pl.when(qi == nb)(functools.partial(attend, nb))


def _moba_prompt(q, k, vt):
    bsz, t_len, _ = q.shape
    nblk = t_len // MOBA_BLOCK
    nblk_pad = -(-nblk // 8) * 8
    kvs = math.gcd(MOBA_KV_HEADS_PER_STEP, KV_HEADS)
    n_streams = kvs * GROUP
    return pl.pallas_call(
        functools.partial(_moba_prompt_kernel, nblk=nblk),
        grid=(bsz, KV_HEADS // kvs, nblk),
        in_specs=[
            pl.BlockSpec((1, MOBA_BLOCK, n_streams * HEAD_DIM), lambda b, h, i: (b, i, h)),
            pl.BlockSpec((1, t_len, kvs * HEAD_DIM), lambda b, h, i: (b, 0, h)),
            pl.BlockSpec((1, kvs, HEAD_DIM, t_len), lambda b, h, i: (b, h, 0, 0)),
        ],
        out_specs=pl.BlockSpec((1, MOBA_BLOCK, n_streams * HEAD_DIM), lambda b, h, i: (b, i, h)),
        out_shape=jax.ShapeDtypeStruct(q.shape, F32),
        scratch_shapes=[
            pltpu.VMEM((kvs, nblk_pad, HEAD_DIM), F32),
            pltpu.VMEM((n_streams, nblk, MOBA_BLOCK, MOBA_BLOCK), F32),
            pltpu.VMEM((n_streams, t_len, MOBA_BLOCK), BF16),
        ],
        compiler_params=pltpu.CompilerParams(
            dimension_semantics=("arbitrary", "arbitrary", "arbitrary"), vmem_limit_bytes=VMEM_LIMIT_BYTES),
        name="moba_prompt",
    )(q, k, vt)


def _moba_sample_kernel(pt_ref, q_ref, knew_ref, vnew_ref, ck_ref, cv_ref, o_ref,
                        kbuf, vbuf, sem, gate_ref, m_ref, l_ref, acc_ref, *, pages_per_step, n_blocks, n_new):
    j = pl.program_id(1)
    steps_per_seq = pl.num_programs(1)
    step = pl.program_id(0) * steps_per_seq + j
    n_steps = pl.num_programs(0) * steps_per_seq
    ahead = SAMPLE_PAGE_SLOTS - 1

    def page_copies(s, slot):
        out = []
        for i in range(pages_per_step):
            page = pt_ref[s * pages_per_step + i]
            out.append(pltpu.make_async_copy(ck_ref.at[page], kbuf.at[slot, i], sem.at[0, slot]))
            out.append(pltpu.make_async_copy(cv_ref.at[page], vbuf.at[slot, i], sem.at[1, slot]))
        return out

    @pl.when(step == 0)
    def _():
        for s in range(ahead):
            for c in page_copies(s, s):
                c.start()

    @pl.when(step + ahead < n_steps)
    def _():
        for c in page_copies(step + ahead, lax.rem(step + ahead, SAMPLE_PAGE_SLOTS)):
            c.start()

    slot = lax.rem(step, SAMPLE_PAGE_SLOTS)
    for c in page_copies(step, slot):
        c.wait()

    pages_per_block = MOBA_BLOCK // PAGE_SIZE
    blocks_per_step = pages_per_step // pages_per_block
    page_rows = PAGE_SIZE * KV_HEADS
    nkv = GROUP * n_new
    nrow = KV_HEADS * nkv
    lane = lax.broadcasted_iota(jnp.int32, (nrow, LANES), 1)
    scale = HEAD_DIM ** -0.5

    @pl.when(j == 0)
    def _():
        gate_ref[...] = jnp.zeros(gate_ref.shape, F32)
        m_ref[...] = jnp.zeros(m_ref.shape, F32)
        l_ref[...] = jnp.zeros(l_ref.shape, F32)

    qf = jnp.concatenate([q_ref[0, :, hq * HEAD_DIM:(hq + 1) * HEAD_DIM] for hq in range(Q_HEADS)], axis=0)
    qs = (qf * scale).astype(BF16)
    blk_cols = pages_per_block * page_rows
    col_head = lax.broadcasted_iota(jnp.int32, (nrow, blk_cols), 1) & (KV_HEADS - 1)
    row_head = lax.div(lax.broadcasted_iota(jnp.int32, (nrow, blk_cols), 0), nkv)
    same_head = col_head == row_head
    sub = 8

    blocks = range(blocks_per_step)
    scores, gcols = [], []
    for bi in blocks:
        kpages = [kbuf[slot, bi * pages_per_block + pp] for pp in range(pages_per_block)]
        scores.append(jnp.concatenate([_dot_nt(qs, kp.astype(BF16)) for kp in kpages], axis=1))
        ksum = sum(jnp.sum(kp.reshape(page_rows // sub, sub, HEAD_DIM), axis=0) for kp in kpages)
        mean = sum(ksum[i * KV_HEADS:(i + 1) * KV_HEADS] for i in range(sub // KV_HEADS)) * (1.0 / MOBA_BLOCK)
        mean_rows = jnp.concatenate(
            [jnp.broadcast_to(mean[h:h + 1], (nkv, HEAD_DIM)) for h in range(KV_HEADS)], axis=0)
        gcols.append(jnp.sum(qf * mean_rows, axis=-1, keepdims=True))
    probs, maxes, sums = [], [], []
    for bi in blocks:
        s = jnp.where(same_head, scores[bi], NEG_INF)
        mb = jnp.max(s, axis=-1, keepdims=True)
        p = jnp.exp(s - mb)
        maxes.append(mb)
        sums.append(jnp.sum(p, axis=-1, keepdims=True))
        probs.append(p.astype(BF16))
    gate, m_all, l_all = gate_ref[...], m_ref[...], l_ref[...]
    for bi in blocks:
        n = j * blocks_per_step + bi
        acc = _dot(probs[bi][:, :page_rows], vbuf[slot, bi * pages_per_block].astype(BF16))
        for pp in range(1, pages_per_block):
            acc = acc + _dot(probs[bi][:, pp * page_rows:(pp + 1) * page_rows],
                             vbuf[slot, bi * pages_per_block + pp].astype(BF16))
        acc_ref[n] = acc
        hot = lane == n
        gate = jnp.where(hot, gcols[bi], gate)
        m_all = jnp.where(hot, maxes[bi], m_all)
        l_all = jnp.where(hot, sums[bi], l_all)
    gate_ref[...] = gate
    m_ref[...] = m_all
    l_ref[...] = l_all

    @pl.when(j == pl.num_programs(1) - 1)
    def _():
        lane = lax.broadcasted_iota(jnp.int32, (nkv, LANES), 1)
        qrow_t = lax.rem(lax.broadcasted_iota(jnp.int32, (nkv, LANES), 0), n_new)
        lane_f = lane.astype(F32)
        for h in range(KV_HEADS):
            hd = slice(h * HEAD_DIM, (h + 1) * HEAD_DIM)
            rs = slice(h * nkv, (h + 1) * nkv)
            qsc = qf[rs] * scale
            gg = jnp.where(lane < n_blocks, gate_ref[rs, :], -jnp.inf)
            picked = jnp.zeros((nkv, LANES), F32)
            for _ in range(min(MOBA_TOPK, n_blocks)):
                mx = jnp.max(gg, axis=-1, keepdims=True)
                first = jnp.min(jnp.where(gg == mx, lane_f, float(LANES)), axis=-1, keepdims=True)
                hit = lane_f == first
                picked = jnp.where(hit, 1.0, picked)
                gg = jnp.where(hit, -jnp.inf, gg)
            sel = jnp.where(lane < n_blocks, picked, 0.0) > 0.0
            knew = knew_ref[0, :, hd]
            vnew = vnew_ref[0, :, hd]
            s_own = jnp.full((nkv, LANES), NEG_INF, F32)
            for jj in range(n_new):
                col = jnp.sum(qsc * knew[jj:jj + 1, :], axis=-1, keepdims=True)
                s_own = jnp.where(jnp.logical_and(lane == jj, qrow_t >= jj), col, s_own)
            mh = m_ref[rs, :]
            m_tot = jnp.maximum(jnp.max(s_own, axis=-1, keepdims=True),
                                jnp.max(jnp.where(sel, mh, NEG_INF), axis=-1, keepdims=True))
            w = jnp.where(sel, jnp.exp(mh - m_tot), 0.0)
            p_own = jnp.exp(s_own - m_tot)
            den = jnp.sum(w * l_ref[rs, :], axis=-1, keepdims=True) + jnp.sum(p_own, axis=-1, keepdims=True)
            num = jnp.zeros((nkv, HEAD_DIM), F32)
            for jj in range(n_new):
                num = num + p_own[:, jj:jj + 1] * vnew[jj:jj + 1, :]
            for n in range(n_blocks):
                num = num + w[:, n:n + 1] * acc_ref[n, rs, :]
            out = num / den
            for g in range(GROUP):
                o_ref[0, :, (h * GROUP + g) * HEAD_DIM:(h * GROUP + g + 1) * HEAD_DIM] = (
                    out[g * n_new:(g + 1) * n_new, :])


def _moba_sample(q, knew, vnew, cache_k, cache_v, page_table):
    db, ds, _ = q.shape
    n_pages = page_table.shape[1]
    pages_per_block = MOBA_BLOCK // PAGE_SIZE
    assert n_pages % pages_per_block == 0, "past length must be a whole number of MoBA blocks"
    n_blocks = n_pages // pages_per_block
    assert 0 < n_blocks <= LANES
    assert KV_HEADS & (KV_HEADS - 1) == 0 and 8 % KV_HEADS == 0
    pps = math.gcd(SAMPLE_PAGES_PER_STEP, n_pages)
    assert pps % pages_per_block == 0
    nrow = Q_HEADS * ds
    kdim = KV_HEADS * HEAD_DIM
    assert db * (n_pages // pps) >= SAMPLE_PAGE_SLOTS - 1
    page_buf = pltpu.VMEM((SAMPLE_PAGE_SLOTS, pps, PAGE_SIZE * KV_HEADS, HEAD_DIM), F32)

    per_seq = lambda w: pl.BlockSpec((1, ds, w), lambda b, j, pt: (b, 0, 0))
    hbm = pl.BlockSpec(memory_space=pl.ANY)
    grid_spec = pltpu.PrefetchScalarGridSpec(
        num_scalar_prefetch=1,
        grid=(db, n_pages // pps),
        in_specs=[per_seq(Q_HEADS * HEAD_DIM), per_seq(kdim), per_seq(kdim), hbm, hbm],
        out_specs=per_seq(Q_HEADS * HEAD_DIM),
        scratch_shapes=[
            page_buf, page_buf, pltpu.SemaphoreType.DMA((2, SAMPLE_PAGE_SLOTS)),
            pltpu.VMEM((nrow, LANES), F32),
            pltpu.VMEM((nrow, LANES), F32),
            pltpu.VMEM((nrow, LANES), F32),
            pltpu.VMEM((n_blocks, nrow, HEAD_DIM), F32),
        ],
    )
    return pl.pallas_call(
        functools.partial(_moba_sample_kernel, pages_per_step=pps, n_blocks=n_blocks, n_new=ds),
        grid_spec=grid_spec,
        out_shape=jax.ShapeDtypeStruct(q.shape, F32),
        compiler_params=pltpu.CompilerParams(
            dimension_semantics=("arbitrary", "arbitrary"), vmem_limit_bytes=VMEM_LIMIT_BYTES),
        name="moba_sample",
    )(page_table.reshape(-1), q, knew, vnew, cache_k, cache_v)


def _rope_tables(pos):
    inv = ROPE_THETA ** (-jnp.arange(0, HEAD_DIM, 2, dtype=F32) / HEAD_DIM)
    ang = pos.astype(F32)[:, None] * inv[None, :]
    cos, sin = jnp.cos(ang), jnp.sin(ang)
    return jnp.concatenate([cos, cos], axis=-1), jnp.concatenate([-sin, sin], axis=-1)


def kernel(x_prompt, x_sample, state_gla, cache_k, cache_v, page_table, w_in_a, w_g2, b_g, g_gla_out, w_o_a,
           norm_kv, w_kv, g_k, w_q_b, g_q, w_o_b, norm_mix, norm_mlp, w_up, w_down):
    bsz, t_len, _ = x_prompt.shape
    db, ds, _ = x_sample.shape
    n_a = state_gla.shape[0]
    depth = norm_mix.shape[0]
    assert n_a == 1 and depth == 2, "one GLA layer followed by one MoBA layer"
    assert t_len % PROMPT_ROW_TILE == 0 and PROMPT_ROW_TILE % PROMPT_GLA_CHUNK == 0
    assert t_len % MOBA_BLOCK == 0
    past = page_table.shape[1] * PAGE_SIZE
    kdim = KV_HEADS * HEAD_DIM
    row2 = lambda a: a.reshape(1, -1)

    main = 2 * GLA_KEY_DIM + 2 * GLA_VALUE_DIM
    wmain = w_in_a[0, :, :main].astype(BF16)
    wglr = jnp.pad(w_in_a[0, :, main:], ((0, 0), (0, LANES - GATE_RANK))).astype(BF16)
    wg2 = jnp.pad(w_g2[0], ((0, LANES - GATE_RANK), (0, 0))).astype(BF16)
    gla_w = (row2(norm_mix[0]), wmain, wglr, wg2, row2(b_g[0]), row2(g_gla_out[0]), w_o_a[0].astype(BF16))
    s0_p = jnp.zeros((bsz,) + state_gla.shape[2:], F32)
    seq_tile = math.gcd(SAMPLE_SEQ_TILE, db)
    hp, st_p = _gla_pipe(x_prompt, s0_p, *gla_w, seqs=1, chunk=PROMPT_GLA_CHUNK)
    hs, st_s = _gla_pipe(x_sample, state_gla[0], *gla_w, seqs=seq_tile, chunk=ds)
    hp = hp.reshape(bsz * t_len, D_MODEL)
    hs = hs.reshape(db * ds, D_MODEL)
    wup = w_up.astype(BF16)
    wdn = w_down.astype(BF16)
    hp = _mlp(hp, row2(norm_mlp[0]), wup[0], wdn[0])
    hs = _mlp(hs, row2(norm_mlp[0]), wup[0], wdn[0])

    cos_p, sin_p = _rope_tables(jnp.arange(t_len, dtype=jnp.int32))
    cos_s, sin_s = _rope_tables(past + jnp.arange(ds, dtype=jnp.int32))
    cos_s, sin_s = jnp.tile(cos_s, (db, 1)), jnp.tile(sin_s, (db, 1))
    kvq_w = (row2(norm_kv), row2(norm_mix[1]), w_kv.astype(BF16), w_q_b[0].astype(BF16), row2(g_k), row2(g_q[0]))
    k_p, q_p, k4_p, v4_p, vt_p = _kvq(hp, *kvq_w, cos_p, sin_p, tm=PROMPT_ROW_TILE, seq_len=t_len)
    k_s, q_s, k4_s, v4_s, v_s = _kvq(hs, *kvq_w, cos_s, sin_s, tm=db * ds)

    a_p = _moba_prompt(q_p.reshape(bsz, t_len, -1), k_p.reshape(bsz, t_len, kdim), vt_p)
    pages = lambda c: c.reshape(c.shape[0], PAGE_SIZE * KV_HEADS, HEAD_DIM)
    a_s = _moba_sample(q_s.reshape(db, ds, -1), k_s.reshape(db, ds, kdim), v_s.reshape(db, ds, kdim),
                       pages(cache_k), pages(cache_v), page_table)
    wob = w_o_b[0].astype(BF16)
    hp = _mlp(hp, row2(norm_mlp[1]), wup[1], wdn[1], attn=a_p.reshape(bsz * t_len, -1), wo=wob)
    hs = _mlp(hs, row2(norm_mlp[1]), wup[1], wdn[1], attn=a_s.reshape(db * ds, -1), wo=wob)

    kv4 = lambda a, n, t: a.reshape(n, t, KV_HEADS, HEAD_DIM)
    return (hp.reshape(bsz, t_len, D_MODEL), hs.reshape(db, ds, D_MODEL), st_p[None], st_s[None],
            kv4(k4_p, bsz, t_len), kv4(v4_p, bsz, t_len), kv4(k4_s, db, ds), kv4(v4_s, db, ds))
```

```python
import functools
import math

import jax
import jax.numpy as jnp
from jax import lax
from jax.experimental import pallas as pl
from jax.experimental.pallas import tpu as pltpu

F32 = jnp.float32
BF16 = jnp.bfloat16

D_MODEL = 1024
D_FF = 4 * D_MODEL
NORM_EPS = 1e-6
GLA_HEADS = 4
GLA_DK = 128
GLA_DV = 256
GLA_KEY_DIM = GLA_HEADS * GLA_DK
GLA_VALUE_DIM = GLA_HEADS * GLA_DV
GATE_RANK = 16
GATE_NORMALIZER = 16.0
HEAD_DIM = 128
Q_HEADS = 8
KV_HEADS = 4
GROUP = Q_HEADS // KV_HEADS
MOBA_BLOCK = 256
MOBA_TOPK = 3
PAGE_SIZE = 128
ROPE_THETA = 10000.0
NEG_INF = -1e30
LANES = 128
SUBLANES = 8

VMEM_LIMIT_BYTES = 56 * 1024 * 1024

PROMPT_ROW_TILE = 256
GLA_MAX_FACTORED_SPAN = 40.0
PROMPT_GLA_CHUNK = 256
MLP_ROW_TILE = 512
MLP_FF_TILE = 1024
MOBA_KV_HEADS_PER_STEP = 2
SAMPLE_SEQ_TILE = 4
SAMPLE_PAGES_PER_STEP = 16
SAMPLE_PAGE_SLOTS = 3


def _rms(x):
    return x * lax.rsqrt(jnp.mean(x * x, axis=-1, keepdims=True) + NORM_EPS)


def _log_sigmoid(x):
    return jnp.minimum(x, 0.0) - jnp.log1p(jnp.exp(-jnp.abs(x)))


def _dot(a, b):
    return jnp.dot(a, b, preferred_element_type=F32)


def _dot_nt(a, b, precision=None):
    return lax.dot_general(a, b, (((1,), (1,)), ((), ())), precision=precision,
                           preferred_element_type=F32)


def _dot_tn(a, b):
    return lax.dot_general(a, b, (((0,), (0,)), ((), ())), preferred_element_type=F32)


def _const_spec(shape):
    n = len(shape)
    return pl.BlockSpec(shape, lambda *_: (0,) * n)


def _gla_pipe_kernel(x_ref, s0_ref, gmix_ref, wmain_ref, wglr_ref, wg2_ref, bg_ref, gout_ref, wo_ref,
                     h_ref, sout_ref, proj_ref, gk_ref, xk_ref, b_ref, oi_ref, o_ref, s_ref, *,
                     seqs, chunk, tiles_per_seq):
    i = pl.program_id(0)
    rows = seqs * chunk
    slot = lax.rem(i, 2)
    prev = 1 - slot
    tile_b = jnp.maximum(i - 1, 0)
    t_in_seq = lax.rem(tile_b, tiles_per_seq)
    kq = 2 * GLA_KEY_DIM
    k_off, v_off, r_off = GLA_KEY_DIM, 2 * GLA_KEY_DIM, 2 * GLA_KEY_DIM + GLA_VALUE_DIM

    @pl.when(i == 0)
    def _():
        proj_ref[1] = jnp.zeros(proj_ref.shape[1:], F32)
        gk_ref[1] = jnp.zeros(gk_ref.shape[1:], F32)
        xk_ref[1] = jnp.zeros(xk_ref.shape[1:], F32)

    @pl.when(t_in_seq == 0)
    def _():
        s_ref[...] = s0_ref[...]

    r_i = lax.broadcasted_iota(jnp.int32, (chunk, chunk), 0)
    c_i = lax.broadcasted_iota(jnp.int32, (chunk, chunk), 1)
    causal = c_i <= r_i
    mm = BF16 if chunk % 16 == 0 else F32
    tri = jnp.where(causal, 1.0, 0.0).astype(mm)
    n_sub = 2 if chunk % 32 == 0 else 1
    sub = chunk // n_sub
    half = max(sub // 2, 1)
    same_sub = causal if n_sub == 1 else jnp.logical_and(causal, (r_i >= sub) == (c_i >= sub))
    gout = gout_ref[...]
    units = [(sq, h) for sq in range(seqs) for h in range(GLA_HEADS)]
    rs = [slice(sq * chunk, (sq + 1) * chunk) for sq, _ in units]
    kd = [slice(h * GLA_DK, (h + 1) * GLA_DK) for _, h in units]
    vd = [slice(h * GLA_DV, (h + 1) * GLA_DV) for _, h in units]
    n_units = range(len(units))

    bcum = []
    for sq in range(seqs):
        g = gk_ref[prev, sq * chunk:(sq + 1) * chunk, :]
        g_hi = g.astype(BF16).astype(F32)
        bcum.append(_dot(tri, g_hi.astype(mm)) + _dot(tri, (g - g_hi).astype(mm)))

    b = [bcum[sq][:, kd[u]] for u, (sq, _) in enumerate(units)]
    b_last = [b[u][chunk - 1:chunk] for u in n_units]
    q = [proj_ref[prev, rs[u], kd[u]] * (GLA_DK ** -0.5) for u in n_units]
    k = [proj_ref[prev, rs[u], k_off + h * GLA_DK:k_off + (h + 1) * GLA_DK] for u, (_, h) in enumerate(units)]
    v = [proj_ref[prev, rs[u], v_off + h * GLA_DV:v_off + (h + 1) * GLA_DV].astype(mm)
         for u, (_, h) in enumerate(units)]
    s_old = [s_ref[sq, h] for sq, h in units]
    q_e = [(q[u] * jnp.exp(b[u])).astype(BF16) for u in n_units]
    k_e = [(k[u] * jnp.exp(b_last[u] - b[u])).astype(mm) for u in n_units]
    q_d, k_d = [], []
    for u in n_units:
        mids = jnp.concatenate(
            [jnp.broadcast_to(b[u][j * sub + half - 1:j * sub + half], (sub, GLA_DK)) for j in range(n_sub)],
            axis=0)
        q_d.append((q[u] * jnp.exp(b[u] - mids)).astype(mm))
        k_d.append((k[u] * jnp.exp(mids - b[u])).astype(mm))

    x = x_ref[...].reshape(rows, D_MODEL)
    xn = (_rms(x) * gmix_ref[...]).astype(BF16)
    proj_ref[slot, :, :kq] = _dot(xn, wmain_ref[:, :kq])
    glr = _dot(xn, wglr_ref[...])
    gpre = _dot(glr.astype(BF16), wg2_ref[...]) + bg_ref[...]
    gk_ref[slot] = _log_sigmoid(gpre) * (1.0 / GATE_NORMALIZER)

    o_inter = [_dot(q_e[u], s_old[u].astype(BF16)) for u in n_units]
    kv = [_dot_tn(k_e[u], v[u]) for u in n_units]
    for u, (sq, h) in enumerate(units):
        decay_col = jnp.transpose(jnp.broadcast_to(jnp.exp(b_last[u]), (GLA_DK, GLA_DK)))
        s_ref[sq, h] = s_old[u] * jnp.concatenate([decay_col, decay_col], axis=1) + kv[u]
    proj_ref[slot, :, kq:r_off] = _dot(xn, wmain_ref[:, kq:r_off])

    a = []
    for u in n_units:
        a.append(jnp.where(same_sub, _dot_nt(q_d[u], k_d[u]), 0.0).astype(mm))
    intra = [_dot(a[u], v[u]) for u in n_units]
    if n_sub == 2:
        a_x = []
        for u in n_units:
            edge = b[u][sub - 1:sub]
            q_x = q[u][sub:] * jnp.exp(b[u][sub:] - edge)
            k_x = k[u][:sub] * jnp.exp(edge - b[u][:sub])
            a_x.append(_dot_nt(q_x.astype(mm), k_x.astype(mm)).astype(mm))
        for u in n_units:
            cross = _dot(a_x[u], v[u][:sub])
            intra[u] = jnp.concatenate([intra[u][:sub], intra[u][sub:] + cross], axis=0)

    proj_ref[slot, :, r_off:] = _dot(xn, wmain_ref[:, r_off:])
    xk_ref[slot] = x

    def gated_norm(o, u):
        h = units[u][1]
        r = proj_ref[prev, rs[u], r_off + h * GLA_DV:r_off + (h + 1) * GLA_DV]
        return _rms(o) * gout * (r * jax.nn.sigmoid(r))

    for sq in range(seqs):
        b_ref[sq * chunk:(sq + 1) * chunk, :] = bcum[sq]
    for u in n_units:
        oi_ref[rs[u], vd[u]] = o_inter[u]
        o_ref[rs[u], vd[u]] = gated_norm(o_inter[u] + intra[u], u)

    def finish():
        y = _dot(o_ref[...].astype(BF16), wo_ref[...])
        h_ref[...] = (xk_ref[prev] + y).reshape(h_ref.shape)

    finish()

    span = jnp.zeros((1, GLA_DK), F32)
    for u in n_units:
        before = jnp.zeros((1, GLA_DK), F32)
        for j in range(n_sub):
            end = b[u][(j + 1) * sub - 1:(j + 1) * sub]
            span = jnp.maximum(span, before - end)
            before = end

    @pl.when(jnp.max(span) > GLA_MAX_FACTORED_SPAN)
    def _():
        col = lax.broadcasted_iota(jnp.int32, (chunk, GLA_DK), 0)

        def row_group(gi, c):
            for u, (sq, h) in enumerate(units):
                grp = pl.ds(pl.multiple_of(sq * chunk + gi * SUBLANES, SUBLANES), SUBLANES)
                b_g = b_ref[grp, kd[u]]
                q_g = proj_ref[prev, grp, kd[u]] * (GLA_DK ** -0.5)
                b_u = b_ref[rs[u], kd[u]]
                k_u = proj_ref[prev, rs[u], k_off + h * GLA_DK:k_off + (h + 1) * GLA_DK]
                v_u = proj_ref[prev, rs[u], v_off + h * GLA_DV:v_off + (h + 1) * GLA_DV]
                out_rows = []
                for r in range(SUBLANES):
                    live = col <= gi * SUBLANES + r
                    decay = jnp.where(live, jnp.exp(jnp.minimum(b_g[r:r + 1] - b_u, 0.0)), 0.0)
                    a_col = jnp.sum(decay * k_u * q_g[r:r + 1], axis=-1, keepdims=True)
                    out_rows.append(jnp.sum(a_col * v_u, axis=0, keepdims=True))
                oi_ref[grp, vd[u]] += jnp.concatenate(out_rows, axis=0)
            return c

        lax.fori_loop(0, chunk // SUBLANES, row_group, 0)
        for u in n_units:
            o_ref[rs[u], vd[u]] = gated_norm(oi_ref[rs[u], vd[u]], u)
        finish()

    @pl.when(t_in_seq == tiles_per_seq - 1)
    def _():
        sout_ref[...] = s_ref[...]


def _gla_pipe(x, s0, gmix, wmain, wglr, wg2, bg, gout, wo, *, seqs, chunk):
    nseq, t_len, _ = x.shape
    assert nseq % seqs == 0 and t_len % chunk == 0 and chunk % 8 == 0
    assert seqs == 1 or t_len == chunk, "several sequences per tile only when a tile spans the sequence"
    tiles_per_seq = t_len // chunk
    n_tiles = (nseq // seqs) * tiles_per_seq
    rows = seqs * chunk
    state_block = (seqs, GLA_HEADS, GLA_DK, GLA_DV)
    proj_tile = lambda i: jnp.minimum(i, n_tiles - 1)
    rec_tile = lambda i: jnp.maximum(i - 1, 0)
    x_spec = lambda tile: pl.BlockSpec(
        (seqs, chunk, D_MODEL), lambda i: (tile(i) // tiles_per_seq, tile(i) % tiles_per_seq, 0))
    s_spec = pl.BlockSpec(state_block, lambda i: (rec_tile(i) // tiles_per_seq, 0, 0, 0))
    kern = functools.partial(_gla_pipe_kernel, seqs=seqs, chunk=chunk, tiles_per_seq=tiles_per_seq)
    return pl.pallas_call(
        kern,
        grid=(n_tiles + 1,),
        in_specs=[
            x_spec(proj_tile), s_spec,
            _const_spec(gmix.shape), _const_spec(wmain.shape), _const_spec(wglr.shape),
            _const_spec(wg2.shape), _const_spec(bg.shape), _const_spec(gout.shape), _const_spec(wo.shape),
        ],
        out_specs=[x_spec(rec_tile), s_spec],
        out_shape=[jax.ShapeDtypeStruct(x.shape, F32), jax.ShapeDtypeStruct(s0.shape, F32)],
        scratch_shapes=[
            pltpu.VMEM((2, rows, 2 * GLA_KEY_DIM + 2 * GLA_VALUE_DIM), F32),
            pltpu.VMEM((2, rows, GLA_KEY_DIM), F32),
            pltpu.VMEM((2, rows, D_MODEL), F32),
            pltpu.VMEM((rows, GLA_KEY_DIM), F32),
            pltpu.VMEM((rows, GLA_VALUE_DIM), F32),
            pltpu.VMEM((rows, GLA_VALUE_DIM), F32),
            pltpu.VMEM(state_block, F32),
        ],
        compiler_params=pltpu.CompilerParams(
            dimension_semantics=("arbitrary",), vmem_limit_bytes=VMEM_LIMIT_BYTES),
        name="gla_layer",
    )(x, s0, gmix, wmain, wglr, wg2, bg, gout, wo)


def _mlp_kernel(*refs, has_attn):
    if has_attn:
        h_ref, a_ref, wo_ref, g_ref, wup_ref, wdn_ref, out_ref, act_ref = refs
        h = h_ref[...] + _dot(a_ref[...].astype(BF16), wo_ref[...])
    else:
        h_ref, g_ref, wup_ref, wdn_ref, out_ref, act_ref = refs
        h = h_ref[...]
    xn = (_rms(h) * g_ref[...]).astype(BF16)
    for j in range(D_FF // MLP_FF_TILE):
        ff = slice(j * MLP_FF_TILE, (j + 1) * MLP_FF_TILE)
        a = jnp.maximum(_dot(xn, wup_ref[:, ff]), 0.0)
        act_ref[:, ff] = (a * a).astype(BF16)
    out_ref[...] = h + _dot(act_ref[...], wdn_ref[...])


def _mlp(h, g, wup, wdn, attn=None, wo=None):
    n = h.shape[0]
    tm = min(MLP_ROW_TILE, n)
    row_spec = pl.BlockSpec((tm, D_MODEL), lambda i: (i, 0))
    single = pl.Buffered(1)

    def wspec(w):
        return pl.BlockSpec(w.shape, lambda i: (0, 0), pipeline_mode=single)

    if attn is None:
        args = (h, g, wup, wdn)
        in_specs = [row_spec, _const_spec(g.shape), wspec(wup), wspec(wdn)]
    else:
        args = (h, attn, wo, g, wup, wdn)
        in_specs = [row_spec, row_spec, wspec(wo), _const_spec(g.shape), wspec(wup), wspec(wdn)]
    return pl.pallas_call(
        functools.partial(_mlp_kernel, has_attn=attn is not None),
        grid=(n // tm,),
        in_specs=in_specs,
        out_specs=row_spec,
        out_shape=jax.ShapeDtypeStruct(h.shape, F32),
        scratch_shapes=[pltpu.VMEM((tm, D_FF), BF16)],
        compiler_params=pltpu.CompilerParams(
            dimension_semantics=("arbitrary",), vmem_limit_bytes=VMEM_LIMIT_BYTES),
        name="mlp",
    )(*args)


def _rope(x, cos, sin_signed):
    return x * cos + pltpu.roll(x, HEAD_DIM // 2, 1) * sin_signed


def _kvq_kernel(h_ref, nkv_ref, nq_ref, wkv_ref, wq_ref, gk_ref, gq_ref, cos_ref, sin_ref,
                k_ref, q_ref, k4_ref, v4_ref, v_ref, *, transposed_v):
    tm = h_ref.shape[0]
    xh = _rms(h_ref[...])
    kv = _dot((xh * nkv_ref[...]).astype(BF16), wkv_ref[...])
    qq = _dot((xh * nq_ref[...]).astype(BF16), wq_ref[...])
    cos = cos_ref[...]
    sin = sin_ref[...]
    kdim = KV_HEADS * HEAD_DIM
    for hh in range(KV_HEADS):
        hd = slice(hh * HEAD_DIM, (hh + 1) * HEAD_DIM)
        kh = _rope(_rms(kv[:, hd]) * gk_ref[...], cos, sin)
        vh = kv[:, kdim + hh * HEAD_DIM:kdim + (hh + 1) * HEAD_DIM]
        k_ref[:, hd] = kh
        k4_ref[pl.ds(hh, tm, stride=KV_HEADS), :] = kh
        v4_ref[pl.ds(hh, tm, stride=KV_HEADS), :] = vh
        if transposed_v:
            v_ref[0, hh] = jnp.transpose(vh).astype(BF16)
    if not transposed_v:
        v_ref[...] = kv[:, kdim:]
    for hh in range(Q_HEADS):
        hd = slice(hh * HEAD_DIM, (hh + 1) * HEAD_DIM)
        q_ref[:, hd] = _rope(_rms(qq[:, hd]) * gq_ref[...], cos, sin)


def _kvq(h, nkv, nq, wkv, wq, gk, gq, cos, sin, *, tm, seq_len=None):
    n = h.shape[0]
    pos_tiles = cos.shape[0] // tm
    kdim = KV_HEADS * HEAD_DIM
    row = lambda w: pl.BlockSpec((tm, w), lambda i: (i, 0))
    row4 = pl.BlockSpec((tm * KV_HEADS, HEAD_DIM), lambda i: (i, 0))
    tab = pl.BlockSpec((tm, HEAD_DIM), lambda i: (i % pos_tiles, 0))
    out_specs = [row(kdim), row(Q_HEADS * HEAD_DIM), row4, row4]
    out_shape = [jax.ShapeDtypeStruct((n, kdim), F32), jax.ShapeDtypeStruct((n, Q_HEADS * HEAD_DIM), F32),
                 jax.ShapeDtypeStruct((n * KV_HEADS, HEAD_DIM), F32),
                 jax.ShapeDtypeStruct((n * KV_HEADS, HEAD_DIM), F32)]
    if seq_len is not None:
        assert seq_len % tm == 0 and tm % LANES == 0
        tiles = seq_len // tm
        out_specs.append(pl.BlockSpec((1, KV_HEADS, HEAD_DIM, tm), lambda i: (i // tiles, 0, 0, i % tiles)))
        out_shape.append(jax.ShapeDtypeStruct((n // seq_len, KV_HEADS, HEAD_DIM, seq_len), BF16))
    else:
        out_specs.append(row(kdim))
        out_shape.append(jax.ShapeDtypeStruct((n, kdim), F32))
    return pl.pallas_call(
        functools.partial(_kvq_kernel, transposed_v=seq_len is not None),
        grid=(n // tm,),
        in_specs=[row(D_MODEL), _const_spec(nkv.shape), _const_spec(nq.shape), _const_spec(wkv.shape),
                  _const_spec(wq.shape), _const_spec(gk.shape), _const_spec(gq.shape), tab, tab],
        out_specs=out_specs,
        out_shape=out_shape,
        compiler_params=pltpu.CompilerParams(
            dimension_semantics=("arbitrary",), vmem_limit_bytes=VMEM_LIMIT_BYTES),
        name="kvq",
    )(h, nkv, nq, wkv, wq, gk, gq, cos, sin)


def _select_topk_rows(gate, cand, ksel):
    n = gate.shape[0]
    row = lax.broadcasted_iota(jnp.int32, gate.shape, 0).astype(F32)
    gg = jnp.where(cand, gate, -jnp.inf)
    sel = jnp.zeros(gate.shape, F32)
    for _ in range(ksel):
        mx = jnp.max(gg, axis=0, keepdims=True)
        first = jnp.min(jnp.where(gg == mx, row, float(n)), axis=0, keepdims=True)
        hit = row == first
        sel = jnp.where(hit, 1.0, sel)
        gg = jnp.where(hit, -jnp.inf, gg)
    return jnp.where(cand, sel, 0.0)


def _split_bf16(x):
    hi = x.astype(BF16)
    return hi, (x - hi.astype(F32)).astype(BF16)


def _moba_prompt_kernel(q_ref, k_ref, vt_ref, o_ref, means_ref, s_ref, p_ref, *, nblk):
    qi = pl.program_id(2)
    sub = SUBLANES
    kv_heads = k_ref.shape[2] // HEAD_DIM
    streams = [(kv, kv * GROUP + g) for kv in range(kv_heads) for g in range(GROUP)]

    @pl.when(qi == 0)
    def _():
        means_ref[...] = jnp.zeros(means_ref.shape, F32)
        for kv in range(kv_heads):
            for n in range(nblk):
                blk = k_ref[0, n * MOBA_BLOCK:(n + 1) * MOBA_BLOCK, kv * HEAD_DIM:(kv + 1) * HEAD_DIM]
                means_ref[kv, n:n + 1, :] = jnp.sum(blk, axis=0, keepdims=True) * (1.0 / MOBA_BLOCK)

    def attend(nb):
        qb = q_ref[0]
        heads = [qb[:, hq * HEAD_DIM:(hq + 1) * HEAD_DIM] for _, hq in streams]
        qs = [(h * (HEAD_DIM ** -0.5 * math.log2(math.e))).astype(BF16) for h in heads]
        causal = (lax.broadcasted_iota(jnp.int32, (MOBA_BLOCK, MOBA_BLOCK), 0)
                  <= lax.broadcasted_iota(jnp.int32, (MOBA_BLOCK, MOBA_BLOCK), 1))
        ksel = min(MOBA_TOPK, nblk)
        keys = (nb + 1) * MOBA_BLOCK
        k_all = [k_ref[0, :keys, kv * HEAD_DIM:(kv + 1) * HEAD_DIM].astype(BF16) for kv in range(kv_heads)]

        sel = [None] * len(streams)
        if nb > ksel:
            row = lax.broadcasted_iota(jnp.int32, (means_ref.shape[1], MOBA_BLOCK), 0)
            for st, (kv, _) in enumerate(streams):
                m_hi, m_lo = _split_bf16(means_ref[kv])
                q_hi, q_lo = _split_bf16(heads[st])
                gate = _dot_nt(m_hi, q_hi) + _dot_nt(m_hi, q_lo) + _dot_nt(m_lo, q_hi)
                sel[st] = _select_topk_rows(gate, row < nb, ksel)
        m_parts = []
        for st, (kv, _) in enumerate(streams):
            m_part = None
            s_all = _dot_nt(k_all[kv], qs[st])
            for n in range(nb + 1):
                s = s_all[n * MOBA_BLOCK:(n + 1) * MOBA_BLOCK]
                if n == nb:
                    s = jnp.where(causal, s, NEG_INF)
                elif sel[st] is not None:
                    s = jnp.where(sel[st][n:n + 1, :] > 0.0, s, NEG_INF)
                s_ref[st, n] = s
                part = jnp.max(s.reshape(MOBA_BLOCK // sub, sub, MOBA_BLOCK), axis=0)
                m_part = part if m_part is None else jnp.maximum(m_part, part)
            m_parts.append(m_part)
        for st, (kv, hq) in enumerate(streams):
            m = jnp.max(m_parts[st], axis=0, keepdims=True)
            l_part = jnp.zeros((sub, MOBA_BLOCK), F32)
            for n in range(nb + 1):
                p = jnp.exp2(s_ref[st, n] - m)
                l_part = l_part + jnp.sum(p.reshape(MOBA_BLOCK // sub, sub, MOBA_BLOCK), axis=0)
                p_ref[st, n * MOBA_BLOCK:(n + 1) * MOBA_BLOCK, :] = p.astype(BF16)
            acc = _dot(vt_ref[0, kv, :, :keys], p_ref[st, :keys, :])
            out = acc * (1.0 / jnp.sum(l_part, axis=0, keepdims=True))
            o_ref[0, :, hq * HEAD_DIM:(hq + 1) * HEAD_DIM] = jnp.transpose(out)

    for nb in range(nblk):
        pl.when(qi == nb)(functools.partial(attend, nb))


def _moba_prompt(q, k, vt):
    bsz, t_len, _ = q.shape
    nblk = t_len // MOBA_BLOCK
    nblk_pad = -(-nblk // SUBLANES) * SUBLANES
    kvs = math.gcd(MOBA_KV_HEADS_PER_STEP, KV_HEADS)
    n_streams = kvs * GROUP
    return pl.pallas_call(
        functools.partial(_moba_prompt_kernel, nblk=nblk),
        grid=(bsz, KV_HEADS // kvs, nblk),
        in_specs=[
            pl.BlockSpec((1, MOBA_BLOCK, n_streams * HEAD_DIM), lambda b, h, i: (b, i, h)),
            pl.BlockSpec((1, t_len, kvs * HEAD_DIM), lambda b, h, i: (b, 0, h)),
            pl.BlockSpec((1, kvs, HEAD_DIM, t_len), lambda b, h, i: (b, h, 0, 0)),
        ],
        out_specs=pl.BlockSpec((1, MOBA_BLOCK, n_streams * HEAD_DIM), lambda b, h, i: (b, i, h)),
        out_shape=jax.ShapeDtypeStruct(q.shape, F32),
        scratch_shapes=[
            pltpu.VMEM((kvs, nblk_pad, HEAD_DIM), F32),
            pltpu.VMEM((n_streams, nblk, MOBA_BLOCK, MOBA_BLOCK), F32),
            pltpu.VMEM((n_streams, t_len, MOBA_BLOCK), BF16),
        ],
        compiler_params=pltpu.CompilerParams(
            dimension_semantics=("arbitrary", "arbitrary", "arbitrary"), vmem_limit_bytes=VMEM_LIMIT_BYTES),
        name="moba_prompt",
    )(q, k, vt)


def _moba_sample_kernel(pt_ref, q_ref, knew_ref, vnew_ref, ck_ref, cv_ref, o_ref,
                        kbuf, vbuf, sem, gate_ref, m_ref, l_ref, acc_ref, *, pages_per_step, n_blocks, n_new):
    j = pl.program_id(1)
    steps_per_seq = pl.num_programs(1)
    step = pl.program_id(0) * steps_per_seq + j
    n_steps = pl.num_programs(0) * steps_per_seq
    ahead = SAMPLE_PAGE_SLOTS - 1

    def page_copies(s, slot):
        out = []
        for i in range(pages_per_step):
            page = pt_ref[s * pages_per_step + i]
            out.append(pltpu.make_async_copy(ck_ref.at[page], kbuf.at[slot, i], sem.at[0, slot]))
            out.append(pltpu.make_async_copy(cv_ref.at[page], vbuf.at[slot, i], sem.at[1, slot]))
        return out

    @pl.when(step == 0)
    def _():
        for s in range(ahead):
            for c in page_copies(s, s):
                c.start()

    @pl.when(step + ahead < n_steps)
    def _():
        for c in page_copies(step + ahead, lax.rem(step + ahead, SAMPLE_PAGE_SLOTS)):
            c.start()

    slot = lax.rem(step, SAMPLE_PAGE_SLOTS)
    for c in page_copies(step, slot):
        c.wait()

    pages_per_block = MOBA_BLOCK // PAGE_SIZE
    blocks_per_step = pages_per_step // pages_per_block
    page_rows = PAGE_SIZE * KV_HEADS
    nkv = GROUP * n_new
    nrow = KV_HEADS * nkv
    lane = lax.broadcasted_iota(jnp.int32, (nrow, LANES), 1)
    scale = HEAD_DIM ** -0.5

    @pl.when(j == 0)
    def _():
        gate_ref[...] = jnp.zeros(gate_ref.shape, F32)
        m_ref[...] = jnp.zeros(m_ref.shape, F32)
        l_ref[...] = jnp.zeros(l_ref.shape, F32)

    qf = jnp.concatenate([q_ref[0, :, hq * HEAD_DIM:(hq + 1) * HEAD_DIM] for hq in range(Q_HEADS)], axis=0)
    qs = (qf * scale).astype(BF16)
    blk_cols = pages_per_block * page_rows
    col_head = lax.broadcasted_iota(jnp.int32, (nrow, blk_cols), 1) & (KV_HEADS - 1)
    row_head = lax.div(lax.broadcasted_iota(jnp.int32, (nrow, blk_cols), 0), nkv)
    same_head = col_head == row_head
    sub = SUBLANES

    blocks = range(blocks_per_step)
    scores, gcols = [], []
    for bi in blocks:
        kpages = [kbuf[slot, bi * pages_per_block + pp] for pp in range(pages_per_block)]
        scores.append(jnp.concatenate([_dot_nt(qs, kp.astype(BF16)) for kp in kpages], axis=1))
        ksum = sum(jnp.sum(kp.reshape(page_rows // sub, sub, HEAD_DIM), axis=0) for kp in kpages)
        mean = sum(ksum[i * KV_HEADS:(i + 1) * KV_HEADS] for i in range(sub // KV_HEADS)) * (1.0 / MOBA_BLOCK)
        mean_rows = jnp.concatenate(
            [jnp.broadcast_to(mean[h:h + 1], (nkv, HEAD_DIM)) for h in range(KV_HEADS)], axis=0)
        gcols.append(jnp.sum(qf * mean_rows, axis=-1, keepdims=True))
    probs, maxes, sums = [], [], []
    for bi in blocks:
        s = jnp.where(same_head, scores[bi], NEG_INF)
        mb = jnp.max(s, axis=-1, keepdims=True)
        p = jnp.exp(s - mb)
        maxes.append(mb)
        sums.append(jnp.sum(p, axis=-1, keepdims=True))
        probs.append(p.astype(BF16))
    gate, m_all, l_all = gate_ref[...], m_ref[...], l_ref[...]
    for bi in blocks:
        n = j * blocks_per_step + bi
        acc = _dot(probs[bi][:, :page_rows], vbuf[slot, bi * pages_per_block].astype(BF16))
        for pp in range(1, pages_per_block):
            acc = acc + _dot(probs[bi][:, pp * page_rows:(pp + 1) * page_rows],
                             vbuf[slot, bi * pages_per_block + pp].astype(BF16))
        acc_ref[n] = acc
        hot = lane == n
        gate = jnp.where(hot, gcols[bi], gate)
        m_all = jnp.where(hot, maxes[bi], m_all)
        l_all = jnp.where(hot, sums[bi], l_all)
    gate_ref[...] = gate
    m_ref[...] = m_all
    l_ref[...] = l_all

    @pl.when(j == pl.num_programs(1) - 1)
    def _():
        lane = lax.broadcasted_iota(jnp.int32, (nkv, LANES), 1)
        qrow_t = lax.rem(lax.broadcasted_iota(jnp.int32, (nkv, LANES), 0), n_new)
        lane_f = lane.astype(F32)
        for h in range(KV_HEADS):
            hd = slice(h * HEAD_DIM, (h + 1) * HEAD_DIM)
            rs = slice(h * nkv, (h + 1) * nkv)
            qsc = qf[rs] * scale
            gg = jnp.where(lane < n_blocks, gate_ref[rs, :], -jnp.inf)
            picked = jnp.zeros((nkv, LANES), F32)
            for _ in range(min(MOBA_TOPK, n_blocks)):
                mx = jnp.max(gg, axis=-1, keepdims=True)
                first = jnp.min(jnp.where(gg == mx, lane_f, float(LANES)), axis=-1, keepdims=True)
                hit = lane_f == first
                picked = jnp.where(hit, 1.0, picked)
                gg = jnp.where(hit, -jnp.inf, gg)
            sel = jnp.where(lane < n_blocks, picked, 0.0) > 0.0
            knew = knew_ref[0, :, hd]
            vnew = vnew_ref[0, :, hd]
            s_own = jnp.full((nkv, LANES), NEG_INF, F32)
            for jj in range(n_new):
                col = jnp.sum(qsc * knew[jj:jj + 1, :], axis=-1, keepdims=True)
                s_own = jnp.where(jnp.logical_and(lane == jj, qrow_t >= jj), col, s_own)
            mh = m_ref[rs, :]
            m_tot = jnp.maximum(jnp.max(s_own, axis=-1, keepdims=True),
                                jnp.max(jnp.where(sel, mh, NEG_INF), axis=-1, keepdims=True))
            w = jnp.where(sel, jnp.exp(mh - m_tot), 0.0)
            p_own = jnp.exp(s_own - m_tot)
            den = jnp.sum(w * l_ref[rs, :], axis=-1, keepdims=True) + jnp.sum(p_own, axis=-1, keepdims=True)
            num = jnp.zeros((nkv, HEAD_DIM), F32)
            for jj in range(n_new):
                num = num + p_own[:, jj:jj + 1] * vnew[jj:jj + 1, :]
            for n in range(n_blocks):
                num = num + w[:, n:n + 1] * acc_ref[n, rs, :]
            out = num / den
            for g in range(GROUP):
                o_ref[0, :, (h * GROUP + g) * HEAD_DIM:(h * GROUP + g + 1) * HEAD_DIM] = (
                    out[g * n_new:(g + 1) * n_new, :])


def _moba_sample(q, knew, vnew, cache_k, cache_v, page_table):
    db, ds, _ = q.shape
    n_pages = page_table.shape[1]
    pages_per_block = MOBA_BLOCK // PAGE_SIZE
    assert n_pages % pages_per_block == 0, "past length must be a whole number of MoBA blocks"
    n_blocks = n_pages // pages_per_block
    assert 0 < n_blocks <= LANES
    assert KV_HEADS & (KV_HEADS - 1) == 0 and 8 % KV_HEADS == 0
    pps = math.gcd(SAMPLE_PAGES_PER_STEP, n_pages)
    assert pps % pages_per_block == 0
    nrow = Q_HEADS * ds
    kdim = KV_HEADS * HEAD_DIM
    assert db * (n_pages // pps) >= SAMPLE_PAGE_SLOTS - 1
    page_buf = pltpu.VMEM((SAMPLE_PAGE_SLOTS, pps, PAGE_SIZE * KV_HEADS, HEAD_DIM), F32)

    per_seq = lambda w: pl.BlockSpec((1, ds, w), lambda b, j, pt: (b, 0, 0))
    hbm = pl.BlockSpec(memory_space=pl.ANY)
    grid_spec = pltpu.PrefetchScalarGridSpec(
        num_scalar_prefetch=1,
        grid=(db, n_pages // pps),
        in_specs=[per_seq(Q_HEADS * HEAD_DIM), per_seq(kdim), per_seq(kdim), hbm, hbm],
        out_specs=per_seq(Q_HEADS * HEAD_DIM),
        scratch_shapes=[
            page_buf, page_buf, pltpu.SemaphoreType.DMA((2, SAMPLE_PAGE_SLOTS)),
            pltpu.VMEM((nrow, LANES), F32),
            pltpu.VMEM((nrow, LANES), F32),
            pltpu.VMEM((nrow, LANES), F32),
            pltpu.VMEM((n_blocks, nrow, HEAD_DIM), F32),
        ],
    )
    return pl.pallas_call(
        functools.partial(_moba_sample_kernel, pages_per_step=pps, n_blocks=n_blocks, n_new=ds),
        grid_spec=grid_spec,
        out_shape=jax.ShapeDtypeStruct(q.shape, F32),
        compiler_params=pltpu.CompilerParams(
            dimension_semantics=("arbitrary", "arbitrary"), vmem_limit_bytes=VMEM_LIMIT_BYTES),
        name="moba_sample",
    )(page_table.reshape(-1), q, knew, vnew, cache_k, cache_v)


def _rope_tables(pos):
    inv = ROPE_THETA ** (-jnp.arange(0, HEAD_DIM, 2, dtype=F32) / HEAD_DIM)
    ang = pos.astype(F32)[:, None] * inv[None, :]
    cos, sin = jnp.cos(ang), jnp.sin(ang)
    return jnp.concatenate([cos, cos], axis=-1), jnp.concatenate([-sin, sin], axis=-1)


def kernel(x_prompt, x_sample, state_gla, cache_k, cache_v, page_table, w_in_a, w_g2, b_g, g_gla_out, w_o_a,
           norm_kv, w_kv, g_k, w_q_b, g_q, w_o_b, norm_mix, norm_mlp, w_up, w_down):
    bsz, t_len, _ = x_prompt.shape
    db, ds, _ = x_sample.shape
    n_a = state_gla.shape[0]
    depth = norm_mix.shape[0]
    assert n_a == 1 and depth == 2, "one GLA layer followed by one MoBA layer"
    assert t_len % PROMPT_ROW_TILE == 0 and t_len % PROMPT_GLA_CHUNK == 0 and t_len % MOBA_BLOCK == 0
    past = page_table.shape[1] * PAGE_SIZE
    kdim = KV_HEADS * HEAD_DIM
    row2 = lambda a: a.reshape(1, -1)

    main = 2 * GLA_KEY_DIM + 2 * GLA_VALUE_DIM
    wmain = w_in_a[0, :, :main].astype(BF16)
    wglr = jnp.pad(w_in_a[0, :, main:], ((0, 0), (0, LANES - GATE_RANK))).astype(BF16)
    wg2 = jnp.pad(w_g2[0], ((0, LANES - GATE_RANK), (0, 0))).astype(BF16)
    gla_w = (row2(norm_mix[0]), wmain, wglr, wg2, row2(b_g[0]), row2(g_gla_out[0]), w_o_a[0].astype(BF16))
    s0_p = jnp.zeros((bsz,) + state_gla.shape[2:], F32)
    seq_tile = math.gcd(SAMPLE_SEQ_TILE, db)
    hp, st_p = _gla_pipe(x_prompt, s0_p, *gla_w, seqs=1, chunk=PROMPT_GLA_CHUNK)
    hs, st_s = _gla_pipe(x_sample, state_gla[0], *gla_w, seqs=seq_tile, chunk=ds)
    hp = hp.reshape(bsz * t_len, D_MODEL)
    hs = hs.reshape(db * ds, D_MODEL)
    wup = w_up.astype(BF16)
    wdn = w_down.astype(BF16)
    hp = _mlp(hp, row2(norm_mlp[0]), wup[0], wdn[0])
    hs = _mlp(hs, row2(norm_mlp[0]), wup[0], wdn[0])

    cos_p, sin_p = _rope_tables(jnp.arange(t_len, dtype=jnp.int32))
    cos_s, sin_s = _rope_tables(past + jnp.arange(ds, dtype=jnp.int32))
    cos_s, sin_s = jnp.tile(cos_s, (db, 1)), jnp.tile(sin_s, (db, 1))
    kvq_w = (row2(norm_kv), row2(norm_mix[1]), w_kv.astype(BF16), w_q_b[0].astype(BF16), row2(g_k), row2(g_q[0]))
    k_p, q_p, k4_p, v4_p, vt_p = _kvq(hp, *kvq_w, cos_p, sin_p, tm=PROMPT_ROW_TILE, seq_len=t_len)
    k_s, q_s, k4_s, v4_s, v_s = _kvq(hs, *kvq_w, cos_s, sin_s, tm=db * ds)

    a_p = _moba_prompt(q_p.reshape(bsz, t_len, -1), k_p.reshape(bsz, t_len, kdim), vt_p)
    pages = lambda c: c.reshape(c.shape[0], PAGE_SIZE * KV_HEADS, HEAD_DIM)
    a_s = _moba_sample(q_s.reshape(db, ds, -1), k_s.reshape(db, ds, kdim), v_s.reshape(db, ds, kdim),
                       pages(cache_k), pages(cache_v), page_table)
    wob = w_o_b[0].astype(BF16)
    hp = _mlp(hp, row2(norm_mlp[1]), wup[1], wdn[1], attn=a_p.reshape(bsz * t_len, -1), wo=wob)
    hs = _mlp(hs, row2(norm_mlp[1]), wup[1], wdn[1], attn=a_s.reshape(db * ds, -1), wo=wob)

    kv4 = lambda a, n, t: a.reshape(n, t, KV_HEADS, HEAD_DIM)
    return (hp.reshape(bsz, t_len, D_MODEL), hs.reshape(db, ds, D_MODEL), st_p[None], st_s[None],
            kv4(k4_p, bsz, t_len), kv4(v4_p, bsz, t_len), kv4(k4_s, db, ds), kv4(v4_s, db, ds))
```

```python
import functools
import math

import jax
import jax.numpy as jnp
from jax import lax
from jax.experimental import pallas as pl
from jax.experimental.pallas import tpu as pltpu

F32 = jnp.float32
BF16 = jnp.bfloat16

D_MODEL = 1024
D_FF = 4 * D_MODEL
NORM_EPS = 1e-6
GLA_HEADS = 4
GLA_DK = 128
GLA_DV = 256
GLA_KEY_DIM = GLA_HEADS * GLA_DK
GLA_VALUE_DIM = GLA_HEADS * GLA_DV
GATE_RANK = 16
GATE_NORMALIZER = 16.0
HEAD_DIM = 128
Q_HEADS = 8
KV_HEADS = 4
GROUP = Q_HEADS // KV_HEADS
MOBA_BLOCK = 256
MOBA_TOPK = 3
PAGE_SIZE = 128
ROPE_THETA = 10000.0
NEG_INF = -1e30
LANES = 128
SUBLANES = 8

VMEM_LIMIT_BYTES = 56 * 1024 * 1024

PROMPT_ROW_TILE = 256
GLA_MAX_FACTORED_SPAN = 40.0
PROMPT_GLA_CHUNK = 256
PROMPT_GLA_SEQS = 1
MLP_ROW_TILE = 512
MLP_FF_TILE = 1024
MOBA_KV_HEADS_PER_STEP = 4
SAMPLE_SEQ_TILE = 4
SAMPLE_PAGES_PER_STEP = 16
SAMPLE_PAGE_SLOTS = 3


def _rms(x):
    return x * lax.rsqrt(jnp.mean(x * x, axis=-1, keepdims=True) + NORM_EPS)


def _log_sigmoid(x):
    return jnp.minimum(x, 0.0) - jnp.log1p(jnp.exp(-jnp.abs(x)))


def _dot(a, b):
    return jnp.dot(a, b, preferred_element_type=F32)


def _dot_nt(a, b, precision=None):
    return lax.dot_general(a, b, (((1,), (1,)), ((), ())), precision=precision,
                           preferred_element_type=F32)


def _dot_tn(a, b):
    return lax.dot_general(a, b, (((0,), (0,)), ((), ())), preferred_element_type=F32)


def _const_spec(shape):
    n = len(shape)
    return pl.BlockSpec(shape, lambda *_: (0,) * n)


def _gla_pipe_kernel(x_ref, s0_ref, gmix_ref, wmain_ref, wglr_ref, wg2_ref, bg_ref, gout_ref, wo_ref,
                     h_ref, sout_ref, proj_ref, gk_ref, xk_ref, b_ref, oi_ref, o_ref, s_ref, *,
                     seqs, chunk, tiles_per_seq):
    i = pl.program_id(0)
    rows = seqs * chunk
    slot = lax.rem(i, 2)
    prev = 1 - slot
    tile_b = jnp.maximum(i - 1, 0)
    t_in_seq = lax.rem(tile_b, tiles_per_seq)
    kq = 2 * GLA_KEY_DIM
    k_off, v_off, r_off = GLA_KEY_DIM, 2 * GLA_KEY_DIM, 2 * GLA_KEY_DIM + GLA_VALUE_DIM

    @pl.when(i == 0)
    def _():
        proj_ref[1] = jnp.zeros(proj_ref.shape[1:], F32)
        gk_ref[1] = jnp.zeros(gk_ref.shape[1:], F32)
        xk_ref[1] = jnp.zeros(xk_ref.shape[1:], F32)

    @pl.when(t_in_seq == 0)
    def _():
        s_ref[...] = s0_ref[...]

    r_i = lax.broadcasted_iota(jnp.int32, (chunk, chunk), 0)
    c_i = lax.broadcasted_iota(jnp.int32, (chunk, chunk), 1)
    causal = c_i <= r_i
    mm = BF16 if chunk % 16 == 0 else F32
    tri = jnp.where(causal, 1.0, 0.0).astype(mm)
    n_sub = 2 if chunk % 32 == 0 else 1
    sub = chunk // n_sub
    half = max(sub // 2, 1)
    same_sub = causal if n_sub == 1 else jnp.logical_and(causal, (r_i >= sub) == (c_i >= sub))
    gout = gout_ref[...]
    units = [(sq, h) for sq in range(seqs) for h in range(GLA_HEADS)]
    rs = [slice(sq * chunk, (sq + 1) * chunk) for sq, _ in units]
    kd = [slice(h * GLA_DK, (h + 1) * GLA_DK) for _, h in units]
    vd = [slice(h * GLA_DV, (h + 1) * GLA_DV) for _, h in units]
    n_units = range(len(units))

    bcum = []
    for sq in range(seqs):
        g = gk_ref[prev, sq * chunk:(sq + 1) * chunk, :]
        g_hi = g.astype(BF16).astype(F32)
        bcum.append(_dot(tri, g_hi.astype(mm)) + _dot(tri, (g - g_hi).astype(mm)))

    b = [bcum[sq][:, kd[u]] for u, (sq, _) in enumerate(units)]
    b_last = [b[u][chunk - 1:chunk] for u in n_units]
    q = [proj_ref[prev, rs[u], kd[u]] * (GLA_DK ** -0.5) for u in n_units]
    k = [proj_ref[prev, rs[u], k_off + h * GLA_DK:k_off + (h + 1) * GLA_DK] for u, (_, h) in enumerate(units)]
    v = [proj_ref[prev, rs[u], v_off + h * GLA_DV:v_off + (h + 1) * GLA_DV].astype(mm)
         for u, (_, h) in enumerate(units)]
    s_old = [s_ref[sq, h] for sq, h in units]
    q_e = [(q[u] * jnp.exp(b[u])).astype(BF16) for u in n_units]
    k_e = [(k[u] * jnp.exp(b_last[u] - b[u])).astype(mm) for u in n_units]
    q_d, k_d = [], []
    for u in n_units:
        mids = jnp.concatenate(
            [jnp.broadcast_to(b[u][j * sub + half - 1:j * sub + half], (sub, GLA_DK)) for j in range(n_sub)],
            axis=0)
        q_d.append((q[u] * jnp.exp(b[u] - mids)).astype(mm))
        k_d.append((k[u] * jnp.exp(mids - b[u])).astype(mm))

    x = x_ref[...].reshape(rows, D_MODEL)
    xn = (_rms(x) * gmix_ref[...]).astype(BF16)
    proj_ref[slot, :, :kq] = _dot(xn, wmain_ref[:, :kq])
    glr = _dot(xn, wglr_ref[...])
    gpre = _dot(glr.astype(BF16), wg2_ref[...]) + bg_ref[...]
    gk_ref[slot] = _log_sigmoid(gpre) * (1.0 / GATE_NORMALIZER)

    o_inter = [_dot(q_e[u], s_old[u].astype(BF16)) for u in n_units]
    kv = [_dot_tn(k_e[u], v[u]) for u in n_units]
    for u, (sq, h) in enumerate(units):
        decay_col = jnp.transpose(jnp.broadcast_to(jnp.exp(b_last[u]), (GLA_DK, GLA_DK)))
        s_ref[sq, h] = s_old[u] * jnp.concatenate([decay_col, decay_col], axis=1) + kv[u]
    proj_ref[slot, :, kq:r_off] = _dot(xn, wmain_ref[:, kq:r_off])

    a = []
    for u in n_units:
        a.append(jnp.where(same_sub, _dot_nt(q_d[u], k_d[u]), 0.0).astype(mm))
    intra = [_dot(a[u], v[u]) for u in n_units]
    if n_sub == 2:
        a_x = []
        for u in n_units:
            edge = b[u][sub - 1:sub]
            q_x = q[u][sub:] * jnp.exp(b[u][sub:] - edge)
            k_x = k[u][:sub] * jnp.exp(edge - b[u][:sub])
            a_x.append(_dot_nt(q_x.astype(mm), k_x.astype(mm)).astype(mm))
        for u in n_units:
            cross = _dot(a_x[u], v[u][:sub])
            intra[u] = jnp.concatenate([intra[u][:sub], intra[u][sub:] + cross], axis=0)

    proj_ref[slot, :, r_off:] = _dot(xn, wmain_ref[:, r_off:])
    xk_ref[slot] = x

    def gated_norm(o, u):
        h = units[u][1]
        r = proj_ref[prev, rs[u], r_off + h * GLA_DV:r_off + (h + 1) * GLA_DV]
        return _rms(o) * gout * (r * jax.nn.sigmoid(r))

    for sq in range(seqs):
        b_ref[sq * chunk:(sq + 1) * chunk, :] = bcum[sq]
    for u in n_units:
        oi_ref[rs[u], vd[u]] = o_inter[u]
        o_ref[rs[u], vd[u]] = gated_norm(o_inter[u] + intra[u], u)

    def finish():
        y = _dot(o_ref[...].astype(BF16), wo_ref[...])
        h_ref[...] = (xk_ref[prev] + y).reshape(h_ref.shape)

    finish()

    span = jnp.zeros((1, GLA_DK), F32)
    for u in n_units:
        before = jnp.zeros((1, GLA_DK), F32)
        for j in range(n_sub):
            end = b[u][(j + 1) * sub - 1:(j + 1) * sub]
            span = jnp.maximum(span, before - end)
            before = end

    @pl.when(jnp.max(span) > GLA_MAX_FACTORED_SPAN)
    def _():
        col = lax.broadcasted_iota(jnp.int32, (chunk, GLA_DK), 0)

        def row_group(gi, c):
            for u, (sq, h) in enumerate(units):
                grp = pl.ds(pl.multiple_of(sq * chunk + gi * SUBLANES, SUBLANES), SUBLANES)
                b_g = b_ref[grp, kd[u]]
                q_g = proj_ref[prev, grp, kd[u]] * (GLA_DK ** -0.5)
                b_u = b_ref[rs[u], kd[u]]
                k_u = proj_ref[prev, rs[u], k_off + h * GLA_DK:k_off + (h + 1) * GLA_DK]
                v_u = proj_ref[prev, rs[u], v_off + h * GLA_DV:v_off + (h + 1) * GLA_DV]
                out_rows = []
                for r in range(SUBLANES):
                    live = col <= gi * SUBLANES + r
                    decay = jnp.where(live, jnp.exp(jnp.minimum(b_g[r:r + 1] - b_u, 0.0)), 0.0)
                    a_col = jnp.sum(decay * k_u * q_g[r:r + 1], axis=-1, keepdims=True)
                    out_rows.append(jnp.sum(a_col * v_u, axis=0, keepdims=True))
                oi_ref[grp, vd[u]] += jnp.concatenate(out_rows, axis=0)
            return c

        lax.fori_loop(0, chunk // SUBLANES, row_group, 0)
        for u in n_units:
            o_ref[rs[u], vd[u]] = gated_norm(oi_ref[rs[u], vd[u]], u)
        finish()

    @pl.when(t_in_seq == tiles_per_seq - 1)
    def _():
        sout_ref[...] = s_ref[...]


def _gla_pipe(x, s0, gmix, wmain, wglr, wg2, bg, gout, wo, *, seqs, chunk):
    nseq, t_len, _ = x.shape
    assert nseq % seqs == 0 and t_len % chunk == 0 and chunk % SUBLANES == 0
    tiles_per_seq = t_len // chunk
    n_tiles = (nseq // seqs) * tiles_per_seq
    rows = seqs * chunk
    state_block = (seqs, GLA_HEADS, GLA_DK, GLA_DV)
    proj_tile = lambda i: jnp.minimum(i, n_tiles - 1)
    rec_tile = lambda i: jnp.maximum(i - 1, 0)
    x_spec = lambda tile: pl.BlockSpec(
        (seqs, chunk, D_MODEL), lambda i: (tile(i) // tiles_per_seq, tile(i) % tiles_per_seq, 0))
    s_spec = pl.BlockSpec(state_block, lambda i: (rec_tile(i) // tiles_per_seq, 0, 0, 0))
    kern = functools.partial(_gla_pipe_kernel, seqs=seqs, chunk=chunk, tiles_per_seq=tiles_per_seq)
    resident = lambda w: pl.BlockSpec(w.shape, lambda i: (0, 0), pipeline_mode=pl.Buffered(1))
    return pl.pallas_call(
        kern,
        grid=(n_tiles + 1,),
        in_specs=[
            x_spec(proj_tile), s_spec,
            _const_spec(gmix.shape), resident(wmain), resident(wglr),
            resident(wg2), _const_spec(bg.shape), _const_spec(gout.shape), resident(wo),
        ],
        out_specs=[x_spec(rec_tile), s_spec],
        out_shape=[jax.ShapeDtypeStruct(x.shape, F32), jax.ShapeDtypeStruct(s0.shape, F32)],
        scratch_shapes=[
            pltpu.VMEM((2, rows, 2 * GLA_KEY_DIM + 2 * GLA_VALUE_DIM), F32),
            pltpu.VMEM((2, rows, GLA_KEY_DIM), F32),
            pltpu.VMEM((2, rows, D_MODEL), F32),
            pltpu.VMEM((rows, GLA_KEY_DIM), F32),
            pltpu.VMEM((rows, GLA_VALUE_DIM), F32),
            pltpu.VMEM((rows, GLA_VALUE_DIM), F32),
            pltpu.VMEM(state_block, F32),
        ],
        compiler_params=pltpu.CompilerParams(
            dimension_semantics=("arbitrary",), vmem_limit_bytes=VMEM_LIMIT_BYTES),
        name="gla_layer",
    )(x, s0, gmix, wmain, wglr, wg2, bg, gout, wo)


def _mlp_kernel(*refs, has_attn):
    if has_attn:
        h_ref, a_ref, wo_ref, g_ref, wup_ref, wdn_ref, out_ref, act_ref = refs
        h = h_ref[...] + _dot(a_ref[...].astype(BF16), wo_ref[...])
    else:
        h_ref, g_ref, wup_ref, wdn_ref, out_ref, act_ref = refs
        h = h_ref[...]
    xn = (_rms(h) * g_ref[...]).astype(BF16)
    for j in range(D_FF // MLP_FF_TILE):
        ff = slice(j * MLP_FF_TILE, (j + 1) * MLP_FF_TILE)
        a = jnp.maximum(_dot(xn, wup_ref[:, ff]), 0.0)
        act_ref[:, ff] = (a * a).astype(BF16)
    out_ref[...] = h + _dot(act_ref[...], wdn_ref[...])


def _mlp(h, g, wup, wdn, attn=None, wo=None):
    n = h.shape[0]
    tm = min(MLP_ROW_TILE, n)
    row_spec = pl.BlockSpec((tm, D_MODEL), lambda i: (i, 0))
    single = pl.Buffered(1)

    def wspec(w):
        return pl.BlockSpec(w.shape, lambda i: (0, 0), pipeline_mode=single)

    if attn is None:
        args = (h, g, wup, wdn)
        in_specs = [row_spec, _const_spec(g.shape), wspec(wup), wspec(wdn)]
    else:
        args = (h, attn, wo, g, wup, wdn)
        in_specs = [row_spec, row_spec, wspec(wo), _const_spec(g.shape), wspec(wup), wspec(wdn)]
    return pl.pallas_call(
        functools.partial(_mlp_kernel, has_attn=attn is not None),
        grid=(n // tm,),
        in_specs=in_specs,
        out_specs=row_spec,
        out_shape=jax.ShapeDtypeStruct(h.shape, F32),
        scratch_shapes=[pltpu.VMEM((tm, D_FF), BF16)],
        compiler_params=pltpu.CompilerParams(
            dimension_semantics=("arbitrary",), vmem_limit_bytes=VMEM_LIMIT_BYTES),
        name="mlp",
    )(*args)


def _rope(x, cos, sin_signed):
    return x * cos + pltpu.roll(x, HEAD_DIM // 2, 1) * sin_signed


def _kvq_kernel(h_ref, nkv_ref, nq_ref, wkv_ref, wq_ref, gk_ref, gq_ref, cos_ref, sin_ref,
                k_ref, q_ref, k4_ref, v4_ref, v_ref, *, transposed_v):
    tm = h_ref.shape[0]
    xh = _rms(h_ref[...])
    kv = _dot((xh * nkv_ref[...]).astype(BF16), wkv_ref[...])
    qq = _dot((xh * nq_ref[...]).astype(BF16), wq_ref[...])
    cos = cos_ref[...]
    sin = sin_ref[...]
    kdim = KV_HEADS * HEAD_DIM
    for hh in range(KV_HEADS):
        hd = slice(hh * HEAD_DIM, (hh + 1) * HEAD_DIM)
        kh = _rope(_rms(kv[:, hd]) * gk_ref[...], cos, sin)
        vh = kv[:, kdim + hh * HEAD_DIM:kdim + (hh + 1) * HEAD_DIM]
        k_ref[:, hd] = kh
        k4_ref[pl.ds(hh, tm, stride=KV_HEADS), :] = kh
        v4_ref[pl.ds(hh, tm, stride=KV_HEADS), :] = vh
        if transposed_v:
            v_ref[0, hh] = jnp.transpose(vh).astype(BF16)
    if not transposed_v:
        v_ref[...] = kv[:, kdim:]
    for hh in range(Q_HEADS):
        hd = slice(hh * HEAD_DIM, (hh + 1) * HEAD_DIM)
        q_ref[:, hd] = _rope(_rms(qq[:, hd]) * gq_ref[...], cos, sin)


def _kvq(h, nkv, nq, wkv, wq, gk, gq, cos, sin, *, tm, seq_len=None):
    n = h.shape[0]
    pos_tiles = cos.shape[0] // tm
    kdim = KV_HEADS * HEAD_DIM
    row = lambda w: pl.BlockSpec((tm, w), lambda i: (i, 0))
    row4 = pl.BlockSpec((tm * KV_HEADS, HEAD_DIM), lambda i: (i, 0))
    tab = pl.BlockSpec((tm, HEAD_DIM), lambda i: (i % pos_tiles, 0))
    out_specs = [row(kdim), row(Q_HEADS * HEAD_DIM), row4, row4]
    out_shape = [jax.ShapeDtypeStruct((n, kdim), F32), jax.ShapeDtypeStruct((n, Q_HEADS * HEAD_DIM), F32),
                 jax.ShapeDtypeStruct((n * KV_HEADS, HEAD_DIM), F32),
                 jax.ShapeDtypeStruct((n * KV_HEADS, HEAD_DIM), F32)]
    if seq_len is not None:
        assert seq_len % tm == 0 and tm % LANES == 0
        tiles = seq_len // tm
        out_specs.append(pl.BlockSpec((1, KV_HEADS, HEAD_DIM, tm), lambda i: (i // tiles, 0, 0, i % tiles)))
        out_shape.append(jax.ShapeDtypeStruct((n // seq_len, KV_HEADS, HEAD_DIM, seq_len), BF16))
    else:
        out_specs.append(row(kdim))
        out_shape.append(jax.ShapeDtypeStruct((n, kdim), F32))
    return pl.pallas_call(
        functools.partial(_kvq_kernel, transposed_v=seq_len is not None),
        grid=(n // tm,),
        in_specs=[row(D_MODEL), _const_spec(nkv.shape), _const_spec(nq.shape), _const_spec(wkv.shape),
                  _const_spec(wq.shape), _const_spec(gk.shape), _const_spec(gq.shape), tab, tab],
        out_specs=out_specs,
        out_shape=out_shape,
        compiler_params=pltpu.CompilerParams(
            dimension_semantics=("arbitrary",), vmem_limit_bytes=VMEM_LIMIT_BYTES),
        name="kvq",
    )(h, nkv, nq, wkv, wq, gk, gq, cos, sin)


def _select_topk_rows(gate, cand, ksel):
    n = gate.shape[0]
    row = lax.broadcasted_iota(jnp.int32, gate.shape, 0).astype(F32)
    gg = jnp.where(cand, gate, -jnp.inf)
    sel = jnp.zeros(gate.shape, F32)
    for _ in range(ksel):
        mx = jnp.max(gg, axis=0, keepdims=True)
        first = jnp.min(jnp.where(gg == mx, row, float(n)), axis=0, keepdims=True)
        hit = row == first
        sel = jnp.where(hit, 1.0, sel)
        gg = jnp.where(hit, -jnp.inf, gg)
    return jnp.where(cand, sel, 0.0)


def _split_bf16(x):
    hi = x.astype(BF16)
    return hi, (x - hi.astype(F32)).astype(BF16)


def _moba_prompt_kernel(q_ref, k_ref, vt_ref, o_ref, means_ref, s_ref, p_ref, *, nblk):
    qi = pl.program_id(2)
    sub = SUBLANES
    kv_heads = k_ref.shape[2] // HEAD_DIM
    streams = [(kv, kv * GROUP + g) for kv in range(kv_heads) for g in range(GROUP)]

    @pl.when(qi == 0)
    def _():
        means_ref[...] = jnp.zeros(means_ref.shape, F32)
        for kv in range(kv_heads):
            for n in range(nblk):
                blk = k_ref[0, n * MOBA_BLOCK:(n + 1) * MOBA_BLOCK, kv * HEAD_DIM:(kv + 1) * HEAD_DIM]
                means_ref[kv, n:n + 1, :] = jnp.sum(blk, axis=0, keepdims=True) * (1.0 / MOBA_BLOCK)

    def attend(nb):
        qb = q_ref[0]
        heads = [qb[:, hq * HEAD_DIM:(hq + 1) * HEAD_DIM] for _, hq in streams]
        qs = [(h * (HEAD_DIM ** -0.5 * math.log2(math.e))).astype(BF16) for h in heads]
        causal = (lax.broadcasted_iota(jnp.int32, (MOBA_BLOCK, MOBA_BLOCK), 0)
                  <= lax.broadcasted_iota(jnp.int32, (MOBA_BLOCK, MOBA_BLOCK), 1))
        ksel = min(MOBA_TOPK, nblk)
        keys = (nb + 1) * MOBA_BLOCK
        k_all = [k_ref[0, :keys, kv * HEAD_DIM:(kv + 1) * HEAD_DIM].astype(BF16) for kv in range(kv_heads)]

        sel = [None] * len(streams)
        if nb > ksel:
            row = lax.broadcasted_iota(jnp.int32, (means_ref.shape[1], MOBA_BLOCK), 0)
            for st, (kv, _) in enumerate(streams):
                m_hi, m_lo = _split_bf16(means_ref[kv])
                q_hi, q_lo = _split_bf16(heads[st])
                gate = _dot_nt(m_hi, q_hi) + _dot_nt(m_hi, q_lo) + _dot_nt(m_lo, q_hi)
                sel[st] = _select_topk_rows(gate, row < nb, ksel)
        m_parts = []
        for st, (kv, _) in enumerate(streams):
            m_part = None
            s_all = _dot_nt(k_all[kv], qs[st])
            for n in range(nb + 1):
                s = s_all[n * MOBA_BLOCK:(n + 1) * MOBA_BLOCK]
                if n == nb:
                    s = jnp.where(causal, s, NEG_INF)
                elif sel[st] is not None:
                    s = jnp.where(sel[st][n:n + 1, :] > 0.0, s, NEG_INF)
                s_ref[st, n] = s
                part = jnp.max(s.reshape(MOBA_BLOCK // sub, sub, MOBA_BLOCK), axis=0)
                m_part = part if m_part is None else jnp.maximum(m_part, part)
            m_parts.append(m_part)
        for st, (kv, hq) in enumerate(streams):
            m = jnp.max(m_parts[st], axis=0, keepdims=True)
            l_part = jnp.zeros((sub, MOBA_BLOCK), F32)
            for n in range(nb + 1):
                p = jnp.exp2(s_ref[st, n] - m)
                l_part = l_part + jnp.sum(p.reshape(MOBA_BLOCK // sub, sub, MOBA_BLOCK), axis=0)
                p_ref[st, n * MOBA_BLOCK:(n + 1) * MOBA_BLOCK, :] = p.astype(BF16)
            acc = _dot(vt_ref[0, kv, :, :keys], p_ref[st, :keys, :])
            out = acc * (1.0 / jnp.sum(l_part, axis=0, keepdims=True))
            o_ref[0, :, hq * HEAD_DIM:(hq + 1) * HEAD_DIM] = jnp.transpose(out)

    for nb in range(nblk):
        pl.when(qi == nb)(functools.partial(attend, nb))


def _moba_prompt(q, k, vt):
    bsz, t_len, _ = q.shape
    nblk = t_len // MOBA_BLOCK
    nblk_pad = -(-nblk // SUBLANES) * SUBLANES
    kvs = math.gcd(MOBA_KV_HEADS_PER_STEP, KV_HEADS)
    n_streams = kvs * GROUP
    return pl.pallas_call(
        functools.partial(_moba_prompt_kernel, nblk=nblk),
        grid=(bsz, KV_HEADS // kvs, nblk),
        in_specs=[
            pl.BlockSpec((1, MOBA_BLOCK, n_streams * HEAD_DIM), lambda b, h, i: (b, i, h)),
            pl.BlockSpec((1, t_len, kvs * HEAD_DIM), lambda b, h, i: (b, 0, h)),
            pl.BlockSpec((1, kvs, HEAD_DIM, t_len), lambda b, h, i: (b, h, 0, 0)),
        ],
        out_specs=pl.BlockSpec((1, MOBA_BLOCK, n_streams * HEAD_DIM), lambda b, h, i: (b, i, h)),
        out_shape=jax.ShapeDtypeStruct(q.shape, F32),
        scratch_shapes=[
            pltpu.VMEM((kvs, nblk_pad, HEAD_DIM), F32),
            pltpu.VMEM((n_streams, nblk, MOBA_BLOCK, MOBA_BLOCK), F32),
            pltpu.VMEM((n_streams, t_len, MOBA_BLOCK), BF16),
        ],
        compiler_params=pltpu.CompilerParams(
            dimension_semantics=("arbitrary", "arbitrary", "arbitrary"), vmem_limit_bytes=VMEM_LIMIT_BYTES),
        name="moba_prompt",
    )(q, k, vt)


def _moba_sample_kernel(pt_ref, q_ref, knew_ref, vnew_ref, ck_ref, cv_ref, o_ref,
                        kbuf, vbuf, sem, gate_ref, m_ref, l_ref, acc_ref, *, pages_per_step, n_blocks, n_new):
    j = pl.program_id(1)
    steps_per_seq = pl.num_programs(1)
    step = pl.program_id(0) * steps_per_seq + j
    n_steps = pl.num_programs(0) * steps_per_seq
    ahead = SAMPLE_PAGE_SLOTS - 1

    def page_copies(s, slot):
        out = []
        for i in range(pages_per_step):
            page = pt_ref[s * pages_per_step + i]
            out.append(pltpu.make_async_copy(ck_ref.at[page], kbuf.at[slot, i], sem.at[0, slot]))
            out.append(pltpu.make_async_copy(cv_ref.at[page], vbuf.at[slot, i], sem.at[1, slot]))
        return out

    @pl.when(step == 0)
    def _():
        for s in range(ahead):
            for c in page_copies(s, s):
                c.start()

    @pl.when(step + ahead < n_steps)
    def _():
        for c in page_copies(step + ahead, lax.rem(step + ahead, SAMPLE_PAGE_SLOTS)):
            c.start()

    slot = lax.rem(step, SAMPLE_PAGE_SLOTS)
    for c in page_copies(step, slot):
        c.wait()

    pages_per_block = MOBA_BLOCK // PAGE_SIZE
    blocks_per_step = pages_per_step // pages_per_block
    page_rows = PAGE_SIZE * KV_HEADS
    nkv = GROUP * n_new
    nrow = KV_HEADS * nkv
    lane = lax.broadcasted_iota(jnp.int32, (nrow, LANES), 1)
    scale = HEAD_DIM ** -0.5

    @pl.when(j == 0)
    def _():
        gate_ref[...] = jnp.zeros(gate_ref.shape, F32)
        m_ref[...] = jnp.zeros(m_ref.shape, F32)
        l_ref[...] = jnp.zeros(l_ref.shape, F32)

    qf = jnp.concatenate([q_ref[0, :, hq * HEAD_DIM:(hq + 1) * HEAD_DIM] for hq in range(Q_HEADS)], axis=0)
    qs = (qf * scale).astype(BF16)
    blk_cols = pages_per_block * page_rows
    col_head = lax.broadcasted_iota(jnp.int32, (nrow, blk_cols), 1) & (KV_HEADS - 1)
    row_head = lax.div(lax.broadcasted_iota(jnp.int32, (nrow, blk_cols), 0), nkv)
    same_head = col_head == row_head
    sub = SUBLANES

    blocks = range(blocks_per_step)
    scores, gcols = [], []
    for bi in blocks:
        kpages = [kbuf[slot, bi * pages_per_block + pp] for pp in range(pages_per_block)]
        scores.append(jnp.concatenate([_dot_nt(qs, kp.astype(BF16)) for kp in kpages], axis=1))
        ksum = sum(jnp.sum(kp.reshape(page_rows // sub, sub, HEAD_DIM), axis=0) for kp in kpages)
        mean = sum(ksum[i * KV_HEADS:(i + 1) * KV_HEADS] for i in range(sub // KV_HEADS)) * (1.0 / MOBA_BLOCK)
        mean_rows = jnp.concatenate(
            [jnp.broadcast_to(mean[h:h + 1], (nkv, HEAD_DIM)) for h in range(KV_HEADS)], axis=0)
        gcols.append(jnp.sum(qf * mean_rows, axis=-1, keepdims=True))
    probs, maxes, sums = [], [], []
    for bi in blocks:
        s = jnp.where(same_head, scores[bi], NEG_INF)
        mb = jnp.max(s, axis=-1, keepdims=True)
        p = jnp.exp(s - mb)
        maxes.append(mb)
        sums.append(jnp.sum(p, axis=-1, keepdims=True))
        probs.append(p.astype(BF16))
    gate, m_all, l_all = gate_ref[...], m_ref[...], l_ref[...]
    for bi in blocks:
        n = j * blocks_per_step + bi
        acc = _dot(probs[bi][:, :page_rows], vbuf[slot, bi * pages_per_block].astype(BF16))
        for pp in range(1, pages_per_block):
            acc = acc + _dot(probs[bi][:, pp * page_rows:(pp + 1) * page_rows],
                             vbuf[slot, bi * pages_per_block + pp].astype(BF16))
        acc_ref[n] = acc
        hot = lane == n
        gate = jnp.where(hot, gcols[bi], gate)
        m_all = jnp.where(hot, maxes[bi], m_all)
        l_all = jnp.where(hot, sums[bi], l_all)
    gate_ref[...] = gate
    m_ref[...] = m_all
    l_ref[...] = l_all

    @pl.when(j == pl.num_programs(1) - 1)
    def _():
        lane = lax.broadcasted_iota(jnp.int32, (nkv, LANES), 1)
        qrow_t = lax.rem(lax.broadcasted_iota(jnp.int32, (nkv, LANES), 0), n_new)
        lane_f = lane.astype(F32)
        for h in range(KV_HEADS):
            hd = slice(h * HEAD_DIM, (h + 1) * HEAD_DIM)
            rs = slice(h * nkv, (h + 1) * nkv)
            qsc = qf[rs] * scale
            gg = jnp.where(lane < n_blocks, gate_ref[rs, :], -jnp.inf)
            picked = jnp.zeros((nkv, LANES), F32)
            for _ in range(min(MOBA_TOPK, n_blocks)):
                mx = jnp.max(gg, axis=-1, keepdims=True)
                first = jnp.min(jnp.where(gg == mx, lane_f, float(LANES)), axis=-1, keepdims=True)
                hit = lane_f == first
                picked = jnp.where(hit, 1.0, picked)
                gg = jnp.where(hit, -jnp.inf, gg)
            sel = jnp.where(lane < n_blocks, picked, 0.0) > 0.0
            knew = knew_ref[0, :, hd]
            vnew = vnew_ref[0, :, hd]
            s_own = jnp.full((nkv, LANES), NEG_INF, F32)
            for jj in range(n_new):
                col = jnp.sum(qsc * knew[jj:jj + 1, :], axis=-1, keepdims=True)
                s_own = jnp.where(jnp.logical_and(lane == jj, qrow_t >= jj), col, s_own)
            mh = m_ref[rs, :]
            m_tot = jnp.maximum(jnp.max(s_own, axis=-1, keepdims=True),
                                jnp.max(jnp.where(sel, mh, NEG_INF), axis=-1, keepdims=True))
            w = jnp.where(sel, jnp.exp(mh - m_tot), 0.0)
            p_own = jnp.exp(s_own - m_tot)
            den = jnp.sum(w * l_ref[rs, :], axis=-1, keepdims=True) + jnp.sum(p_own, axis=-1, keepdims=True)
            num = jnp.zeros((nkv, HEAD_DIM), F32)
            for jj in range(n_new):
                num = num + p_own[:, jj:jj + 1] * vnew[jj:jj + 1, :]
            for n in range(n_blocks):
                num = num + w[:, n:n + 1] * acc_ref[n, rs, :]
            out = num / den
            for g in range(GROUP):
                o_ref[0, :, (h * GROUP + g) * HEAD_DIM:(h * GROUP + g + 1) * HEAD_DIM] = (
                    out[g * n_new:(g + 1) * n_new, :])


def _moba_sample(q, knew, vnew, cache_k, cache_v, page_table):
    db, ds, _ = q.shape
    n_pages = page_table.shape[1]
    pages_per_block = MOBA_BLOCK // PAGE_SIZE
    assert n_pages % pages_per_block == 0, "past length must be a whole number of MoBA blocks"
    n_blocks = n_pages // pages_per_block
    assert 0 < n_blocks <= LANES
    assert KV_HEADS & (KV_HEADS - 1) == 0 and 8 % KV_HEADS == 0
    pps = math.gcd(SAMPLE_PAGES_PER_STEP, n_pages)
    assert pps % pages_per_block == 0
    nrow = Q_HEADS * ds
    kdim = KV_HEADS * HEAD_DIM
    assert db * (n_pages // pps) >= SAMPLE_PAGE_SLOTS - 1
    page_buf = pltpu.VMEM((SAMPLE_PAGE_SLOTS, pps, PAGE_SIZE * KV_HEADS, HEAD_DIM), F32)

    per_seq = lambda w: pl.BlockSpec((1, ds, w), lambda b, j, pt: (b, 0, 0))
    hbm = pl.BlockSpec(memory_space=pl.ANY)
    grid_spec = pltpu.PrefetchScalarGridSpec(
        num_scalar_prefetch=1,
        grid=(db, n_pages // pps),
        in_specs=[per_seq(Q_HEADS * HEAD_DIM), per_seq(kdim), per_seq(kdim), hbm, hbm],
        out_specs=per_seq(Q_HEADS * HEAD_DIM),
        scratch_shapes=[
            page_buf, page_buf, pltpu.SemaphoreType.DMA((2, SAMPLE_PAGE_SLOTS)),
            pltpu.VMEM((nrow, LANES), F32),
            pltpu.VMEM((nrow, LANES), F32),
            pltpu.VMEM((nrow, LANES), F32),
            pltpu.VMEM((n_blocks, nrow, HEAD_DIM), F32),
        ],
    )
    return pl.pallas_call(
        functools.partial(_moba_sample_kernel, pages_per_step=pps, n_blocks=n_blocks, n_new=ds),
        grid_spec=grid_spec,
        out_shape=jax.ShapeDtypeStruct(q.shape, F32),
        compiler_params=pltpu.CompilerParams(
            dimension_semantics=("arbitrary", "arbitrary"), vmem_limit_bytes=VMEM_LIMIT_BYTES),
        name="moba_sample",
    )(page_table.reshape(-1), q, knew, vnew, cache_k, cache_v)


def _rope_tables(pos):
    inv = ROPE_THETA ** (-jnp.arange(0, HEAD_DIM, 2, dtype=F32) / HEAD_DIM)
    ang = pos.astype(F32)[:, None] * inv[None, :]
    cos, sin = jnp.cos(ang), jnp.sin(ang)
    return jnp.concatenate([cos, cos], axis=-1), jnp.concatenate([-sin, sin], axis=-1)


def kernel(x_prompt, x_sample, state_gla, cache_k, cache_v, page_table, w_in_a, w_g2, b_g, g_gla_out, w_o_a,
           norm_kv, w_kv, g_k, w_q_b, g_q, w_o_b, norm_mix, norm_mlp, w_up, w_down):
    bsz, t_len, _ = x_prompt.shape
    db, ds, _ = x_sample.shape
    n_a = state_gla.shape[0]
    depth = norm_mix.shape[0]
    assert n_a == 1 and depth == 2, "one GLA layer followed by one MoBA layer"
    assert t_len % PROMPT_ROW_TILE == 0 and t_len % PROMPT_GLA_CHUNK == 0 and t_len % MOBA_BLOCK == 0
    past = page_table.shape[1] * PAGE_SIZE
    kdim = KV_HEADS * HEAD_DIM
    row2 = lambda a: a.reshape(1, -1)

    main = 2 * GLA_KEY_DIM + 2 * GLA_VALUE_DIM
    wmain = w_in_a[0, :, :main].astype(BF16)
    wglr = jnp.pad(w_in_a[0, :, main:], ((0, 0), (0, LANES - GATE_RANK))).astype(BF16)
    wg2 = jnp.pad(w_g2[0], ((0, LANES - GATE_RANK), (0, 0))).astype(BF16)
    gla_w = (row2(norm_mix[0]), wmain, wglr, wg2, row2(b_g[0]), row2(g_gla_out[0]), w_o_a[0].astype(BF16))
    s0_p = jnp.zeros((bsz,) + state_gla.shape[2:], F32)
    seq_tile = math.gcd(SAMPLE_SEQ_TILE, db)
    hp, st_p = _gla_pipe(x_prompt, s0_p, *gla_w, seqs=math.gcd(PROMPT_GLA_SEQS, bsz), chunk=PROMPT_GLA_CHUNK)
    hs, st_s = _gla_pipe(x_sample, state_gla[0], *gla_w, seqs=seq_tile, chunk=ds)
    hp = hp.reshape(bsz * t_len, D_MODEL)
    hs = hs.reshape(db * ds, D_MODEL)
    wup = w_up.astype(BF16)
    wdn = w_down.astype(BF16)
    hp = _mlp(hp, row2(norm_mlp[0]), wup[0], wdn[0])
    hs = _mlp(hs, row2(norm_mlp[0]), wup[0], wdn[0])

    cos_p, sin_p = _rope_tables(jnp.arange(t_len, dtype=jnp.int32))
    cos_s, sin_s = _rope_tables(past + jnp.arange(ds, dtype=jnp.int32))
    cos_s, sin_s = jnp.tile(cos_s, (db, 1)), jnp.tile(sin_s, (db, 1))
    kvq_w = (row2(norm_kv), row2(norm_mix[1]), w_kv.astype(BF16), w_q_b[0].astype(BF16), row2(g_k), row2(g_q[0]))
    k_p, q_p, k4_p, v4_p, vt_p = _kvq(hp, *kvq_w, cos_p, sin_p, tm=PROMPT_ROW_TILE, seq_len=t_len)
    k_s, q_s, k4_s, v4_s, v_s = _kvq(hs, *kvq_w, cos_s, sin_s, tm=db * ds)

    a_p = _moba_prompt(q_p.reshape(bsz, t_len, -1), k_p.reshape(bsz, t_len, kdim), vt_p)
    pages = lambda c: c.reshape(c.shape[0], PAGE_SIZE * KV_HEADS, HEAD_DIM)
    a_s = _moba_sample(q_s.reshape(db, ds, -1), k_s.reshape(db, ds, kdim), v_s.reshape(db, ds, kdim),
                       pages(cache_k), pages(cache_v), page_table)
    wob = w_o_b[0].astype(BF16)
    hp = _mlp(hp, row2(norm_mlp[1]), wup[1], wdn[1], attn=a_p.reshape(bsz * t_len, -1), wo=wob)
    hs = _mlp(hs, row2(norm_mlp[1]), wup[1], wdn[1], attn=a_s.reshape(db * ds, -1), wo=wob)

    kv4 = lambda a, n, t: a.reshape(n, t, KV_HEADS, HEAD_DIM)
    return (hp.reshape(bsz, t_len, D_MODEL), hs.reshape(db, ds, D_MODEL), st_p[None], st_s[None],
            kv4(k4_p, bsz, t_len), kv4(v4_p, bsz, t_len), kv4(k4_s, db, ds), kv4(v4_s, db, ds))
```

```python
import functools
import math

import jax
import jax.numpy as jnp
from jax import lax
from jax.experimental import pallas as pl
from jax.experimental.pallas import tpu as pltpu

F32 = jnp.float32
BF16 = jnp.bfloat16

D_MODEL = 1024
D_FF = 4 * D_MODEL
NORM_EPS = 1e-6
GLA_HEADS = 4
GLA_DK = 128
GLA_DV = 256
GLA_KEY_DIM = GLA_HEADS * GLA_DK
GLA_VALUE_DIM = GLA_HEADS * GLA_DV
GATE_RANK = 16
GATE_NORMALIZER = 16.0
HEAD_DIM = 128
Q_HEADS = 8
KV_HEADS = 4
GROUP = Q_HEADS // KV_HEADS
MOBA_BLOCK = 256
MOBA_TOPK = 3
PAGE_SIZE = 128
ROPE_THETA = 10000.0
NEG_INF = -1e30
LANES = 128
SUBLANES = 8

VMEM_LIMIT_BYTES = 56 * 1024 * 1024

PROMPT_ROW_TILE = 256
GLA_MAX_FACTORED_SPAN = 40.0
PROMPT_GLA_CHUNK = 256
PROMPT_GLA_SEQS = 2
MLP_ROW_TILE = 512
MLP_FF_TILE = 1024
MOBA_KV_HEADS_PER_STEP = 4
SAMPLE_SEQ_TILE = 4
SAMPLE_PAGES_PER_STEP = 16
SAMPLE_PAGE_SLOTS = 3


def _rms(x):
    return x * lax.rsqrt(jnp.mean(x * x, axis=-1, keepdims=True) + NORM_EPS)


def _log_sigmoid(x):
    return jnp.minimum(x, 0.0) - jnp.log1p(jnp.exp(-jnp.abs(x)))


def _dot(a, b):
    return jnp.dot(a, b, preferred_element_type=F32)


def _dot_nt(a, b, precision=None):
    return lax.dot_general(a, b, (((1,), (1,)), ((), ())), precision=precision,
                           preferred_element_type=F32)


def _dot_tn(a, b):
    return lax.dot_general(a, b, (((0,), (0,)), ((), ())), preferred_element_type=F32)


def _const_spec(shape):
    n = len(shape)
    return pl.BlockSpec(shape, lambda *_: (0,) * n)


def _gla_pipe_kernel(x_ref, s0_ref, gmix_ref, wmain_ref, wglr_ref, wg2_ref, bg_ref, gout_ref, wo_ref,
                     h_ref, sout_ref, proj_ref, gk_ref, xk_ref, b_ref, oi_ref, o_ref, s_ref, *,
                     seqs, chunk, tiles_per_seq):
    i = pl.program_id(0)
    rows = seqs * chunk
    slot = lax.rem(i, 2)
    prev = 1 - slot
    tile_b = jnp.maximum(i - 1, 0)
    t_in_seq = lax.rem(tile_b, tiles_per_seq)
    kq = 2 * GLA_KEY_DIM
    k_off, v_off, r_off = GLA_KEY_DIM, 2 * GLA_KEY_DIM, 2 * GLA_KEY_DIM + GLA_VALUE_DIM

    @pl.when(i == 0)
    def _():
        proj_ref[1] = jnp.zeros(proj_ref.shape[1:], F32)
        gk_ref[1] = jnp.zeros(gk_ref.shape[1:], F32)
        xk_ref[1] = jnp.zeros(xk_ref.shape[1:], F32)

    @pl.when(t_in_seq == 0)
    def _():
        s_ref[...] = s0_ref[...]

    r_i = lax.broadcasted_iota(jnp.int32, (chunk, chunk), 0)
    c_i = lax.broadcasted_iota(jnp.int32, (chunk, chunk), 1)
    causal = c_i <= r_i
    mm = BF16 if chunk % 16 == 0 else F32
    tri = jnp.where(causal, 1.0, 0.0).astype(mm)
    n_sub = 2 if chunk % 32 == 0 else 1
    sub = chunk // n_sub
    half = max(sub // 2, 1)
    same_sub = causal if n_sub == 1 else jnp.logical_and(causal, (r_i >= sub) == (c_i >= sub))
    gout = gout_ref[...]
    units = [(sq, h) for sq in range(seqs) for h in range(GLA_HEADS)]
    rs = [slice(sq * chunk, (sq + 1) * chunk) for sq, _ in units]
    kd = [slice(h * GLA_DK, (h + 1) * GLA_DK) for _, h in units]
    vd = [slice(h * GLA_DV, (h + 1) * GLA_DV) for _, h in units]
    n_units = range(len(units))

    bcum = []
    for sq in range(seqs):
        g = gk_ref[prev, sq * chunk:(sq + 1) * chunk, :]
        g_hi = g.astype(BF16).astype(F32)
        bcum.append(_dot(tri, g_hi.astype(mm)) + _dot(tri, (g - g_hi).astype(mm)))

    b = [bcum[sq][:, kd[u]] for u, (sq, _) in enumerate(units)]
    b_last = [b[u][chunk - 1:chunk] for u in n_units]
    q = [proj_ref[prev, rs[u], kd[u]] * (GLA_DK ** -0.5) for u in n_units]
    k = [proj_ref[prev, rs[u], k_off + h * GLA_DK:k_off + (h + 1) * GLA_DK] for u, (_, h) in enumerate(units)]
    v = [proj_ref[prev, rs[u], v_off + h * GLA_DV:v_off + (h + 1) * GLA_DV].astype(mm)
         for u, (_, h) in enumerate(units)]
    s_old = [s_ref[sq, h] for sq, h in units]
    q_e = [(q[u] * jnp.exp(b[u])).astype(BF16) for u in n_units]
    k_e = [(k[u] * jnp.exp(b_last[u] - b[u])).astype(mm) for u in n_units]
    q_d, k_d = [], []
    for u in n_units:
        mids = jnp.concatenate(
            [jnp.broadcast_to(b[u][j * sub + half - 1:j * sub + half], (sub, GLA_DK)) for j in range(n_sub)],
            axis=0)
        q_d.append((q[u] * jnp.exp(b[u] - mids)).astype(mm))
        k_d.append((k[u] * jnp.exp(mids - b[u])).astype(mm))

    x = x_ref[...].reshape(rows, D_MODEL)
    xn = (_rms(x) * gmix_ref[...]).astype(BF16)
    proj_ref[slot, :, :kq] = _dot(xn, wmain_ref[:, :kq])
    glr = _dot(xn, wglr_ref[...])
    gpre = _dot(glr.astype(BF16), wg2_ref[...]) + bg_ref[...]
    gk_ref[slot] = _log_sigmoid(gpre) * (1.0 / GATE_NORMALIZER)

    o_inter = [_dot(q_e[u], s_old[u].astype(BF16)) for u in n_units]
    kv = [_dot_tn(k_e[u], v[u]) for u in n_units]
    for u, (sq, h) in enumerate(units):
        decay_col = jnp.transpose(jnp.broadcast_to(jnp.exp(b_last[u]), (GLA_DK, GLA_DK)))
        s_ref[sq, h] = s_old[u] * jnp.concatenate([decay_col, decay_col], axis=1) + kv[u]
    proj_ref[slot, :, kq:r_off] = _dot(xn, wmain_ref[:, kq:r_off])

    a = []
    for u in n_units:
        a.append(jnp.where(same_sub, _dot_nt(q_d[u], k_d[u]), 0.0).astype(mm))
    intra = [_dot(a[u], v[u]) for u in n_units]
    if n_sub == 2:
        a_x = []
        for u in n_units:
            edge = b[u][sub - 1:sub]
            q_x = q[u][sub:] * jnp.exp(b[u][sub:] - edge)
            k_x = k[u][:sub] * jnp.exp(edge - b[u][:sub])
            a_x.append(_dot_nt(q_x.astype(mm), k_x.astype(mm)).astype(mm))
        for u in n_units:
            cross = _dot(a_x[u], v[u][:sub])
            intra[u] = jnp.concatenate([intra[u][:sub], intra[u][sub:] + cross], axis=0)

    proj_ref[slot, :, r_off:] = _dot(xn, wmain_ref[:, r_off:])
    xk_ref[slot] = x

    def gated_norm(o, u):
        h = units[u][1]
        r = proj_ref[prev, rs[u], r_off + h * GLA_DV:r_off + (h + 1) * GLA_DV]
        return _rms(o) * gout * (r * jax.nn.sigmoid(r))

    for sq in range(seqs):
        b_ref[sq * chunk:(sq + 1) * chunk, :] = bcum[sq]
    for u in n_units:
        oi_ref[rs[u], vd[u]] = o_inter[u]
        o_ref[rs[u], vd[u]] = gated_norm(o_inter[u] + intra[u], u)

    def finish():
        y = _dot(o_ref[...].astype(BF16), wo_ref[...])
        h_ref[...] = (xk_ref[prev] + y).reshape(h_ref.shape)

    finish()

    span = jnp.zeros((1, GLA_DK), F32)
    for u in n_units:
        before = jnp.zeros((1, GLA_DK), F32)
        for j in range(n_sub):
            end = b[u][(j + 1) * sub - 1:(j + 1) * sub]
            span = jnp.maximum(span, before - end)
            before = end

    @pl.when(jnp.max(span) > GLA_MAX_FACTORED_SPAN)
    def _():
        col = lax.broadcasted_iota(jnp.int32, (chunk, GLA_DK), 0)

        def row_group(gi, c):
            for u, (sq, h) in enumerate(units):
                grp = pl.ds(pl.multiple_of(sq * chunk + gi * SUBLANES, SUBLANES), SUBLANES)
                b_g = b_ref[grp, kd[u]]
                q_g = proj_ref[prev, grp, kd[u]] * (GLA_DK ** -0.5)
                b_u = b_ref[rs[u], kd[u]]
                k_u = proj_ref[prev, rs[u], k_off + h * GLA_DK:k_off + (h + 1) * GLA_DK]
                v_u = proj_ref[prev, rs[u], v_off + h * GLA_DV:v_off + (h + 1) * GLA_DV]
                out_rows = []
                for r in range(SUBLANES):
                    live = col <= gi * SUBLANES + r
                    decay = jnp.where(live, jnp.exp(jnp.minimum(b_g[r:r + 1] - b_u, 0.0)), 0.0)
                    a_col = jnp.sum(decay * k_u * q_g[r:r + 1], axis=-1, keepdims=True)
                    out_rows.append(jnp.sum(a_col * v_u, axis=0, keepdims=True))
                oi_ref[grp, vd[u]] += jnp.concatenate(out_rows, axis=0)
            return c

        lax.fori_loop(0, chunk // SUBLANES, row_group, 0)
        for u in n_units:
            o_ref[rs[u], vd[u]] = gated_norm(oi_ref[rs[u], vd[u]], u)
        finish()

    @pl.when(t_in_seq == tiles_per_seq - 1)
    def _():
        sout_ref[...] = s_ref[...]


def _gla_pipe(x, s0, gmix, wmain, wglr, wg2, bg, gout, wo, *, seqs, chunk):
    nseq, t_len, _ = x.shape
    assert nseq % seqs == 0 and t_len % chunk == 0 and chunk % SUBLANES == 0
    tiles_per_seq = t_len // chunk
    n_tiles = (nseq // seqs) * tiles_per_seq
    rows = seqs * chunk
    state_block = (seqs, GLA_HEADS, GLA_DK, GLA_DV)
    proj_tile = lambda i: jnp.minimum(i, n_tiles - 1)
    rec_tile = lambda i: jnp.maximum(i - 1, 0)
    x_spec = lambda tile: pl.BlockSpec(
        (seqs, chunk, D_MODEL), lambda i: (tile(i) // tiles_per_seq, tile(i) % tiles_per_seq, 0))
    s_spec = pl.BlockSpec(state_block, lambda i: (rec_tile(i) // tiles_per_seq, 0, 0, 0))
    kern = functools.partial(_gla_pipe_kernel, seqs=seqs, chunk=chunk, tiles_per_seq=tiles_per_seq)
    resident = lambda w: pl.BlockSpec(w.shape, lambda i: (0, 0), pipeline_mode=pl.Buffered(1))
    return pl.pallas_call(
        kern,
        grid=(n_tiles + 1,),
        in_specs=[
            x_spec(proj_tile), s_spec,
            _const_spec(gmix.shape), resident(wmain), resident(wglr),
            resident(wg2), _const_spec(bg.shape), _const_spec(gout.shape), resident(wo),
        ],
        out_specs=[x_spec(rec_tile), s_spec],
        out_shape=[jax.ShapeDtypeStruct(x.shape, F32), jax.ShapeDtypeStruct(s0.shape, F32)],
        scratch_shapes=[
            pltpu.VMEM((2, rows, 2 * GLA_KEY_DIM + 2 * GLA_VALUE_DIM), F32),
            pltpu.VMEM((2, rows, GLA_KEY_DIM), F32),
            pltpu.VMEM((2, rows, D_MODEL), F32),
            pltpu.VMEM((rows, GLA_KEY_DIM), F32),
            pltpu.VMEM((rows, GLA_VALUE_DIM), F32),
            pltpu.VMEM((rows, GLA_VALUE_DIM), F32),
            pltpu.VMEM(state_block, F32),
        ],
        compiler_params=pltpu.CompilerParams(
            dimension_semantics=("arbitrary",), vmem_limit_bytes=VMEM_LIMIT_BYTES),
        name="gla_layer",
    )(x, s0, gmix, wmain, wglr, wg2, bg, gout, wo)


def _mlp_kernel(*refs, has_attn):
    if has_attn:
        h_ref, a_ref, wo_ref, g_ref, wup_ref, wdn_ref, out_ref, act_ref = refs
        h = h_ref[...] + _dot(a_ref[...].astype(BF16), wo_ref[...])
    else:
        h_ref, g_ref, wup_ref, wdn_ref, out_ref, act_ref = refs
        h = h_ref[...]
    xn = (_rms(h) * g_ref[...]).astype(BF16)
    for j in range(D_FF // MLP_FF_TILE):
        ff = slice(j * MLP_FF_TILE, (j + 1) * MLP_FF_TILE)
        a = jnp.maximum(_dot(xn, wup_ref[:, ff]), 0.0)
        act_ref[:, ff] = (a * a).astype(BF16)
    out_ref[...] = h + _dot(act_ref[...], wdn_ref[...])


def _mlp(h, g, wup, wdn, attn=None, wo=None):
    n = h.shape[0]
    tm = min(MLP_ROW_TILE, n)
    row_spec = pl.BlockSpec((tm, D_MODEL), lambda i: (i, 0))
    single = pl.Buffered(1)

    def wspec(w):
        return pl.BlockSpec(w.shape, lambda i: (0, 0), pipeline_mode=single)

    if attn is None:
        args = (h, g, wup, wdn)
        in_specs = [row_spec, _const_spec(g.shape), wspec(wup), wspec(wdn)]
    else:
        args = (h, attn, wo, g, wup, wdn)
        in_specs = [row_spec, row_spec, wspec(wo), _const_spec(g.shape), wspec(wup), wspec(wdn)]
    return pl.pallas_call(
        functools.partial(_mlp_kernel, has_attn=attn is not None),
        grid=(n // tm,),
        in_specs=in_specs,
        out_specs=row_spec,
        out_shape=jax.ShapeDtypeStruct(h.shape, F32),
        scratch_shapes=[pltpu.VMEM((tm, D_FF), BF16)],
        compiler_params=pltpu.CompilerParams(
            dimension_semantics=("arbitrary",), vmem_limit_bytes=VMEM_LIMIT_BYTES),
        name="mlp",
    )(*args)


def _rope(x, cos, sin_signed):
    return x * cos + pltpu.roll(x, HEAD_DIM // 2, 1) * sin_signed


def _kvq_kernel(h_ref, nkv_ref, nq_ref, wkv_ref, wq_ref, gk_ref, gq_ref, cos_ref, sin_ref,
                k_ref, q_ref, k4_ref, v4_ref, v_ref, *, transposed_v):
    tm = h_ref.shape[0]
    xh = _rms(h_ref[...])
    kv = _dot((xh * nkv_ref[...]).astype(BF16), wkv_ref[...])
    qq = _dot((xh * nq_ref[...]).astype(BF16), wq_ref[...])
    cos = cos_ref[...]
    sin = sin_ref[...]
    kdim = KV_HEADS * HEAD_DIM
    for hh in range(KV_HEADS):
        hd = slice(hh * HEAD_DIM, (hh + 1) * HEAD_DIM)
        kh = _rope(_rms(kv[:, hd]) * gk_ref[...], cos, sin)
        vh = kv[:, kdim + hh * HEAD_DIM:kdim + (hh + 1) * HEAD_DIM]
        k_ref[:, hd] = kh
        k4_ref[pl.ds(hh, tm, stride=KV_HEADS), :] = kh
        v4_ref[pl.ds(hh, tm, stride=KV_HEADS), :] = vh
        if transposed_v:
            v_ref[0, hh] = jnp.transpose(vh).astype(BF16)
    if not transposed_v:
        v_ref[...] = kv[:, kdim:]
    for hh in range(Q_HEADS):
        hd = slice(hh * HEAD_DIM, (hh + 1) * HEAD_DIM)
        q_ref[:, hd] = _rope(_rms(qq[:, hd]) * gq_ref[...], cos, sin)


def _kvq(h, nkv, nq, wkv, wq, gk, gq, cos, sin, *, tm, seq_len=None):
    n = h.shape[0]
    pos_tiles = cos.shape[0] // tm
    kdim = KV_HEADS * HEAD_DIM
    row = lambda w: pl.BlockSpec((tm, w), lambda i: (i, 0))
    row4 = pl.BlockSpec((tm * KV_HEADS, HEAD_DIM), lambda i: (i, 0))
    tab = pl.BlockSpec((tm, HEAD_DIM), lambda i: (i % pos_tiles, 0))
    out_specs = [row(kdim), row(Q_HEADS * HEAD_DIM), row4, row4]
    out_shape = [jax.ShapeDtypeStruct((n, kdim), F32), jax.ShapeDtypeStruct((n, Q_HEADS * HEAD_DIM), F32),
                 jax.ShapeDtypeStruct((n * KV_HEADS, HEAD_DIM), F32),
                 jax.ShapeDtypeStruct((n * KV_HEADS, HEAD_DIM), F32)]
    if seq_len is not None:
        assert seq_len % tm == 0 and tm % LANES == 0
        tiles = seq_len // tm
        out_specs.append(pl.BlockSpec((1, KV_HEADS, HEAD_DIM, tm), lambda i: (i // tiles, 0, 0, i % tiles)))
        out_shape.append(jax.ShapeDtypeStruct((n // seq_len, KV_HEADS, HEAD_DIM, seq_len), BF16))
    else:
        out_specs.append(row(kdim))
        out_shape.append(jax.ShapeDtypeStruct((n, kdim), F32))
    return pl.pallas_call(
        functools.partial(_kvq_kernel, transposed_v=seq_len is not None),
        grid=(n // tm,),
        in_specs=[row(D_MODEL), _const_spec(nkv.shape), _const_spec(nq.shape), _const_spec(wkv.shape),
                  _const_spec(wq.shape), _const_spec(gk.shape), _const_spec(gq.shape), tab, tab],
        out_specs=out_specs,
        out_shape=out_shape,
        compiler_params=pltpu.CompilerParams(
            dimension_semantics=("arbitrary",), vmem_limit_bytes=VMEM_LIMIT_BYTES),
        name="kvq",
    )(h, nkv, nq, wkv, wq, gk, gq, cos, sin)


def _select_topk_rows(gate, cand, ksel):
    n = gate.shape[0]
    row = lax.broadcasted_iota(jnp.int32, gate.shape, 0).astype(F32)
    gg = jnp.where(cand, gate, -jnp.inf)
    sel = jnp.zeros(gate.shape, F32)
    for _ in range(ksel):
        mx = jnp.max(gg, axis=0, keepdims=True)
        first = jnp.min(jnp.where(gg == mx, row, float(n)), axis=0, keepdims=True)
        hit = row == first
        sel = jnp.where(hit, 1.0, sel)
        gg = jnp.where(hit, -jnp.inf, gg)
    return jnp.where(cand, sel, 0.0)


def _split_bf16(x):
    hi = x.astype(BF16)
    return hi, (x - hi.astype(F32)).astype(BF16)


def _moba_prompt_kernel(q_ref, k_ref, vt_ref, o_ref, means_ref, s_ref, p_ref, *, nblk):
    qi = pl.program_id(2)
    sub = SUBLANES
    kv_heads = k_ref.shape[2] // HEAD_DIM
    streams = [(kv, kv * GROUP + g) for kv in range(kv_heads) for g in range(GROUP)]

    @pl.when(qi == 0)
    def _():
        means_ref[...] = jnp.zeros(means_ref.shape, F32)
        for kv in range(kv_heads):
            for n in range(nblk):
                blk = k_ref[0, n * MOBA_BLOCK:(n + 1) * MOBA_BLOCK, kv * HEAD_DIM:(kv + 1) * HEAD_DIM]
                means_ref[kv, n:n + 1, :] = jnp.sum(blk, axis=0, keepdims=True) * (1.0 / MOBA_BLOCK)

    def attend(nb):
        qb = q_ref[0]
        heads = [qb[:, hq * HEAD_DIM:(hq + 1) * HEAD_DIM] for _, hq in streams]
        qs = [(h * (HEAD_DIM ** -0.5 * math.log2(math.e))).astype(BF16) for h in heads]
        causal = (lax.broadcasted_iota(jnp.int32, (MOBA_BLOCK, MOBA_BLOCK), 0)
                  <= lax.broadcasted_iota(jnp.int32, (MOBA_BLOCK, MOBA_BLOCK), 1))
        ksel = min(MOBA_TOPK, nblk)
        keys = (nb + 1) * MOBA_BLOCK
        k_all = [k_ref[0, :keys, kv * HEAD_DIM:(kv + 1) * HEAD_DIM].astype(BF16) for kv in range(kv_heads)]

        sel = [None] * len(streams)
        if nb > ksel:
            row = lax.broadcasted_iota(jnp.int32, (means_ref.shape[1], MOBA_BLOCK), 0)
            for st, (kv, _) in enumerate(streams):
                m_hi, m_lo = _split_bf16(means_ref[kv])
                q_hi, q_lo = _split_bf16(heads[st])
                gate = _dot_nt(m_hi, q_hi) + _dot_nt(m_hi, q_lo) + _dot_nt(m_lo, q_hi)
                sel[st] = _select_topk_rows(gate, row < nb, ksel)
        m_parts = []
        for st, (kv, _) in enumerate(streams):
            m_part = None
            s_all = _dot_nt(k_all[kv], qs[st])
            for n in range(nb + 1):
                s = s_all[n * MOBA_BLOCK:(n + 1) * MOBA_BLOCK]
                if n == nb:
                    s = jnp.where(causal, s, NEG_INF)
                elif sel[st] is not None:
                    s = jnp.where(sel[st][n:n + 1, :] > 0.0, s, NEG_INF)
                s_ref[st, n] = s
                part = jnp.max(s.reshape(MOBA_BLOCK // sub, sub, MOBA_BLOCK), axis=0)
                m_part = part if m_part is None else jnp.maximum(m_part, part)
            m_parts.append(m_part)
        for st, (kv, hq) in enumerate(streams):
            m = jnp.max(m_parts[st], axis=0, keepdims=True)
            l_part = jnp.zeros((sub, MOBA_BLOCK), F32)
            for n in range(nb + 1):
                p = jnp.exp2(s_ref[st, n] - m)
                l_part = l_part + jnp.sum(p.reshape(MOBA_BLOCK // sub, sub, MOBA_BLOCK), axis=0)
                p_ref[st, n * MOBA_BLOCK:(n + 1) * MOBA_BLOCK, :] = p.astype(BF16)
            acc = _dot(vt_ref[0, kv, :, :keys], p_ref[st, :keys, :])
            out = acc * (1.0 / jnp.sum(l_part, axis=0, keepdims=True))
            o_ref[0, :, hq * HEAD_DIM:(hq + 1) * HEAD_DIM] = jnp.transpose(out)

    for nb in range(nblk):
        pl.when(qi == nb)(functools.partial(attend, nb))


def _moba_prompt(q, k, vt):
    bsz, t_len, _ = q.shape
    nblk = t_len // MOBA_BLOCK
    nblk_pad = -(-nblk // SUBLANES) * SUBLANES
    kvs = math.gcd(MOBA_KV_HEADS_PER_STEP, KV_HEADS)
    n_streams = kvs * GROUP
    return pl.pallas_call(
        functools.partial(_moba_prompt_kernel, nblk=nblk),
        grid=(bsz, KV_HEADS // kvs, nblk),
        in_specs=[
            pl.BlockSpec((1, MOBA_BLOCK, n_streams * HEAD_DIM), lambda b, h, i: (b, i, h)),
            pl.BlockSpec((1, t_len, kvs * HEAD_DIM), lambda b, h, i: (b, 0, h)),
            pl.BlockSpec((1, kvs, HEAD_DIM, t_len), lambda b, h, i: (b, h, 0, 0)),
        ],
        out_specs=pl.BlockSpec((1, MOBA_BLOCK, n_streams * HEAD_DIM), lambda b, h, i: (b, i, h)),
        out_shape=jax.ShapeDtypeStruct(q.shape, F32),
        scratch_shapes=[
            pltpu.VMEM((kvs, nblk_pad, HEAD_DIM), F32),
            pltpu.VMEM((n_streams, nblk, MOBA_BLOCK, MOBA_BLOCK), F32),
            pltpu.VMEM((n_streams, t_len, MOBA_BLOCK), BF16),
        ],
        compiler_params=pltpu.CompilerParams(
            dimension_semantics=("arbitrary", "arbitrary", "arbitrary"), vmem_limit_bytes=VMEM_LIMIT_BYTES),
        name="moba_prompt",
    )(q, k, vt)


def _moba_sample_kernel(pt_ref, q_ref, knew_ref, vnew_ref, ck_ref, cv_ref, o_ref,
                        kbuf, vbuf, sem, gate_ref, m_ref, l_ref, acc_ref, *, pages_per_step, n_blocks, n_new):
    j = pl.program_id(1)
    steps_per_seq = pl.num_programs(1)
    step = pl.program_id(0) * steps_per_seq + j
    n_steps = pl.num_programs(0) * steps_per_seq
    ahead = SAMPLE_PAGE_SLOTS - 1

    def page_copies(s, slot):
        out = []
        for i in range(pages_per_step):
            page = pt_ref[s * pages_per_step + i]
            out.append(pltpu.make_async_copy(ck_ref.at[page], kbuf.at[slot, i], sem.at[0, slot]))
            out.append(pltpu.make_async_copy(cv_ref.at[page], vbuf.at[slot, i], sem.at[1, slot]))
        return out

    @pl.when(step == 0)
    def _():
        for s in range(ahead):
            for c in page_copies(s, s):
                c.start()

    @pl.when(step + ahead < n_steps)
    def _():
        for c in page_copies(step + ahead, lax.rem(step + ahead, SAMPLE_PAGE_SLOTS)):
            c.start()

    slot = lax.rem(step, SAMPLE_PAGE_SLOTS)
    for c in page_copies(step, slot):
        c.wait()

    pages_per_block = MOBA_BLOCK // PAGE_SIZE
    blocks_per_step = pages_per_step // pages_per_block
    page_rows = PAGE_SIZE * KV_HEADS
    nkv = GROUP * n_new
    nrow = KV_HEADS * nkv
    lane = lax.broadcasted_iota(jnp.int32, (nrow, LANES), 1)
    scale = HEAD_DIM ** -0.5

    @pl.when(j == 0)
    def _():
        gate_ref[...] = jnp.zeros(gate_ref.shape, F32)
        m_ref[...] = jnp.zeros(m_ref.shape, F32)
        l_ref[...] = jnp.zeros(l_ref.shape, F32)

    qf = jnp.concatenate([q_ref[0, :, hq * HEAD_DIM:(hq + 1) * HEAD_DIM] for hq in range(Q_HEADS)], axis=0)
    qs = (qf * scale).astype(BF16)
    blk_cols = pages_per_block * page_rows
    col_head = lax.broadcasted_iota(jnp.int32, (nrow, blk_cols), 1) & (KV_HEADS - 1)
    row_head = lax.div(lax.broadcasted_iota(jnp.int32, (nrow, blk_cols), 0), nkv)
    same_head = col_head == row_head
    sub = SUBLANES

    blocks = range(blocks_per_step)
    scores, gcols = [], []
    for bi in blocks:
        kpages = [kbuf[slot, bi * pages_per_block + pp] for pp in range(pages_per_block)]
        scores.append(jnp.concatenate([_dot_nt(qs, kp.astype(BF16)) for kp in kpages], axis=1))
        ksum = sum(jnp.sum(kp.reshape(page_rows // sub, sub, HEAD_DIM), axis=0) for kp in kpages)
        mean = sum(ksum[i * KV_HEADS:(i + 1) * KV_HEADS] for i in range(sub // KV_HEADS)) * (1.0 / MOBA_BLOCK)
        mean_rows = jnp.concatenate(
            [jnp.broadcast_to(mean[h:h + 1], (nkv, HEAD_DIM)) for h in range(KV_HEADS)], axis=0)
        gcols.append(jnp.sum(qf * mean_rows, axis=-1, keepdims=True))
    probs, maxes, sums = [], [], []
    for bi in blocks:
        s = jnp.where(same_head, scores[bi], NEG_INF)
        mb = jnp.max(s, axis=-1, keepdims=True)
        p = jnp.exp(s - mb)
        maxes.append(mb)
        sums.append(jnp.sum(p, axis=-1, keepdims=True))
        probs.append(p.astype(BF16))
    gate, m_all, l_all = gate_ref[...], m_ref[...], l_ref[...]
    for bi in blocks:
        n = j * blocks_per_step + bi
        acc = _dot(probs[bi][:, :page_rows], vbuf[slot, bi * pages_per_block].astype(BF16))
        for pp in range(1, pages_per_block):
            acc = acc + _dot(probs[bi][:, pp * page_rows:(pp + 1) * page_rows],
                             vbuf[slot, bi * pages_per_block + pp].astype(BF16))
        acc_ref[n] = acc
        hot = lane == n
        gate = jnp.where(hot, gcols[bi], gate)
        m_all = jnp.where(hot, maxes[bi], m_all)
        l_all = jnp.where(hot, sums[bi], l_all)
    gate_ref[...] = gate
    m_ref[...] = m_all
    l_ref[...] = l_all

    @pl.when(j == pl.num_programs(1) - 1)
    def _():
        lane = lax.broadcasted_iota(jnp.int32, (nkv, LANES), 1)
        qrow_t = lax.rem(lax.broadcasted_iota(jnp.int32, (nkv, LANES), 0), n_new)
        lane_f = lane.astype(F32)
        for h in range(KV_HEADS):
            hd = slice(h * HEAD_DIM, (h + 1) * HEAD_DIM)
            rs = slice(h * nkv, (h + 1) * nkv)
            qsc = qf[rs] * scale
            gg = jnp.where(lane < n_blocks, gate_ref[rs, :], -jnp.inf)
            picked = jnp.zeros((nkv, LANES), F32)
            for _ in range(min(MOBA_TOPK, n_blocks)):
                mx = jnp.max(gg, axis=-1, keepdims=True)
                first = jnp.min(jnp.where(gg == mx, lane_f, float(LANES)), axis=-1, keepdims=True)
                hit = lane_f == first
                picked = jnp.where(hit, 1.0, picked)
                gg = jnp.where(hit, -jnp.inf, gg)
            sel = jnp.where(lane < n_blocks, picked, 0.0) > 0.0
            knew = knew_ref[0, :, hd]
            vnew = vnew_ref[0, :, hd]
            s_own = jnp.full((nkv, LANES), NEG_INF, F32)
            for jj in range(n_new):
                col = jnp.sum(qsc * knew[jj:jj + 1, :], axis=-1, keepdims=True)
                s_own = jnp.where(jnp.logical_and(lane == jj, qrow_t >= jj), col, s_own)
            mh = m_ref[rs, :]
            m_tot = jnp.maximum(jnp.max(s_own, axis=-1, keepdims=True),
                                jnp.max(jnp.where(sel, mh, NEG_INF), axis=-1, keepdims=True))
            w = jnp.where(sel, jnp.exp(mh - m_tot), 0.0)
            p_own = jnp.exp(s_own - m_tot)
            den = jnp.sum(w * l_ref[rs, :], axis=-1, keepdims=True) + jnp.sum(p_own, axis=-1, keepdims=True)
            num = jnp.zeros((nkv, HEAD_DIM), F32)
            for jj in range(n_new):
                num = num + p_own[:, jj:jj + 1] * vnew[jj:jj + 1, :]
            for n in range(n_blocks):
                num = num + w[:, n:n + 1] * acc_ref[n, rs, :]
            out = num / den
            for g in range(GROUP):
                o_ref[0, :, (h * GROUP + g) * HEAD_DIM:(h * GROUP + g + 1) * HEAD_DIM] = (
                    out[g * n_new:(g + 1) * n_new, :])


def _moba_sample(q, knew, vnew, cache_k, cache_v, page_table):
    db, ds, _ = q.shape
    n_pages = page_table.shape[1]
    pages_per_block = MOBA_BLOCK // PAGE_SIZE
    assert n_pages % pages_per_block == 0, "past length must be a whole number of MoBA blocks"
    n_blocks = n_pages // pages_per_block
    assert 0 < n_blocks <= LANES
    assert KV_HEADS & (KV_HEADS - 1) == 0 and 8 % KV_HEADS == 0
    pps = math.gcd(SAMPLE_PAGES_PER_STEP, n_pages)
    assert pps % pages_per_block == 0
    nrow = Q_HEADS * ds
    kdim = KV_HEADS * HEAD_DIM
    assert db * (n_pages // pps) >= SAMPLE_PAGE_SLOTS - 1
    page_buf = pltpu.VMEM((SAMPLE_PAGE_SLOTS, pps, PAGE_SIZE * KV_HEADS, HEAD_DIM), F32)

    per_seq = lambda w: pl.BlockSpec((1, ds, w), lambda b, j, pt: (b, 0, 0))
    hbm = pl.BlockSpec(memory_space=pl.ANY)
    grid_spec = pltpu.PrefetchScalarGridSpec(
        num_scalar_prefetch=1,
        grid=(db, n_pages // pps),
        in_specs=[per_seq(Q_HEADS * HEAD_DIM), per_seq(kdim), per_seq(kdim), hbm, hbm],
        out_specs=per_seq(Q_HEADS * HEAD_DIM),
        scratch_shapes=[
            page_buf, page_buf, pltpu.SemaphoreType.DMA((2, SAMPLE_PAGE_SLOTS)),
            pltpu.VMEM((nrow, LANES), F32),
            pltpu.VMEM((nrow, LANES), F32),
            pltpu.VMEM((nrow, LANES), F32),
            pltpu.VMEM((n_blocks, nrow, HEAD_DIM), F32),
        ],
    )
    return pl.pallas_call(
        functools.partial(_moba_sample_kernel, pages_per_step=pps, n_blocks=n_blocks, n_new=ds),
        grid_spec=grid_spec,
        out_shape=jax.ShapeDtypeStruct(q.shape, F32),
        compiler_params=pltpu.CompilerParams(
            dimension_semantics=("arbitrary", "arbitrary"), vmem_limit_bytes=VMEM_LIMIT_BYTES),
        name="moba_sample",
    )(page_table.reshape(-1), q, knew, vnew, cache_k, cache_v)


def _rope_tables(pos):
    inv = ROPE_THETA ** (-jnp.arange(0, HEAD_DIM, 2, dtype=F32) / HEAD_DIM)
    ang = pos.astype(F32)[:, None] * inv[None, :]
    cos, sin = jnp.cos(ang), jnp.sin(ang)
    return jnp.concatenate([cos, cos], axis=-1), jnp.concatenate([-sin, sin], axis=-1)


def kernel(x_prompt, x_sample, state_gla, cache_k, cache_v, page_table, w_in_a, w_g2, b_g, g_gla_out, w_o_a,
           norm_kv, w_kv, g_k, w_q_b, g_q, w_o_b, norm_mix, norm_mlp, w_up, w_down):
    bsz, t_len, _ = x_prompt.shape
    db, ds, _ = x_sample.shape
    n_a = state_gla.shape[0]
    depth = norm_mix.shape[0]
    assert n_a == 1 and depth == 2, "one GLA layer followed by one MoBA layer"
    assert t_len % PROMPT_ROW_TILE == 0 and t_len % PROMPT_GLA_CHUNK == 0 and t_len % MOBA_BLOCK == 0
    past = page_table.shape[1] * PAGE_SIZE
    kdim = KV_HEADS * HEAD_DIM
    row2 = lambda a: a.reshape(1, -1)

    main = 2 * GLA_KEY_DIM + 2 * GLA_VALUE_DIM
    wmain = w_in_a[0, :, :main].astype(BF16)
    wglr = jnp.pad(w_in_a[0, :, main:], ((0, 0), (0, LANES - GATE_RANK))).astype(BF16)
    wg2 = jnp.pad(w_g2[0], ((0, LANES - GATE_RANK), (0, 0))).astype(BF16)
    gla_w = (row2(norm_mix[0]), wmain, wglr, wg2, row2(b_g[0]), row2(g_gla_out[0]), w_o_a[0].astype(BF16))
    s0_p = jnp.zeros((bsz,) + state_gla.shape[2:], F32)
    seq_tile = math.gcd(SAMPLE_SEQ_TILE, db)
    hp, st_p = _gla_pipe(x_prompt, s0_p, *gla_w, seqs=math.gcd(PROMPT_GLA_SEQS, bsz), chunk=PROMPT_GLA_CHUNK)
    hs, st_s = _gla_pipe(x_sample, state_gla[0], *gla_w, seqs=seq_tile, chunk=ds)
    hp = hp.reshape(bsz * t_len, D_MODEL)
    hs = hs.reshape(db * ds, D_MODEL)
    wup = w_up.astype(BF16)
    wdn = w_down.astype(BF16)
    hp = _mlp(hp, row2(norm_mlp[0]), wup[0], wdn[0])
    hs = _mlp(hs, row2(norm_mlp[0]), wup[0], wdn[0])

    cos_p, sin_p = _rope_tables(jnp.arange(t_len, dtype=jnp.int32))
    cos_s, sin_s = _rope_tables(past + jnp.arange(ds, dtype=jnp.int32))
    cos_s, sin_s = jnp.tile(cos_s, (db, 1)), jnp.tile(sin_s, (db, 1))
    kvq_w = (row2(norm_kv), row2(norm_mix[1]), w_kv.astype(BF16), w_q_b[0].astype(BF16), row2(g_k), row2(g_q[0]))
    k_p, q_p, k4_p, v4_p, vt_p = _kvq(hp, *kvq_w, cos_p, sin_p, tm=PROMPT_ROW_TILE, seq_len=t_len)
    k_s, q_s, k4_s, v4_s, v_s = _kvq(hs, *kvq_w, cos_s, sin_s, tm=db * ds)

    a_p = _moba_prompt(q_p.reshape(bsz, t_len, -1), k_p.reshape(bsz, t_len, kdim), vt_p)
    pages = lambda c: c.reshape(c.shape[0], PAGE_SIZE * KV_HEADS, HEAD_DIM)
    a_s = _moba_sample(q_s.reshape(db, ds, -1), k_s.reshape(db, ds, kdim), v_s.reshape(db, ds, kdim),
                       pages(cache_k), pages(cache_v), page_table)
    wob = w_o_b[0].astype(BF16)
    hp = _mlp(hp, row2(norm_mlp[1]), wup[1], wdn[1], attn=a_p.reshape(bsz * t_len, -1), wo=wob)
    hs = _mlp(hs, row2(norm_mlp[1]), wup[1], wdn[1], attn=a_s.reshape(db * ds, -1), wo=wob)

    kv4 = lambda a, n, t: a.reshape(n, t, KV_HEADS, HEAD_DIM)
    return (hp.reshape(bsz, t_len, D_MODEL), hs.reshape(db, ds, D_MODEL), st_p[None], st_s[None],
            kv4(k4_p, bsz, t_len), kv4(v4_p, bsz, t_len), kv4(k4_s, db, ds), kv4(v4_s, db, ds))
```

```python
import functools
import math

import jax
import jax.numpy as jnp
from jax import lax
from jax.experimental import pallas as pl
from jax.experimental.pallas import tpu as pltpu

F32 = jnp.float32
BF16 = jnp.bfloat16

D_MODEL = 1024
D_FF = 4 * D_MODEL
NORM_EPS = 1e-6
GLA_HEADS = 4
GLA_DK = 128
GLA_DV = 256
GLA_KEY_DIM = GLA_HEADS * GLA_DK
GLA_VALUE_DIM = GLA_HEADS * GLA_DV
GATE_RANK = 16
GATE_NORMALIZER = 16.0
HEAD_DIM = 128
Q_HEADS = 8
KV_HEADS = 4
GROUP = Q_HEADS // KV_HEADS
MOBA_BLOCK = 256
MOBA_TOPK = 3
PAGE_SIZE = 128
ROPE_THETA = 10000.0
NEG_INF = -1e30
LANES = 128
SUBLANES = 8

VMEM_LIMIT_BYTES = 56 * 1024 * 1024

PROMPT_ROW_TILE = 256
GLA_MAX_FACTORED_SPAN = 40.0
PROMPT_GLA_CHUNK = 256
PROMPT_GLA_SEQS = 2
MLP_ROW_TILE = 512
MLP_ROW_TILE_NO_ATTN = 1024
MLP_FF_TILE = 1024
MOBA_KV_HEADS_PER_STEP = 4
SAMPLE_SEQ_TILE = 8
SAMPLE_PAGES_PER_STEP = 16
SAMPLE_PAGE_SLOTS = 3


def _rms(x):
    return x * lax.rsqrt(jnp.mean(x * x, axis=-1, keepdims=True) + NORM_EPS)


def _log_sigmoid(x):
    return jnp.minimum(x, 0.0) - jnp.log1p(jnp.exp(-jnp.abs(x)))


def _dot(a, b):
    return jnp.dot(a, b, preferred_element_type=F32)


def _dot_nt(a, b, precision=None):
    return lax.dot_general(a, b, (((1,), (1,)), ((), ())), precision=precision,
                           preferred_element_type=F32)


def _dot_tn(a, b):
    return lax.dot_general(a, b, (((0,), (0,)), ((), ())), preferred_element_type=F32)


def _const_spec(shape):
    n = len(shape)
    return pl.BlockSpec(shape, lambda *_: (0,) * n)


def _gla_pipe_kernel(x_ref, s0_ref, gmix_ref, wmain_ref, wglr_ref, wg2_ref, bg_ref, gout_ref, wo_ref,
                     h_ref, sout_ref, proj_ref, gk_ref, xk_ref, b_ref, oi_ref, o_ref, s_ref, *,
                     seqs, chunk, tiles_per_seq):
    i = pl.program_id(0)
    rows = seqs * chunk
    slot = lax.rem(i, 2)
    prev = 1 - slot
    tile_b = jnp.maximum(i - 1, 0)
    t_in_seq = lax.rem(tile_b, tiles_per_seq)
    kq = 2 * GLA_KEY_DIM
    k_off, v_off, r_off = GLA_KEY_DIM, 2 * GLA_KEY_DIM, 2 * GLA_KEY_DIM + GLA_VALUE_DIM

    @pl.when(i == 0)
    def _():
        proj_ref[1] = jnp.zeros(proj_ref.shape[1:], F32)
        gk_ref[1] = jnp.zeros(gk_ref.shape[1:], F32)
        xk_ref[1] = jnp.zeros(xk_ref.shape[1:], F32)

    @pl.when(t_in_seq == 0)
    def _():
        s_ref[...] = s0_ref[...]

    r_i = lax.broadcasted_iota(jnp.int32, (chunk, chunk), 0)
    c_i = lax.broadcasted_iota(jnp.int32, (chunk, chunk), 1)
    causal = c_i <= r_i
    mm = BF16 if chunk % 16 == 0 else F32
    tri = jnp.where(causal, 1.0, 0.0).astype(mm)
    n_sub = 2 if chunk % 32 == 0 else 1
    sub = chunk // n_sub
    half = max(sub // 2, 1)
    same_sub = causal if n_sub == 1 else jnp.logical_and(causal, (r_i >= sub) == (c_i >= sub))
    gout = gout_ref[...]
    units = [(sq, h) for sq in range(seqs) for h in range(GLA_HEADS)]
    rs = [slice(sq * chunk, (sq + 1) * chunk) for sq, _ in units]
    kd = [slice(h * GLA_DK, (h + 1) * GLA_DK) for _, h in units]
    vd = [slice(h * GLA_DV, (h + 1) * GLA_DV) for _, h in units]
    n_units = range(len(units))

    bcum = []
    for sq in range(seqs):
        g = gk_ref[prev, sq * chunk:(sq + 1) * chunk, :]
        g_hi = g.astype(BF16).astype(F32)
        bcum.append(_dot(tri, g_hi.astype(mm)) + _dot(tri, (g - g_hi).astype(mm)))

    b = [bcum[sq][:, kd[u]] for u, (sq, _) in enumerate(units)]
    b_last = [b[u][chunk - 1:chunk] for u in n_units]
    q = [proj_ref[prev, rs[u], kd[u]] * (GLA_DK ** -0.5) for u in n_units]
    k = [proj_ref[prev, rs[u], k_off + h * GLA_DK:k_off + (h + 1) * GLA_DK] for u, (_, h) in enumerate(units)]
    v = [proj_ref[prev, rs[u], v_off + h * GLA_DV:v_off + (h + 1) * GLA_DV].astype(mm)
         for u, (_, h) in enumerate(units)]
    s_old = [s_ref[sq, h] for sq, h in units]
    q_e = [(q[u] * jnp.exp(b[u])).astype(BF16) for u in n_units]
    k_e = [(k[u] * jnp.exp(b_last[u] - b[u])).astype(mm) for u in n_units]
    q_d, k_d = [], []
    for u in n_units:
        mids = jnp.concatenate(
            [jnp.broadcast_to(b[u][j * sub + half - 1:j * sub + half], (sub, GLA_DK)) for j in range(n_sub)],
            axis=0)
        q_d.append((q[u] * jnp.exp(b[u] - mids)).astype(mm))
        k_d.append((k[u] * jnp.exp(mids - b[u])).astype(mm))

    x = x_ref[...].reshape(rows, D_MODEL)
    xn = (_rms(x) * gmix_ref[...]).astype(BF16)
    proj_ref[slot, :, :kq] = _dot(xn, wmain_ref[:, :kq])
    glr = _dot(xn, wglr_ref[...])
    gpre = _dot(glr.astype(BF16), wg2_ref[...]) + bg_ref[...]
    gk_ref[slot] = _log_sigmoid(gpre) * (1.0 / GATE_NORMALIZER)

    o_inter = [_dot(q_e[u], s_old[u].astype(BF16)) for u in n_units]
    kv = [_dot_tn(k_e[u], v[u]) for u in n_units]
    for u, (sq, h) in enumerate(units):
        decay_col = jnp.transpose(jnp.broadcast_to(jnp.exp(b_last[u]), (GLA_DK, GLA_DK)))
        s_ref[sq, h] = s_old[u] * jnp.concatenate([decay_col, decay_col], axis=1) + kv[u]
    proj_ref[slot, :, kq:r_off] = _dot(xn, wmain_ref[:, kq:r_off])

    a = []
    for u in n_units:
        a.append(jnp.where(same_sub, _dot_nt(q_d[u], k_d[u]), 0.0).astype(mm))
    intra = [_dot(a[u], v[u]) for u in n_units]
    if n_sub == 2:
        a_x = []
        for u in n_units:
            edge = b[u][sub - 1:sub]
            q_x = q[u][sub:] * jnp.exp(b[u][sub:] - edge)
            k_x = k[u][:sub] * jnp.exp(edge - b[u][:sub])
            a_x.append(_dot_nt(q_x.astype(mm), k_x.astype(mm)).astype(mm))
        for u in n_units:
            cross = _dot(a_x[u], v[u][:sub])
            intra[u] = jnp.concatenate([intra[u][:sub], intra[u][sub:] + cross], axis=0)

    proj_ref[slot, :, r_off:] = _dot(xn, wmain_ref[:, r_off:])
    xk_ref[slot] = x

    def gated_norm(o, u):
        h = units[u][1]
        r = proj_ref[prev, rs[u], r_off + h * GLA_DV:r_off + (h + 1) * GLA_DV]
        return _rms(o) * gout * (r * jax.nn.sigmoid(r))

    for sq in range(seqs):
        b_ref[sq * chunk:(sq + 1) * chunk, :] = bcum[sq]
    for u in n_units:
        oi_ref[rs[u], vd[u]] = o_inter[u]
        o_ref[rs[u], vd[u]] = gated_norm(o_inter[u] + intra[u], u)

    def finish():
        y = _dot(o_ref[...].astype(BF16), wo_ref[...])
        h_ref[...] = (xk_ref[prev] + y).reshape(h_ref.shape)

    finish()

    span = jnp.zeros((1, GLA_DK), F32)
    for u in n_units:
        before = jnp.zeros((1, GLA_DK), F32)
        for j in range(n_sub):
            end = b[u][(j + 1) * sub - 1:(j + 1) * sub]
            span = jnp.maximum(span, before - end)
            before = end

    @pl.when(jnp.max(span) > GLA_MAX_FACTORED_SPAN)
    def _():
        col = lax.broadcasted_iota(jnp.int32, (chunk, GLA_DK), 0)

        def row_group(gi, c):
            for u, (sq, h) in enumerate(units):
                grp = pl.ds(pl.multiple_of(sq * chunk + gi * SUBLANES, SUBLANES), SUBLANES)
                b_g = b_ref[grp, kd[u]]
                q_g = proj_ref[prev, grp, kd[u]] * (GLA_DK ** -0.5)
                b_u = b_ref[rs[u], kd[u]]
                k_u = proj_ref[prev, rs[u], k_off + h * GLA_DK:k_off + (h + 1) * GLA_DK]
                v_u = proj_ref[prev, rs[u], v_off + h * GLA_DV:v_off + (h + 1) * GLA_DV]
                out_rows = []
                for r in range(SUBLANES):
                    live = col <= gi * SUBLANES + r
                    decay = jnp.where(live, jnp.exp(jnp.minimum(b_g[r:r + 1] - b_u, 0.0)), 0.0)
                    a_col = jnp.sum(decay * k_u * q_g[r:r + 1], axis=-1, keepdims=True)
                    out_rows.append(jnp.sum(a_col * v_u, axis=0, keepdims=True))
                oi_ref[grp, vd[u]] += jnp.concatenate(out_rows, axis=0)
            return c

        lax.fori_loop(0, chunk // SUBLANES, row_group, 0)
        for u in n_units:
            o_ref[rs[u], vd[u]] = gated_norm(oi_ref[rs[u], vd[u]], u)
        finish()

    @pl.when(t_in_seq == tiles_per_seq - 1)
    def _():
        sout_ref[...] = s_ref[...]


def _gla_pipe(x, s0, gmix, wmain, wglr, wg2, bg, gout, wo, *, seqs, chunk):
    nseq, t_len, _ = x.shape
    assert nseq % seqs == 0 and t_len % chunk == 0 and chunk % SUBLANES == 0
    tiles_per_seq = t_len // chunk
    n_tiles = (nseq // seqs) * tiles_per_seq
    rows = seqs * chunk
    state_block = (seqs, GLA_HEADS, GLA_DK, GLA_DV)
    proj_tile = lambda i: jnp.minimum(i, n_tiles - 1)
    rec_tile = lambda i: jnp.maximum(i - 1, 0)
    x_spec = lambda tile: pl.BlockSpec(
        (seqs, chunk, D_MODEL), lambda i: (tile(i) // tiles_per_seq, tile(i) % tiles_per_seq, 0))
    s_spec = pl.BlockSpec(state_block, lambda i: (rec_tile(i) // tiles_per_seq, 0, 0, 0))
    kern = functools.partial(_gla_pipe_kernel, seqs=seqs, chunk=chunk, tiles_per_seq=tiles_per_seq)
    resident = lambda w: pl.BlockSpec(w.shape, lambda i: (0, 0), pipeline_mode=pl.Buffered(1))
    return pl.pallas_call(
        kern,
        grid=(n_tiles + 1,),
        in_specs=[
            x_spec(proj_tile), s_spec,
            _const_spec(gmix.shape), resident(wmain), resident(wglr),
            resident(wg2), _const_spec(bg.shape), _const_spec(gout.shape), resident(wo),
        ],
        out_specs=[x_spec(rec_tile), s_spec],
        out_shape=[jax.ShapeDtypeStruct(x.shape, F32), jax.ShapeDtypeStruct(s0.shape, F32)],
        scratch_shapes=[
            pltpu.VMEM((2, rows, 2 * GLA_KEY_DIM + 2 * GLA_VALUE_DIM), F32),
            pltpu.VMEM((2, rows, GLA_KEY_DIM), F32),
            pltpu.VMEM((2, rows, D_MODEL), F32),
            pltpu.VMEM((rows, GLA_KEY_DIM), F32),
            pltpu.VMEM((rows, GLA_VALUE_DIM), F32),
            pltpu.VMEM((rows, GLA_VALUE_DIM), F32),
            pltpu.VMEM(state_block, F32),
        ],
        compiler_params=pltpu.CompilerParams(
            dimension_semantics=("arbitrary",), vmem_limit_bytes=VMEM_LIMIT_BYTES),
        name="gla_layer",
    )(x, s0, gmix, wmain, wglr, wg2, bg, gout, wo)


def _mlp_kernel(*refs, has_attn):
    if has_attn:
        h_ref, a_ref, wo_ref, g_ref, wup_ref, wdn_ref, out_ref, act_ref = refs
        h = h_ref[...] + _dot(a_ref[...].astype(BF16), wo_ref[...])
    else:
        h_ref, g_ref, wup_ref, wdn_ref, out_ref, act_ref = refs
        h = h_ref[...]
    xn = (_rms(h) * g_ref[...]).astype(BF16)
    for j in range(D_FF // MLP_FF_TILE):
        ff = slice(j * MLP_FF_TILE, (j + 1) * MLP_FF_TILE)
        a = jnp.maximum(_dot(xn, wup_ref[:, ff]), 0.0)
        act_ref[:, ff] = (a * a).astype(BF16)
    out_ref[...] = h + _dot(act_ref[...], wdn_ref[...])


def _mlp(h, g, wup, wdn, attn=None, wo=None):
    n = h.shape[0]
    tm = min(MLP_ROW_TILE if attn is not None else MLP_ROW_TILE_NO_ATTN, n)
    row_spec = pl.BlockSpec((tm, D_MODEL), lambda i: (i, 0))
    single = pl.Buffered(1)

    def wspec(w):
        return pl.BlockSpec(w.shape, lambda i: (0, 0), pipeline_mode=single)

    if attn is None:
        args = (h, g, wup, wdn)
        in_specs = [row_spec, _const_spec(g.shape), wspec(wup), wspec(wdn)]
    else:
        args = (h, attn, wo, g, wup, wdn)
        in_specs = [row_spec, row_spec, wspec(wo), _const_spec(g.shape), wspec(wup), wspec(wdn)]
    return pl.pallas_call(
        functools.partial(_mlp_kernel, has_attn=attn is not None),
        grid=(n // tm,),
        in_specs=in_specs,
        out_specs=row_spec,
        out_shape=jax.ShapeDtypeStruct(h.shape, F32),
        scratch_shapes=[pltpu.VMEM((tm, D_FF), BF16)],
        compiler_params=pltpu.CompilerParams(
            dimension_semantics=("arbitrary",), vmem_limit_bytes=VMEM_LIMIT_BYTES),
        name="mlp",
    )(*args)


def _rope(x, cos, sin_signed):
    return x * cos + pltpu.roll(x, HEAD_DIM // 2, 1) * sin_signed


def _kvq_kernel(h_ref, nkv_ref, nq_ref, wkv_ref, wq_ref, gk_ref, gq_ref, cos_ref, sin_ref,
                k_ref, q_ref, k4_ref, v4_ref, v_ref, *, transposed_v):
    tm = h_ref.shape[0]
    xh = _rms(h_ref[...])
    kv = _dot((xh * nkv_ref[...]).astype(BF16), wkv_ref[...])
    qq = _dot((xh * nq_ref[...]).astype(BF16), wq_ref[...])
    cos = cos_ref[...]
    sin = sin_ref[...]
    kdim = KV_HEADS * HEAD_DIM
    for hh in range(KV_HEADS):
        hd = slice(hh * HEAD_DIM, (hh + 1) * HEAD_DIM)
        kh = _rope(_rms(kv[:, hd]) * gk_ref[...], cos, sin)
        vh = kv[:, kdim + hh * HEAD_DIM:kdim + (hh + 1) * HEAD_DIM]
        k_ref[:, hd] = kh
        k4_ref[pl.ds(hh, tm, stride=KV_HEADS), :] = kh
        v4_ref[pl.ds(hh, tm, stride=KV_HEADS), :] = vh
        if transposed_v:
            v_ref[0, hh] = jnp.transpose(vh).astype(BF16)
    if not transposed_v:
        v_ref[...] = kv[:, kdim:]
    for hh in range(Q_HEADS):
        hd = slice(hh * HEAD_DIM, (hh + 1) * HEAD_DIM)
        q_ref[:, hd] = _rope(_rms(qq[:, hd]) * gq_ref[...], cos, sin)


def _kvq(h, nkv, nq, wkv, wq, gk, gq, cos, sin, *, tm, seq_len=None):
    n = h.shape[0]
    pos_tiles = cos.shape[0] // tm
    kdim = KV_HEADS * HEAD_DIM
    row = lambda w: pl.BlockSpec((tm, w), lambda i: (i, 0))
    row4 = pl.BlockSpec((tm * KV_HEADS, HEAD_DIM), lambda i: (i, 0))
    tab = pl.BlockSpec((tm, HEAD_DIM), lambda i: (i % pos_tiles, 0))
    out_specs = [row(kdim), row(Q_HEADS * HEAD_DIM), row4, row4]
    out_shape = [jax.ShapeDtypeStruct((n, kdim), F32), jax.ShapeDtypeStruct((n, Q_HEADS * HEAD_DIM), F32),
                 jax.ShapeDtypeStruct((n * KV_HEADS, HEAD_DIM), F32),
                 jax.ShapeDtypeStruct((n * KV_HEADS, HEAD_DIM), F32)]
    if seq_len is not None:
        assert seq_len % tm == 0 and tm % LANES == 0
        tiles = seq_len // tm
        out_specs.append(pl.BlockSpec((1, KV_HEADS, HEAD_DIM, tm), lambda i: (i // tiles, 0, 0, i % tiles)))
        out_shape.append(jax.ShapeDtypeStruct((n // seq_len, KV_HEADS, HEAD_DIM, seq_len), BF16))
    else:
        out_specs.append(row(kdim))
        out_shape.append(jax.ShapeDtypeStruct((n, kdim), F32))
    return pl.pallas_call(
        functools.partial(_kvq_kernel, transposed_v=seq_len is not None),
        grid=(n // tm,),
        in_specs=[row(D_MODEL), _const_spec(nkv.shape), _const_spec(nq.shape), _const_spec(wkv.shape),
                  _const_spec(wq.shape), _const_spec(gk.shape), _const_spec(gq.shape), tab, tab],
        out_specs=out_specs,
        out_shape=out_shape,
        compiler_params=pltpu.CompilerParams(
            dimension_semantics=("arbitrary",), vmem_limit_bytes=VMEM_LIMIT_BYTES),
        name="kvq",
    )(h, nkv, nq, wkv, wq, gk, gq, cos, sin)


def _select_topk_rows(gate, cand, ksel):
    n = gate.shape[0]
    row = lax.broadcasted_iota(jnp.int32, gate.shape, 0).astype(F32)
    gg = jnp.where(cand, gate, -jnp.inf)
    sel = jnp.zeros(gate.shape, F32)
    for _ in range(ksel):
        mx = jnp.max(gg, axis=0, keepdims=True)
        first = jnp.min(jnp.where(gg == mx, row, float(n)), axis=0, keepdims=True)
        hit = row == first
        sel = jnp.where(hit, 1.0, sel)
        gg = jnp.where(hit, -jnp.inf, gg)
    return jnp.where(cand, sel, 0.0)


def _split_bf16(x):
    hi = x.astype(BF16)
    return hi, (x - hi.astype(F32)).astype(BF16)


def _moba_prompt_kernel(q_ref, k_ref, vt_ref, o_ref, means_ref, s_ref, p_ref, *, nblk):
    qi = pl.program_id(2)
    sub = SUBLANES
    kv_heads = k_ref.shape[2] // HEAD_DIM
    streams = [(kv, kv * GROUP + g) for kv in range(kv_heads) for g in range(GROUP)]

    @pl.when(qi == 0)
    def _():
        means_ref[...] = jnp.zeros(means_ref.shape, F32)
        for kv in range(kv_heads):
            for n in range(nblk):
                blk = k_ref[0, n * MOBA_BLOCK:(n + 1) * MOBA_BLOCK, kv * HEAD_DIM:(kv + 1) * HEAD_DIM]
                means_ref[kv, n:n + 1, :] = jnp.sum(blk, axis=0, keepdims=True) * (1.0 / MOBA_BLOCK)

    def attend(nb):
        qb = q_ref[0]
        heads = [qb[:, hq * HEAD_DIM:(hq + 1) * HEAD_DIM] for _, hq in streams]
        qs = [(h * (HEAD_DIM ** -0.5 * math.log2(math.e))).astype(BF16) for h in heads]
        causal = (lax.broadcasted_iota(jnp.int32, (MOBA_BLOCK, MOBA_BLOCK), 0)
                  <= lax.broadcasted_iota(jnp.int32, (MOBA_BLOCK, MOBA_BLOCK), 1))
        ksel = min(MOBA_TOPK, nblk)
        keys = (nb + 1) * MOBA_BLOCK
        k_all = [k_ref[0, :keys, kv * HEAD_DIM:(kv + 1) * HEAD_DIM].astype(BF16) for kv in range(kv_heads)]

        sel = [None] * len(streams)
        if nb > ksel:
            row = lax.broadcasted_iota(jnp.int32, (means_ref.shape[1], MOBA_BLOCK), 0)
            for st, (kv, _) in enumerate(streams):
                m_hi, m_lo = _split_bf16(means_ref[kv])
                q_hi, q_lo = _split_bf16(heads[st])
                gate = _dot_nt(m_hi, q_hi) + _dot_nt(m_hi, q_lo) + _dot_nt(m_lo, q_hi)
                sel[st] = _select_topk_rows(gate, row < nb, ksel)
        m_parts = []
        for st, (kv, _) in enumerate(streams):
            m_part = None
            s_all = _dot_nt(k_all[kv], qs[st])
            for n in range(nb + 1):
                s = s_all[n * MOBA_BLOCK:(n + 1) * MOBA_BLOCK]
                if n == nb:
                    s = jnp.where(causal, s, NEG_INF)
                elif sel[st] is not None:
                    s = jnp.where(sel[st][n:n + 1, :] > 0.0, s, NEG_INF)
                s_ref[st, n] = s
                part = jnp.max(s.reshape(MOBA_BLOCK // sub, sub, MOBA_BLOCK), axis=0)
                m_part = part if m_part is None else jnp.maximum(m_part, part)
            m_parts.append(m_part)
        for st, (kv, hq) in enumerate(streams):
            m = jnp.max(m_parts[st], axis=0, keepdims=True)
            l_part = jnp.zeros((sub, MOBA_BLOCK), F32)
            for n in range(nb + 1):
                p = jnp.exp2(s_ref[st, n] - m)
                l_part = l_part + jnp.sum(p.reshape(MOBA_BLOCK // sub, sub, MOBA_BLOCK), axis=0)
                p_ref[st, n * MOBA_BLOCK:(n + 1) * MOBA_BLOCK, :] = p.astype(BF16)
            acc = _dot(vt_ref[0, kv, :, :keys], p_ref[st, :keys, :])
            out = acc * (1.0 / jnp.sum(l_part, axis=0, keepdims=True))
            o_ref[0, :, hq * HEAD_DIM:(hq + 1) * HEAD_DIM] = jnp.transpose(out)

    for nb in range(nblk):
        pl.when(qi == nb)(functools.partial(attend, nb))


def _moba_prompt(q, k, vt):
    bsz, t_len, _ = q.shape
    nblk = t_len // MOBA_BLOCK
    nblk_pad = -(-nblk // SUBLANES) * SUBLANES
    kvs = math.gcd(MOBA_KV_HEADS_PER_STEP, KV_HEADS)
    n_streams = kvs * GROUP
    return pl.pallas_call(
        functools.partial(_moba_prompt_kernel, nblk=nblk),
        grid=(bsz, KV_HEADS // kvs, nblk),
        in_specs=[
            pl.BlockSpec((1, MOBA_BLOCK, n_streams * HEAD_DIM), lambda b, h, i: (b, i, h)),
            pl.BlockSpec((1, t_len, kvs * HEAD_DIM), lambda b, h, i: (b, 0, h)),
            pl.BlockSpec((1, kvs, HEAD_DIM, t_len), lambda b, h, i: (b, h, 0, 0)),
        ],
        out_specs=pl.BlockSpec((1, MOBA_BLOCK, n_streams * HEAD_DIM), lambda b, h, i: (b, i, h)),
        out_shape=jax.ShapeDtypeStruct(q.shape, F32),
        scratch_shapes=[
            pltpu.VMEM((kvs, nblk_pad, HEAD_DIM), F32),
            pltpu.VMEM((n_streams, nblk, MOBA_BLOCK, MOBA_BLOCK), F32),
            pltpu.VMEM((n_streams, t_len, MOBA_BLOCK), BF16),
        ],
        compiler_params=pltpu.CompilerParams(
            dimension_semantics=("arbitrary", "arbitrary", "arbitrary"), vmem_limit_bytes=VMEM_LIMIT_BYTES),
        name="moba_prompt",
    )(q, k, vt)


def _moba_sample_kernel(pt_ref, q_ref, knew_ref, vnew_ref, ck_ref, cv_ref, o_ref,
                        kbuf, vbuf, sem, gate_ref, m_ref, l_ref, acc_ref, *, pages_per_step, n_blocks, n_new):
    j = pl.program_id(1)
    steps_per_seq = pl.num_programs(1)
    step = pl.program_id(0) * steps_per_seq + j
    n_steps = pl.num_programs(0) * steps_per_seq
    ahead = SAMPLE_PAGE_SLOTS - 1

    def page_copies(s, slot):
        out = []
        for i in range(pages_per_step):
            page = pt_ref[s * pages_per_step + i]
            out.append(pltpu.make_async_copy(ck_ref.at[page], kbuf.at[slot, i], sem.at[0, slot]))
            out.append(pltpu.make_async_copy(cv_ref.at[page], vbuf.at[slot, i], sem.at[1, slot]))
        return out

    @pl.when(step == 0)
    def _():
        for s in range(ahead):
            for c in page_copies(s, s):
                c.start()

    @pl.when(step + ahead < n_steps)
    def _():
        for c in page_copies(step + ahead, lax.rem(step + ahead, SAMPLE_PAGE_SLOTS)):
            c.start()

    slot = lax.rem(step, SAMPLE_PAGE_SLOTS)
    for c in page_copies(step, slot):
        c.wait()

    pages_per_block = MOBA_BLOCK // PAGE_SIZE
    blocks_per_step = pages_per_step // pages_per_block
    page_rows = PAGE_SIZE * KV_HEADS
    nkv = GROUP * n_new
    nrow = KV_HEADS * nkv
    lane = lax.broadcasted_iota(jnp.int32, (nrow, LANES), 1)
    scale = HEAD_DIM ** -0.5

    @pl.when(j == 0)
    def _():
        gate_ref[...] = jnp.zeros(gate_ref.shape, F32)
        m_ref[...] = jnp.zeros(m_ref.shape, F32)
        l_ref[...] = jnp.zeros(l_ref.shape, F32)

    qf = jnp.concatenate([q_ref[0, :, hq * HEAD_DIM:(hq + 1) * HEAD_DIM] for hq in range(Q_HEADS)], axis=0)
    qs = (qf * scale).astype(BF16)
    blk_cols = pages_per_block * page_rows
    col_head = lax.broadcasted_iota(jnp.int32, (nrow, blk_cols), 1) & (KV_HEADS - 1)
    row_head = lax.div(lax.broadcasted_iota(jnp.int32, (nrow, blk_cols), 0), nkv)
    same_head = col_head == row_head
    sub = SUBLANES

    blocks = range(blocks_per_step)
    scores, gcols = [], []
    for bi in blocks:
        kpages = [kbuf[slot, bi * pages_per_block + pp] for pp in range(pages_per_block)]
        scores.append(jnp.concatenate([_dot_nt(qs, kp.astype(BF16)) for kp in kpages], axis=1))
        ksum = sum(jnp.sum(kp.reshape(page_rows // sub, sub, HEAD_DIM), axis=0) for kp in kpages)
        mean = sum(ksum[i * KV_HEADS:(i + 1) * KV_HEADS] for i in range(sub // KV_HEADS)) * (1.0 / MOBA_BLOCK)
        mean_rows = jnp.concatenate(
            [jnp.broadcast_to(mean[h:h + 1], (nkv, HEAD_DIM)) for h in range(KV_HEADS)], axis=0)
        gcols.append(jnp.sum(qf * mean_rows, axis=-1, keepdims=True))
    probs, maxes, sums = [], [], []
    for bi in blocks:
        s = jnp.where(same_head, scores[bi], NEG_INF)
        mb = jnp.max(s, axis=-1, keepdims=True)
        p = jnp.exp(s - mb)
        maxes.append(mb)
        sums.append(jnp.sum(p, axis=-1, keepdims=True))
        probs.append(p.astype(BF16))
    gate, m_all, l_all = gate_ref[...], m_ref[...], l_ref[...]
    for bi in blocks:
        n = j * blocks_per_step + bi
        acc = _dot(probs[bi][:, :page_rows], vbuf[slot, bi * pages_per_block].astype(BF16))
        for pp in range(1, pages_per_block):
            acc = acc + _dot(probs[bi][:, pp * page_rows:(pp + 1) * page_rows],
                             vbuf[slot, bi * pages_per_block + pp].astype(BF16))
        acc_ref[n] = acc
        hot = lane == n
        gate = jnp.where(hot, gcols[bi], gate)
        m_all = jnp.where(hot, maxes[bi], m_all)
        l_all = jnp.where(hot, sums[bi], l_all)
    gate_ref[...] = gate
    m_ref[...] = m_all
    l_ref[...] = l_all

    @pl.when(j == pl.num_programs(1) - 1)
    def _():
        lane = lax.broadcasted_iota(jnp.int32, (nkv, LANES), 1)
        qrow_t = lax.rem(lax.broadcasted_iota(jnp.int32, (nkv, LANES), 0), n_new)
        lane_f = lane.astype(F32)
        for h in range(KV_HEADS):
            hd = slice(h * HEAD_DIM, (h + 1) * HEAD_DIM)
            rs = slice(h * nkv, (h + 1) * nkv)
            qsc = qf[rs] * scale
            gg = jnp.where(lane < n_blocks, gate_ref[rs, :], -jnp.inf)
            picked = jnp.zeros((nkv, LANES), F32)
            for _ in range(min(MOBA_TOPK, n_blocks)):
                mx = jnp.max(gg, axis=-1, keepdims=True)
                first = jnp.min(jnp.where(gg == mx, lane_f, float(LANES)), axis=-1, keepdims=True)
                hit = lane_f == first
                picked = jnp.where(hit, 1.0, picked)
                gg = jnp.where(hit, -jnp.inf, gg)
            sel = jnp.where(lane < n_blocks, picked, 0.0) > 0.0
            knew = knew_ref[0, :, hd]
            vnew = vnew_ref[0, :, hd]
            s_own = jnp.full((nkv, LANES), NEG_INF, F32)
            for jj in range(n_new):
                col = jnp.sum(qsc * knew[jj:jj + 1, :], axis=-1, keepdims=True)
                s_own = jnp.where(jnp.logical_and(lane == jj, qrow_t >= jj), col, s_own)
            mh = m_ref[rs, :]
            m_tot = jnp.maximum(jnp.max(s_own, axis=-1, keepdims=True),
                                jnp.max(jnp.where(sel, mh, NEG_INF), axis=-1, keepdims=True))
            w = jnp.where(sel, jnp.exp(mh - m_tot), 0.0)
            p_own = jnp.exp(s_own - m_tot)
            den = jnp.sum(w * l_ref[rs, :], axis=-1, keepdims=True) + jnp.sum(p_own, axis=-1, keepdims=True)
            num = jnp.zeros((nkv, HEAD_DIM), F32)
            for jj in range(n_new):
                num = num + p_own[:, jj:jj + 1] * vnew[jj:jj + 1, :]
            for n in range(n_blocks):
                num = num + w[:, n:n + 1] * acc_ref[n, rs, :]
            out = num / den
            for g in range(GROUP):
                o_ref[0, :, (h * GROUP + g) * HEAD_DIM:(h * GROUP + g + 1) * HEAD_DIM] = (
                    out[g * n_new:(g + 1) * n_new, :])


def _moba_sample(q, knew, vnew, cache_k, cache_v, page_table):
    db, ds, _ = q.shape
    n_pages = page_table.shape[1]
    pages_per_block = MOBA_BLOCK // PAGE_SIZE
    assert n_pages % pages_per_block == 0, "past length must be a whole number of MoBA blocks"
    n_blocks = n_pages // pages_per_block
    assert 0 < n_blocks <= LANES
    assert KV_HEADS & (KV_HEADS - 1) == 0 and 8 % KV_HEADS == 0
    pps = math.gcd(SAMPLE_PAGES_PER_STEP, n_pages)
    assert pps % pages_per_block == 0
    nrow = Q_HEADS * ds
    kdim = KV_HEADS * HEAD_DIM
    assert db * (n_pages // pps) >= SAMPLE_PAGE_SLOTS - 1
    page_buf = pltpu.VMEM((SAMPLE_PAGE_SLOTS, pps, PAGE_SIZE * KV_HEADS, HEAD_DIM), F32)

    per_seq = lambda w: pl.BlockSpec((1, ds, w), lambda b, j, pt: (b, 0, 0))
    hbm = pl.BlockSpec(memory_space=pl.ANY)
    grid_spec = pltpu.PrefetchScalarGridSpec(
        num_scalar_prefetch=1,
        grid=(db, n_pages // pps),
        in_specs=[per_seq(Q_HEADS * HEAD_DIM), per_seq(kdim), per_seq(kdim), hbm, hbm],
        out_specs=per_seq(Q_HEADS * HEAD_DIM),
        scratch_shapes=[
            page_buf, page_buf, pltpu.SemaphoreType.DMA((2, SAMPLE_PAGE_SLOTS)),
            pltpu.VMEM((nrow, LANES), F32),
            pltpu.VMEM((nrow, LANES), F32),
            pltpu.VMEM((nrow, LANES), F32),
            pltpu.VMEM((n_blocks, nrow, HEAD_DIM), F32),
        ],
    )
    return pl.pallas_call(
        functools.partial(_moba_sample_kernel, pages_per_step=pps, n_blocks=n_blocks, n_new=ds),
        grid_spec=grid_spec,
        out_shape=jax.ShapeDtypeStruct(q.shape, F32),
        compiler_params=pltpu.CompilerParams(
            dimension_semantics=("arbitrary", "arbitrary"), vmem_limit_bytes=VMEM_LIMIT_BYTES),
        name="moba_sample",
    )(page_table.reshape(-1), q, knew, vnew, cache_k, cache_v)


def _rope_tables(pos):
    inv = ROPE_THETA ** (-jnp.arange(0, HEAD_DIM, 2, dtype=F32) / HEAD_DIM)
    ang = pos.astype(F32)[:, None] * inv[None, :]
    cos, sin = jnp.cos(ang), jnp.sin(ang)
    return jnp.concatenate([cos, cos], axis=-1), jnp.concatenate([-sin, sin], axis=-1)


def kernel(x_prompt, x_sample, state_gla, cache_k, cache_v, page_table, w_in_a, w_g2, b_g, g_gla_out, w_o_a,
           norm_kv, w_kv, g_k, w_q_b, g_q, w_o_b, norm_mix, norm_mlp, w_up, w_down):
    bsz, t_len, _ = x_prompt.shape
    db, ds, _ = x_sample.shape
    n_a = state_gla.shape[0]
    depth = norm_mix.shape[0]
    assert n_a == 1 and depth == 2, "one GLA layer followed by one MoBA layer"
    assert t_len % PROMPT_ROW_TILE == 0 and t_len % PROMPT_GLA_CHUNK == 0 and t_len % MOBA_BLOCK == 0
    past = page_table.shape[1] * PAGE_SIZE
    kdim = KV_HEADS * HEAD_DIM
    row2 = lambda a: a.reshape(1, -1)

    main = 2 * GLA_KEY_DIM + 2 * GLA_VALUE_DIM
    wmain = w_in_a[0, :, :main].astype(BF16)
    wglr = jnp.pad(w_in_a[0, :, main:], ((0, 0), (0, LANES - GATE_RANK))).astype(BF16)
    wg2 = jnp.pad(w_g2[0], ((0, LANES - GATE_RANK), (0, 0))).astype(BF16)
    gla_w = (row2(norm_mix[0]), wmain, wglr, wg2, row2(b_g[0]), row2(g_gla_out[0]), w_o_a[0].astype(BF16))
    s0_p = jnp.zeros((bsz,) + state_gla.shape[2:], F32)
    seq_tile = math.gcd(SAMPLE_SEQ_TILE, db)
    hp, st_p = _gla_pipe(x_prompt, s0_p, *gla_w, seqs=math.gcd(PROMPT_GLA_SEQS, bsz), chunk=PROMPT_GLA_CHUNK)
    hs, st_s = _gla_pipe(x_sample, state_gla[0], *gla_w, seqs=seq_tile, chunk=ds)
    hp = hp.reshape(bsz * t_len, D_MODEL)
    hs = hs.reshape(db * ds, D_MODEL)
    wup = w_up.astype(BF16)
    wdn = w_down.astype(BF16)
    hp = _mlp(hp, row2(norm_mlp[0]), wup[0], wdn[0])
    hs = _mlp(hs, row2(norm_mlp[0]), wup[0], wdn[0])

    cos_p, sin_p = _rope_tables(jnp.arange(t_len, dtype=jnp.int32))
    cos_s, sin_s = _rope_tables(past + jnp.arange(ds, dtype=jnp.int32))
    cos_s, sin_s = jnp.tile(cos_s, (db, 1)), jnp.tile(sin_s, (db, 1))
    kvq_w = (row2(norm_kv), row2(norm_mix[1]), w_kv.astype(BF16), w_q_b[0].astype(BF16), row2(g_k), row2(g_q[0]))
    k_p, q_p, k4_p, v4_p, vt_p = _kvq(hp, *kvq_w, cos_p, sin_p, tm=PROMPT_ROW_TILE, seq_len=t_len)
    k_s, q_s, k4_s, v4_s, v_s = _kvq(hs, *kvq_w, cos_s, sin_s, tm=db * ds)

    a_p = _moba_prompt(q_p.reshape(bsz, t_len, -1), k_p.reshape(bsz, t_len, kdim), vt_p)
    pages = lambda c: c.reshape(c.shape[0], PAGE_SIZE * KV_HEADS, HEAD_DIM)
    a_s = _moba_sample(q_s.reshape(db, ds, -1), k_s.reshape(db, ds, kdim), v_s.reshape(db, ds, kdim),
                       pages(cache_k), pages(cache_v), page_table)
    wob = w_o_b[0].astype(BF16)
    hp = _mlp(hp, row2(norm_mlp[1]), wup[1], wdn[1], attn=a_p.reshape(bsz * t_len, -1), wo=wob)
    hs = _mlp(hs, row2(norm_mlp[1]), wup[1], wdn[1], attn=a_s.reshape(db * ds, -1), wo=wob)

    kv4 = lambda a, n, t: a.reshape(n, t, KV_HEADS, HEAD_DIM)
    return (hp.reshape(bsz, t_len, D_MODEL), hs.reshape(db, ds, D_MODEL), st_p[None], st_s[None],
            kv4(k4_p, bsz, t_len), kv4(v4_p, bsz, t_len), kv4(k4_s, db, ds), kv4(v4_s, db, ds))
```

```python
import functools
import math

import jax
import jax.numpy as jnp
from jax import lax
from jax.experimental import pallas as pl
from jax.experimental.pallas import tpu as pltpu

F32 = jnp.float32
BF16 = jnp.bfloat16

D_MODEL = 1024
D_FF = 4 * D_MODEL
NORM_EPS = 1e-6
GLA_HEADS = 4
GLA_DK = 128
GLA_DV = 256
GLA_KEY_DIM = GLA_HEADS * GLA_DK
GLA_VALUE_DIM = GLA_HEADS * GLA_DV
GATE_RANK = 16
GATE_NORMALIZER = 16.0
HEAD_DIM = 128
Q_HEADS = 8
KV_HEADS = 4
GROUP = Q_HEADS // KV_HEADS
MOBA_BLOCK = 256
MOBA_TOPK = 3
PAGE_SIZE = 128
ROPE_THETA = 10000.0
NEG_INF = -1e30
LANES = 128
SUBLANES = 8

VMEM_LIMIT_BYTES = 56 * 1024 * 1024

PROMPT_ROW_TILE = 256
GLA_MAX_FACTORED_SPAN = 40.0
PROMPT_GLA_CHUNK = 256
PROMPT_GLA_SEQS = 2
MLP_ROW_TILE = 512
MLP_ROW_TILE_NO_ATTN = 1024
MLP_FF_TILE = 1024
MOBA_KV_HEADS_PER_STEP = 4
SAMPLE_SEQ_TILE = 8
SAMPLE_PAGES_PER_STEP = 32
SAMPLE_PAGE_SLOTS = 2


def _rms(x):
    return x * lax.rsqrt(jnp.mean(x * x, axis=-1, keepdims=True) + NORM_EPS)


def _log_sigmoid(x):
    return jnp.minimum(x, 0.0) - jnp.log1p(jnp.exp(-jnp.abs(x)))


def _dot(a, b):
    return jnp.dot(a, b, preferred_element_type=F32)


def _dot_nt(a, b, precision=None):
    return lax.dot_general(a, b, (((1,), (1,)), ((), ())), precision=precision,
                           preferred_element_type=F32)


def _dot_tn(a, b):
    return lax.dot_general(a, b, (((0,), (0,)), ((), ())), preferred_element_type=F32)


def _const_spec(shape):
    n = len(shape)
    return pl.BlockSpec(shape, lambda *_: (0,) * n)


def _gla_pipe_kernel(x_ref, s0_ref, gmix_ref, wmain_ref, wglr_ref, wg2_ref, bg_ref, gout_ref, wo_ref,
                     h_ref, sout_ref, proj_ref, gk_ref, xk_ref, b_ref, oi_ref, o_ref, s_ref, *,
                     seqs, chunk, tiles_per_seq):
    i = pl.program_id(0)
    rows = seqs * chunk
    slot = lax.rem(i, 2)
    prev = 1 - slot
    tile_b = jnp.maximum(i - 1, 0)
    t_in_seq = lax.rem(tile_b, tiles_per_seq)
    kq = 2 * GLA_KEY_DIM
    k_off, v_off, r_off = GLA_KEY_DIM, 2 * GLA_KEY_DIM, 2 * GLA_KEY_DIM + GLA_VALUE_DIM

    @pl.when(i == 0)
    def _():
        proj_ref[1] = jnp.zeros(proj_ref.shape[1:], F32)
        gk_ref[1] = jnp.zeros(gk_ref.shape[1:], F32)
        xk_ref[1] = jnp.zeros(xk_ref.shape[1:], F32)

    @pl.when(t_in_seq == 0)
    def _():
        s_ref[...] = s0_ref[...]

    r_i = lax.broadcasted_iota(jnp.int32, (chunk, chunk), 0)
    c_i = lax.broadcasted_iota(jnp.int32, (chunk, chunk), 1)
    causal = c_i <= r_i
    mm = BF16 if chunk % 16 == 0 else F32
    tri = jnp.where(causal, 1.0, 0.0).astype(mm)
    n_sub = 2 if chunk % 32 == 0 else 1
    sub = chunk // n_sub
    half = max(sub // 2, 1)
    same_sub = causal if n_sub == 1 else jnp.logical_and(causal, (r_i >= sub) == (c_i >= sub))
    gout = gout_ref[...]
    units = [(sq, h) for sq in range(seqs) for h in range(GLA_HEADS)]
    rs = [slice(sq * chunk, (sq + 1) * chunk) for sq, _ in units]
    kd = [slice(h * GLA_DK, (h + 1) * GLA_DK) for _, h in units]
    vd = [slice(h * GLA_DV, (h + 1) * GLA_DV) for _, h in units]
    n_units = range(len(units))

    bcum = []
    for sq in range(seqs):
        g = gk_ref[prev, sq * chunk:(sq + 1) * chunk, :]
        g_hi = g.astype(BF16).astype(F32)
        bcum.append(_dot(tri, g_hi.astype(mm)) + _dot(tri, (g - g_hi).astype(mm)))

    b = [bcum[sq][:, kd[u]] for u, (sq, _) in enumerate(units)]
    b_last = [b[u][chunk - 1:chunk] for u in n_units]
    q = [proj_ref[prev, rs[u], kd[u]] * (GLA_DK ** -0.5) for u in n_units]
    k = [proj_ref[prev, rs[u], k_off + h * GLA_DK:k_off + (h + 1) * GLA_DK] for u, (_, h) in enumerate(units)]
    v = [proj_ref[prev, rs[u], v_off + h * GLA_DV:v_off + (h + 1) * GLA_DV].astype(mm)
         for u, (_, h) in enumerate(units)]
    s_old = [s_ref[sq, h] for sq, h in units]
    q_e = [(q[u] * jnp.exp(b[u])).astype(BF16) for u in n_units]
    k_e = [(k[u] * jnp.exp(b_last[u] - b[u])).astype(mm) for u in n_units]
    q_d, k_d = [], []
    for u in n_units:
        mids = jnp.concatenate(
            [jnp.broadcast_to(b[u][j * sub + half - 1:j * sub + half], (sub, GLA_DK)) for j in range(n_sub)],
            axis=0)
        q_d.append((q[u] * jnp.exp(b[u] - mids)).astype(mm))
        k_d.append((k[u] * jnp.exp(mids - b[u])).astype(mm))

    x = x_ref[...].reshape(rows, D_MODEL)
    xn = (_rms(x) * gmix_ref[...]).astype(BF16)
    proj_ref[slot, :, :kq] = _dot(xn, wmain_ref[:, :kq])
    glr = _dot(xn, wglr_ref[...])
    gpre = _dot(glr.astype(BF16), wg2_ref[...]) + bg_ref[...]
    gk_ref[slot] = _log_sigmoid(gpre) * (1.0 / GATE_NORMALIZER)

    o_inter = [_dot(q_e[u], s_old[u].astype(BF16)) for u in n_units]
    kv = [_dot_tn(k_e[u], v[u]) for u in n_units]
    for u, (sq, h) in enumerate(units):
        decay_col = jnp.transpose(jnp.broadcast_to(jnp.exp(b_last[u]), (GLA_DK, GLA_DK)))
        s_ref[sq, h] = s_old[u] * jnp.concatenate([decay_col, decay_col], axis=1) + kv[u]
    proj_ref[slot, :, kq:r_off] = _dot(xn, wmain_ref[:, kq:r_off])

    a = []
    for u in n_units:
        a.append(jnp.where(same_sub, _dot_nt(q_d[u], k_d[u]), 0.0).astype(mm))
    intra = [_dot(a[u], v[u]) for u in n_units]
    if n_sub == 2:
        a_x = []
        for u in n_units:
            edge = b[u][sub - 1:sub]
            q_x = q[u][sub:] * jnp.exp(b[u][sub:] - edge)
            k_x = k[u][:sub] * jnp.exp(edge - b[u][:sub])
            a_x.append(_dot_nt(q_x.astype(mm), k_x.astype(mm)).astype(mm))
        for u in n_units:
            cross = _dot(a_x[u], v[u][:sub])
            intra[u] = jnp.concatenate([intra[u][:sub], intra[u][sub:] + cross], axis=0)

    proj_ref[slot, :, r_off:] = _dot(xn, wmain_ref[:, r_off:])
    xk_ref[slot] = x

    def gated_norm(o, u):
        h = units[u][1]
        r = proj_ref[prev, rs[u], r_off + h * GLA_DV:r_off + (h + 1) * GLA_DV]
        return _rms(o) * gout * (r * jax.nn.sigmoid(r))

    for sq in range(seqs):
        b_ref[sq * chunk:(sq + 1) * chunk, :] = bcum[sq]
    for u in n_units:
        oi_ref[rs[u], vd[u]] = o_inter[u]
        o_ref[rs[u], vd[u]] = gated_norm(o_inter[u] + intra[u], u)

    def finish():
        y = _dot(o_ref[...].astype(BF16), wo_ref[...])
        h_ref[...] = (xk_ref[prev] + y).reshape(h_ref.shape)

    finish()

    span = jnp.zeros((1, GLA_DK), F32)
    for u in n_units:
        before = jnp.zeros((1, GLA_DK), F32)
        for j in range(n_sub):
            end = b[u][(j + 1) * sub - 1:(j + 1) * sub]
            span = jnp.maximum(span, before - end)
            before = end

    @pl.when(jnp.max(span) > GLA_MAX_FACTORED_SPAN)
    def _():
        col = lax.broadcasted_iota(jnp.int32, (chunk, GLA_DK), 0)

        def row_group(gi, c):
            for u, (sq, h) in enumerate(units):
                grp = pl.ds(pl.multiple_of(sq * chunk + gi * SUBLANES, SUBLANES), SUBLANES)
                b_g = b_ref[grp, kd[u]]
                q_g = proj_ref[prev, grp, kd[u]] * (GLA_DK ** -0.5)
                b_u = b_ref[rs[u], kd[u]]
                k_u = proj_ref[prev, rs[u], k_off + h * GLA_DK:k_off + (h + 1) * GLA_DK]
                v_u = proj_ref[prev, rs[u], v_off + h * GLA_DV:v_off + (h + 1) * GLA_DV]
                out_rows = []
                for r in range(SUBLANES):
                    live = col <= gi * SUBLANES + r
                    decay = jnp.where(live, jnp.exp(jnp.minimum(b_g[r:r + 1] - b_u, 0.0)), 0.0)
                    a_col = jnp.sum(decay * k_u * q_g[r:r + 1], axis=-1, keepdims=True)
                    out_rows.append(jnp.sum(a_col * v_u, axis=0, keepdims=True))
                oi_ref[grp, vd[u]] += jnp.concatenate(out_rows, axis=0)
            return c

        lax.fori_loop(0, chunk // SUBLANES, row_group, 0)
        for u in n_units:
            o_ref[rs[u], vd[u]] = gated_norm(oi_ref[rs[u], vd[u]], u)
        finish()

    @pl.when(t_in_seq == tiles_per_seq - 1)
    def _():
        sout_ref[...] = s_ref[...]


def _gla_pipe(x, s0, gmix, wmain, wglr, wg2, bg, gout, wo, *, seqs, chunk):
    nseq, t_len, _ = x.shape
    assert nseq % seqs == 0 and t_len % chunk == 0 and chunk % SUBLANES == 0
    tiles_per_seq = t_len // chunk
    n_tiles = (nseq // seqs) * tiles_per_seq
    rows = seqs * chunk
    state_block = (seqs, GLA_HEADS, GLA_DK, GLA_DV)
    proj_tile = lambda i: jnp.minimum(i, n_tiles - 1)
    rec_tile = lambda i: jnp.maximum(i - 1, 0)
    x_spec = lambda tile: pl.BlockSpec(
        (seqs, chunk, D_MODEL), lambda i: (tile(i) // tiles_per_seq, tile(i) % tiles_per_seq, 0))
    s_spec = pl.BlockSpec(state_block, lambda i: (rec_tile(i) // tiles_per_seq, 0, 0, 0))
    kern = functools.partial(_gla_pipe_kernel, seqs=seqs, chunk=chunk, tiles_per_seq=tiles_per_seq)
    resident = lambda w: pl.BlockSpec(w.shape, lambda i: (0, 0), pipeline_mode=pl.Buffered(1))
    return pl.pallas_call(
        kern,
        grid=(n_tiles + 1,),
        in_specs=[
            x_spec(proj_tile), s_spec,
            _const_spec(gmix.shape), resident(wmain), resident(wglr),
            resident(wg2), _const_spec(bg.shape), _const_spec(gout.shape), resident(wo),
        ],
        out_specs=[x_spec(rec_tile), s_spec],
        out_shape=[jax.ShapeDtypeStruct(x.shape, F32), jax.ShapeDtypeStruct(s0.shape, F32)],
        scratch_shapes=[
            pltpu.VMEM((2, rows, 2 * GLA_KEY_DIM + 2 * GLA_VALUE_DIM), F32),
            pltpu.VMEM((2, rows, GLA_KEY_DIM), F32),
            pltpu.VMEM((2, rows, D_MODEL), F32),
            pltpu.VMEM((rows, GLA_KEY_DIM), F32),
            pltpu.VMEM((rows, GLA_VALUE_DIM), F32),
            pltpu.VMEM((rows, GLA_VALUE_DIM), F32),
            pltpu.VMEM(state_block, F32),
        ],
        compiler_params=pltpu.CompilerParams(
            dimension_semantics=("arbitrary",), vmem_limit_bytes=VMEM_LIMIT_BYTES),
        name="gla_layer",
    )(x, s0, gmix, wmain, wglr, wg2, bg, gout, wo)


def _mlp_kernel(*refs, has_attn):
    if has_attn:
        h_ref, a_ref, wo_ref, g_ref, wup_ref, wdn_ref, out_ref, act_ref = refs
        h = h_ref[...] + _dot(a_ref[...].astype(BF16), wo_ref[...])
    else:
        h_ref, g_ref, wup_ref, wdn_ref, out_ref, act_ref = refs
        h = h_ref[...]
    xn = (_rms(h) * g_ref[...]).astype(BF16)
    for j in range(D_FF // MLP_FF_TILE):
        ff = slice(j * MLP_FF_TILE, (j + 1) * MLP_FF_TILE)
        a = jnp.maximum(_dot(xn, wup_ref[:, ff]), 0.0)
        act_ref[:, ff] = (a * a).astype(BF16)
    out_ref[...] = h + _dot(act_ref[...], wdn_ref[...])


def _mlp(h, g, wup, wdn, attn=None, wo=None):
    n = h.shape[0]
    tm = min(MLP_ROW_TILE if attn is not None else MLP_ROW_TILE_NO_ATTN, n)
    row_spec = pl.BlockSpec((tm, D_MODEL), lambda i: (i, 0))
    single = pl.Buffered(1)

    def wspec(w):
        return pl.BlockSpec(w.shape, lambda i: (0, 0), pipeline_mode=single)

    if attn is None:
        args = (h, g, wup, wdn)
        in_specs = [row_spec, _const_spec(g.shape), wspec(wup), wspec(wdn)]
    else:
        args = (h, attn, wo, g, wup, wdn)
        in_specs = [row_spec, row_spec, wspec(wo), _const_spec(g.shape), wspec(wup), wspec(wdn)]
    return pl.pallas_call(
        functools.partial(_mlp_kernel, has_attn=attn is not None),
        grid=(n // tm,),
        in_specs=in_specs,
        out_specs=row_spec,
        out_shape=jax.ShapeDtypeStruct(h.shape, F32),
        scratch_shapes=[pltpu.VMEM((tm, D_FF), BF16)],
        compiler_params=pltpu.CompilerParams(
            dimension_semantics=("arbitrary",), vmem_limit_bytes=VMEM_LIMIT_BYTES),
        name="mlp",
    )(*args)


def _rope(x, cos, sin_signed):
    return x * cos + pltpu.roll(x, HEAD_DIM // 2, 1) * sin_signed


def _kvq_kernel(h_ref, nkv_ref, nq_ref, wkv_ref, wq_ref, gk_ref, gq_ref, cos_ref, sin_ref,
                k_ref, q_ref, k4_ref, v4_ref, v_ref, *, transposed_v):
    tm = h_ref.shape[0]
    xh = _rms(h_ref[...])
    kv = _dot((xh * nkv_ref[...]).astype(BF16), wkv_ref[...])
    qq = _dot((xh * nq_ref[...]).astype(BF16), wq_ref[...])
    cos = cos_ref[...]
    sin = sin_ref[...]
    kdim = KV_HEADS * HEAD_DIM
    for hh in range(KV_HEADS):
        hd = slice(hh * HEAD_DIM, (hh + 1) * HEAD_DIM)
        kh = _rope(_rms(kv[:, hd]) * gk_ref[...], cos, sin)
        vh = kv[:, kdim + hh * HEAD_DIM:kdim + (hh + 1) * HEAD_DIM]
        k_ref[:, hd] = kh
        k4_ref[pl.ds(hh, tm, stride=KV_HEADS), :] = kh
        v4_ref[pl.ds(hh, tm, stride=KV_HEADS), :] = vh
        if transposed_v:
            v_ref[0, hh] = jnp.transpose(vh).astype(BF16)
    if not transposed_v:
        v_ref[...] = kv[:, kdim:]
    for hh in range(Q_HEADS):
        hd = slice(hh * HEAD_DIM, (hh + 1) * HEAD_DIM)
        q_ref[:, hd] = _rope(_rms(qq[:, hd]) * gq_ref[...], cos, sin)


def _kvq(h, nkv, nq, wkv, wq, gk, gq, cos, sin, *, tm, seq_len=None):
    n = h.shape[0]
    pos_tiles = cos.shape[0] // tm
    kdim = KV_HEADS * HEAD_DIM
    row = lambda w: pl.BlockSpec((tm, w), lambda i: (i, 0))
    row4 = pl.BlockSpec((tm * KV_HEADS, HEAD_DIM), lambda i: (i, 0))
    tab = pl.BlockSpec((tm, HEAD_DIM), lambda i: (i % pos_tiles, 0))
    out_specs = [row(kdim), row(Q_HEADS * HEAD_DIM), row4, row4]
    out_shape = [jax.ShapeDtypeStruct((n, kdim), F32), jax.ShapeDtypeStruct((n, Q_HEADS * HEAD_DIM), F32),
                 jax.ShapeDtypeStruct((n * KV_HEADS, HEAD_DIM), F32),
                 jax.ShapeDtypeStruct((n * KV_HEADS, HEAD_DIM), F32)]
    if seq_len is not None:
        assert seq_len % tm == 0 and tm % LANES == 0
        tiles = seq_len // tm
        out_specs.append(pl.BlockSpec((1, KV_HEADS, HEAD_DIM, tm), lambda i: (i // tiles, 0, 0, i % tiles)))
        out_shape.append(jax.ShapeDtypeStruct((n // seq_len, KV_HEADS, HEAD_DIM, seq_len), BF16))
    else:
        out_specs.append(row(kdim))
        out_shape.append(jax.ShapeDtypeStruct((n, kdim), F32))
    return pl.pallas_call(
        functools.partial(_kvq_kernel, transposed_v=seq_len is not None),
        grid=(n // tm,),
        in_specs=[row(D_MODEL), _const_spec(nkv.shape), _const_spec(nq.shape), _const_spec(wkv.shape),
                  _const_spec(wq.shape), _const_spec(gk.shape), _const_spec(gq.shape), tab, tab],
        out_specs=out_specs,
        out_shape=out_shape,
        compiler_params=pltpu.CompilerParams(
            dimension_semantics=("arbitrary",), vmem_limit_bytes=VMEM_LIMIT_BYTES),
        name="kvq",
    )(h, nkv, nq, wkv, wq, gk, gq, cos, sin)


def _select_topk_rows(gate, cand, ksel):
    n = gate.shape[0]
    row = lax.broadcasted_iota(jnp.int32, gate.shape, 0).astype(F32)
    gg = jnp.where(cand, gate, -jnp.inf)
    sel = jnp.zeros(gate.shape, F32)
    for _ in range(ksel):
        mx = jnp.max(gg, axis=0, keepdims=True)
        first = jnp.min(jnp.where(gg == mx, row, float(n)), axis=0, keepdims=True)
        hit = row == first
        sel = jnp.where(hit, 1.0, sel)
        gg = jnp.where(hit, -jnp.inf, gg)
    return jnp.where(cand, sel, 0.0)


def _split_bf16(x):
    hi = x.astype(BF16)
    return hi, (x - hi.astype(F32)).astype(BF16)


def _moba_prompt_kernel(q_ref, k_ref, vt_ref, o_ref, means_ref, s_ref, p_ref, *, nblk):
    qi = pl.program_id(2)
    sub = SUBLANES
    kv_heads = k_ref.shape[2] // HEAD_DIM
    streams = [(kv, kv * GROUP + g) for kv in range(kv_heads) for g in range(GROUP)]

    @pl.when(qi == 0)
    def _():
        means_ref[...] = jnp.zeros(means_ref.shape, F32)
        for kv in range(kv_heads):
            for n in range(nblk):
                blk = k_ref[0, n * MOBA_BLOCK:(n + 1) * MOBA_BLOCK, kv * HEAD_DIM:(kv + 1) * HEAD_DIM]
                means_ref[kv, n:n + 1, :] = jnp.sum(blk, axis=0, keepdims=True) * (1.0 / MOBA_BLOCK)

    def attend(nb):
        qb = q_ref[0]
        heads = [qb[:, hq * HEAD_DIM:(hq + 1) * HEAD_DIM] for _, hq in streams]
        qs = [(h * (HEAD_DIM ** -0.5 * math.log2(math.e))).astype(BF16) for h in heads]
        causal = (lax.broadcasted_iota(jnp.int32, (MOBA_BLOCK, MOBA_BLOCK), 0)
                  <= lax.broadcasted_iota(jnp.int32, (MOBA_BLOCK, MOBA_BLOCK), 1))
        ksel = min(MOBA_TOPK, nblk)
        keys = (nb + 1) * MOBA_BLOCK
        k_all = [k_ref[0, :keys, kv * HEAD_DIM:(kv + 1) * HEAD_DIM].astype(BF16) for kv in range(kv_heads)]

        sel = [None] * len(streams)
        if nb > ksel:
            row = lax.broadcasted_iota(jnp.int32, (means_ref.shape[1], MOBA_BLOCK), 0)
            for st, (kv, _) in enumerate(streams):
                m_hi, m_lo = _split_bf16(means_ref[kv])
                q_hi, q_lo = _split_bf16(heads[st])
                gate = _dot_nt(m_hi, q_hi) + _dot_nt(m_hi, q_lo) + _dot_nt(m_lo, q_hi)
                sel[st] = _select_topk_rows(gate, row < nb, ksel)
        m_parts = []
        for st, (kv, _) in enumerate(streams):
            m_part = None
            s_all = _dot_nt(k_all[kv], qs[st])
            for n in range(nb + 1):
                s = s_all[n * MOBA_BLOCK:(n + 1) * MOBA_BLOCK]
                if n == nb:
                    s = jnp.where(causal, s, NEG_INF)
                elif sel[st] is not None:
                    s = jnp.where(sel[st][n:n + 1, :] > 0.0, s, NEG_INF)
                s_ref[st, n] = s
                part = jnp.max(s.reshape(MOBA_BLOCK // sub, sub, MOBA_BLOCK), axis=0)
                m_part = part if m_part is None else jnp.maximum(m_part, part)
            m_parts.append(m_part)
        for st, (kv, hq) in enumerate(streams):
            m = jnp.max(m_parts[st], axis=0, keepdims=True)
            l_part = jnp.zeros((sub, MOBA_BLOCK), F32)
            for n in range(nb + 1):
                p = jnp.exp2(s_ref[st, n] - m)
                l_part = l_part + jnp.sum(p.reshape(MOBA_BLOCK // sub, sub, MOBA_BLOCK), axis=0)
                p_ref[st, n * MOBA_BLOCK:(n + 1) * MOBA_BLOCK, :] = p.astype(BF16)
            acc = _dot(vt_ref[0, kv, :, :keys], p_ref[st, :keys, :])
            out = acc * (1.0 / jnp.sum(l_part, axis=0, keepdims=True))
            o_ref[0, :, hq * HEAD_DIM:(hq + 1) * HEAD_DIM] = jnp.transpose(out)

    for nb in range(nblk):
        pl.when(qi == nb)(functools.partial(attend, nb))


def _moba_prompt(q, k, vt):
    bsz, t_len, _ = q.shape
    nblk = t_len // MOBA_BLOCK
    nblk_pad = -(-nblk // SUBLANES) * SUBLANES
    kvs = math.gcd(MOBA_KV_HEADS_PER_STEP, KV_HEADS)
    n_streams = kvs * GROUP
    return pl.pallas_call(
        functools.partial(_moba_prompt_kernel, nblk=nblk),
        grid=(bsz, KV_HEADS // kvs, nblk),
        in_specs=[
            pl.BlockSpec((1, MOBA_BLOCK, n_streams * HEAD_DIM), lambda b, h, i: (b, i, h)),
            pl.BlockSpec((1, t_len, kvs * HEAD_DIM), lambda b, h, i: (b, 0, h)),
            pl.BlockSpec((1, kvs, HEAD_DIM, t_len), lambda b, h, i: (b, h, 0, 0)),
        ],
        out_specs=pl.BlockSpec((1, MOBA_BLOCK, n_streams * HEAD_DIM), lambda b, h, i: (b, i, h)),
        out_shape=jax.ShapeDtypeStruct(q.shape, F32),
        scratch_shapes=[
            pltpu.VMEM((kvs, nblk_pad, HEAD_DIM), F32),
            pltpu.VMEM((n_streams, nblk, MOBA_BLOCK, MOBA_BLOCK), F32),
            pltpu.VMEM((n_streams, t_len, MOBA_BLOCK), BF16),
        ],
        compiler_params=pltpu.CompilerParams(
            dimension_semantics=("arbitrary", "arbitrary", "arbitrary"), vmem_limit_bytes=VMEM_LIMIT_BYTES),
        name="moba_prompt",
    )(q, k, vt)


def _moba_sample_kernel(pt_ref, q_ref, knew_ref, vnew_ref, ck_ref, cv_ref, o_ref,
                        kbuf, vbuf, sem, gate_ref, m_ref, l_ref, acc_ref, *, pages_per_step, n_blocks, n_new):
    j = pl.program_id(1)
    steps_per_seq = pl.num_programs(1)
    step = pl.program_id(0) * steps_per_seq + j
    n_steps = pl.num_programs(0) * steps_per_seq
    ahead = SAMPLE_PAGE_SLOTS - 1

    def page_copies(s, slot):
        out = []
        for i in range(pages_per_step):
            page = pt_ref[s * pages_per_step + i]
            out.append(pltpu.make_async_copy(ck_ref.at[page], kbuf.at[slot, i], sem.at[0, slot]))
            out.append(pltpu.make_async_copy(cv_ref.at[page], vbuf.at[slot, i], sem.at[1, slot]))
        return out

    @pl.when(step == 0)
    def _():
        for s in range(ahead):
            for c in page_copies(s, s):
                c.start()

    @pl.when(step + ahead < n_steps)
    def _():
        for c in page_copies(step + ahead, lax.rem(step + ahead, SAMPLE_PAGE_SLOTS)):
            c.start()

    slot = lax.rem(step, SAMPLE_PAGE_SLOTS)
    for c in page_copies(step, slot):
        c.wait()

    pages_per_block = MOBA_BLOCK // PAGE_SIZE
    blocks_per_step = pages_per_step // pages_per_block
    page_rows = PAGE_SIZE * KV_HEADS
    nkv = GROUP * n_new
    nrow = KV_HEADS * nkv
    lane = lax.broadcasted_iota(jnp.int32, (nrow, LANES), 1)
    scale = HEAD_DIM ** -0.5

    @pl.when(j == 0)
    def _():
        gate_ref[...] = jnp.zeros(gate_ref.shape, F32)
        m_ref[...] = jnp.zeros(m_ref.shape, F32)
        l_ref[...] = jnp.zeros(l_ref.shape, F32)

    qf = jnp.concatenate([q_ref[0, :, hq * HEAD_DIM:(hq + 1) * HEAD_DIM] for hq in range(Q_HEADS)], axis=0)
    qs = (qf * scale).astype(BF16)
    blk_cols = pages_per_block * page_rows
    col_head = lax.broadcasted_iota(jnp.int32, (nrow, blk_cols), 1) & (KV_HEADS - 1)
    row_head = lax.div(lax.broadcasted_iota(jnp.int32, (nrow, blk_cols), 0), nkv)
    same_head = col_head == row_head
    sub = SUBLANES

    blocks = range(blocks_per_step)
    scores, gcols = [], []
    for bi in blocks:
        kpages = [kbuf[slot, bi * pages_per_block + pp] for pp in range(pages_per_block)]
        scores.append(jnp.concatenate([_dot_nt(qs, kp.astype(BF16)) for kp in kpages], axis=1))
        ksum = sum(jnp.sum(kp.reshape(page_rows // sub, sub, HEAD_DIM), axis=0) for kp in kpages)
        mean = sum(ksum[i * KV_HEADS:(i + 1) * KV_HEADS] for i in range(sub // KV_HEADS)) * (1.0 / MOBA_BLOCK)
        mean_rows = jnp.concatenate(
            [jnp.broadcast_to(mean[h:h + 1], (nkv, HEAD_DIM)) for h in range(KV_HEADS)], axis=0)
        gcols.append(jnp.sum(qf * mean_rows, axis=-1, keepdims=True))
    probs, maxes, sums = [], [], []
    for bi in blocks:
        s = jnp.where(same_head, scores[bi], NEG_INF)
        mb = jnp.max(s, axis=-1, keepdims=True)
        p = jnp.exp(s - mb)
        maxes.append(mb)
        sums.append(jnp.sum(p, axis=-1, keepdims=True))
        probs.append(p.astype(BF16))
    gate, m_all, l_all = gate_ref[...], m_ref[...], l_ref[...]
    for bi in blocks:
        n = j * blocks_per_step + bi
        acc = _dot(probs[bi][:, :page_rows], vbuf[slot, bi * pages_per_block].astype(BF16))
        for pp in range(1, pages_per_block):
            acc = acc + _dot(probs[bi][:, pp * page_rows:(pp + 1) * page_rows],
                             vbuf[slot, bi * pages_per_block + pp].astype(BF16))
        acc_ref[n] = acc
        hot = lane == n
        gate = jnp.where(hot, gcols[bi], gate)
        m_all = jnp.where(hot, maxes[bi], m_all)
        l_all = jnp.where(hot, sums[bi], l_all)
    gate_ref[...] = gate
    m_ref[...] = m_all
    l_ref[...] = l_all

    @pl.when(j == pl.num_programs(1) - 1)
    def _():
        lane = lax.broadcasted_iota(jnp.int32, (nkv, LANES), 1)
        qrow_t = lax.rem(lax.broadcasted_iota(jnp.int32, (nkv, LANES), 0), n_new)
        lane_f = lane.astype(F32)
        for h in range(KV_HEADS):
            hd = slice(h * HEAD_DIM, (h + 1) * HEAD_DIM)
            rs = slice(h * nkv, (h + 1) * nkv)
            qsc = qf[rs] * scale
            gg = jnp.where(lane < n_blocks, gate_ref[rs, :], -jnp.inf)
            picked = jnp.zeros((nkv, LANES), F32)
            for _ in range(min(MOBA_TOPK, n_blocks)):
                mx = jnp.max(gg, axis=-1, keepdims=True)
                first = jnp.min(jnp.where(gg == mx, lane_f, float(LANES)), axis=-1, keepdims=True)
                hit = lane_f == first
                picked = jnp.where(hit, 1.0, picked)
                gg = jnp.where(hit, -jnp.inf, gg)
            sel = jnp.where(lane < n_blocks, picked, 0.0) > 0.0
            knew = knew_ref[0, :, hd]
            vnew = vnew_ref[0, :, hd]
            s_own = jnp.full((nkv, LANES), NEG_INF, F32)
            for jj in range(n_new):
                col = jnp.sum(qsc * knew[jj:jj + 1, :], axis=-1, keepdims=True)
                s_own = jnp.where(jnp.logical_and(lane == jj, qrow_t >= jj), col, s_own)
            mh = m_ref[rs, :]
            m_tot = jnp.maximum(jnp.max(s_own, axis=-1, keepdims=True),
                                jnp.max(jnp.where(sel, mh, NEG_INF), axis=-1, keepdims=True))
            w = jnp.where(sel, jnp.exp(mh - m_tot), 0.0)
            p_own = jnp.exp(s_own - m_tot)
            den = jnp.sum(w * l_ref[rs, :], axis=-1, keepdims=True) + jnp.sum(p_own, axis=-1, keepdims=True)
            num = jnp.zeros((nkv, HEAD_DIM), F32)
            for jj in range(n_new):
                num = num + p_own[:, jj:jj + 1] * vnew[jj:jj + 1, :]
            for n in range(n_blocks):
                num = num + w[:, n:n + 1] * acc_ref[n, rs, :]
            out = num / den
            for g in range(GROUP):
                o_ref[0, :, (h * GROUP + g) * HEAD_DIM:(h * GROUP + g + 1) * HEAD_DIM] = (
                    out[g * n_new:(g + 1) * n_new, :])


def _moba_sample(q, knew, vnew, cache_k, cache_v, page_table):
    db, ds, _ = q.shape
    n_pages = page_table.shape[1]
    pages_per_block = MOBA_BLOCK // PAGE_SIZE
    assert n_pages % pages_per_block == 0, "past length must be a whole number of MoBA blocks"
    n_blocks = n_pages // pages_per_block
    assert 0 < n_blocks <= LANES
    assert KV_HEADS & (KV_HEADS - 1) == 0 and 8 % KV_HEADS == 0
    pps = math.gcd(SAMPLE_PAGES_PER_STEP, n_pages)
    assert pps % pages_per_block == 0
    nrow = Q_HEADS * ds
    kdim = KV_HEADS * HEAD_DIM
    assert db * (n_pages // pps) >= SAMPLE_PAGE_SLOTS - 1
    page_buf = pltpu.VMEM((SAMPLE_PAGE_SLOTS, pps, PAGE_SIZE * KV_HEADS, HEAD_DIM), F32)

    per_seq = lambda w: pl.BlockSpec((1, ds, w), lambda b, j, pt: (b, 0, 0))
    hbm = pl.BlockSpec(memory_space=pl.ANY)
    grid_spec = pltpu.PrefetchScalarGridSpec(
        num_scalar_prefetch=1,
        grid=(db, n_pages // pps),
        in_specs=[per_seq(Q_HEADS * HEAD_DIM), per_seq(kdim), per_seq(kdim), hbm, hbm],
        out_specs=per_seq(Q_HEADS * HEAD_DIM),
        scratch_shapes=[
            page_buf, page_buf, pltpu.SemaphoreType.DMA((2, SAMPLE_PAGE_SLOTS)),
            pltpu.VMEM((nrow, LANES), F32),
            pltpu.VMEM((nrow, LANES), F32),
            pltpu.VMEM((nrow, LANES), F32),
            pltpu.VMEM((n_blocks, nrow, HEAD_DIM), F32),
        ],
    )
    return pl.pallas_call(
        functools.partial(_moba_sample_kernel, pages_per_step=pps, n_blocks=n_blocks, n_new=ds),
        grid_spec=grid_spec,
        out_shape=jax.ShapeDtypeStruct(q.shape, F32),
        compiler_params=pltpu.CompilerParams(
            dimension_semantics=("arbitrary", "arbitrary"), vmem_limit_bytes=VMEM_LIMIT_BYTES),
        name="moba_sample",
    )(page_table.reshape(-1), q, knew, vnew, cache_k, cache_v)


def _rope_tables(pos):
    inv = ROPE_THETA ** (-jnp.arange(0, HEAD_DIM, 2, dtype=F32) / HEAD_DIM)
    ang = pos.astype(F32)[:, None] * inv[None, :]
    cos, sin = jnp.cos(ang), jnp.sin(ang)
    return jnp.concatenate([cos, cos], axis=-1), jnp.concatenate([-sin, sin], axis=-1)


def kernel(x_prompt, x_sample, state_gla, cache_k, cache_v, page_table, w_in_a, w_g2, b_g, g_gla_out, w_o_a,
           norm_kv, w_kv, g_k, w_q_b, g_q, w_o_b, norm_mix, norm_mlp, w_up, w_down):
    bsz, t_len, _ = x_prompt.shape
    db, ds, _ = x_sample.shape
    n_a = state_gla.shape[0]
    depth = norm_mix.shape[0]
    assert n_a == 1 and depth == 2, "one GLA layer followed by one MoBA layer"
    assert t_len % PROMPT_ROW_TILE == 0 and t_len % PROMPT_GLA_CHUNK == 0 and t_len % MOBA_BLOCK == 0
    past = page_table.shape[1] * PAGE_SIZE
    kdim = KV_HEADS * HEAD_DIM
    row2 = lambda a: a.reshape(1, -1)

    main = 2 * GLA_KEY_DIM + 2 * GLA_VALUE_DIM
    wmain = w_in_a[0, :, :main].astype(BF16)
    wglr = jnp.pad(w_in_a[0, :, main:], ((0, 0), (0, LANES - GATE_RANK))).astype(BF16)
    wg2 = jnp.pad(w_g2[0], ((0, LANES - GATE_RANK), (0, 0))).astype(BF16)
    gla_w = (row2(norm_mix[0]), wmain, wglr, wg2, row2(b_g[0]), row2(g_gla_out[0]), w_o_a[0].astype(BF16))
    s0_p = jnp.zeros((bsz,) + state_gla.shape[2:], F32)
    seq_tile = math.gcd(SAMPLE_SEQ_TILE, db)
    hp, st_p = _gla_pipe(x_prompt, s0_p, *gla_w, seqs=math.gcd(PROMPT_GLA_SEQS, bsz), chunk=PROMPT_GLA_CHUNK)
    hs, st_s = _gla_pipe(x_sample, state_gla[0], *gla_w, seqs=seq_tile, chunk=ds)
    hp = hp.reshape(bsz * t_len, D_MODEL)
    hs = hs.reshape(db * ds, D_MODEL)
    wup = w_up.astype(BF16)
    wdn = w_down.astype(BF16)
    hp = _mlp(hp, row2(norm_mlp[0]), wup[0], wdn[0])
    hs = _mlp(hs, row2(norm_mlp[0]), wup[0], wdn[0])

    cos_p, sin_p = _rope_tables(jnp.arange(t_len, dtype=jnp.int32))
    cos_s, sin_s = _rope_tables(past + jnp.arange(ds, dtype=jnp.int32))
    cos_s, sin_s = jnp.tile(cos_s, (db, 1)), jnp.tile(sin_s, (db, 1))
    kvq_w = (row2(norm_kv), row2(norm_mix[1]), w_kv.astype(BF16), w_q_b[0].astype(BF16), row2(g_k), row2(g_q[0]))
    k_p, q_p, k4_p, v4_p, vt_p = _kvq(hp, *kvq_w, cos_p, sin_p, tm=PROMPT_ROW_TILE, seq_len=t_len)
    k_s, q_s, k4_s, v4_s, v_s = _kvq(hs, *kvq_w, cos_s, sin_s, tm=db * ds)

    a_p = _moba_prompt(q_p.reshape(bsz, t_len, -1), k_p.reshape(bsz, t_len, kdim), vt_p)
    pages = lambda c: c.reshape(c.shape[0], PAGE_SIZE * KV_HEADS, HEAD_DIM)
    a_s = _moba_sample(q_s.reshape(db, ds, -1), k_s.reshape(db, ds, kdim), v_s.reshape(db, ds, kdim),
                       pages(cache_k), pages(cache_v), page_table)
    wob = w_o_b[0].astype(BF16)
    hp = _mlp(hp, row2(norm_mlp[1]), wup[1], wdn[1], attn=a_p.reshape(bsz * t_len, -1), wo=wob)
    hs = _mlp(hs, row2(norm_mlp[1]), wup[1], wdn[1], attn=a_s.reshape(db * ds, -1), wo=wob)

    kv4 = lambda a, n, t: a.reshape(n, t, KV_HEADS, HEAD_DIM)
    return (hp.reshape(bsz, t_len, D_MODEL), hs.reshape(db, ds, D_MODEL), st_p[None], st_s[None],
            kv4(k4_p, bsz, t_len), kv4(v4_p, bsz, t_len), kv4(k4_s, db, ds), kv4(v4_s, db, ds))
```
